```python
import jax, jax.numpy as jnp
from jax import lax
import numpy as np

D_MODEL = 1024
BATCH = 1
SEQ = 16384
DEPTH = 1

PLE_DIM = 256
RWKV_WIDTH = 512
RWKV_HEAD = 64
RWKV_HEADS = RWKV_WIDTH // RWKV_HEAD
DECAY_LORA = 64
AAA_LORA = 64
GATE_LORA = 128
GMLP_WIDTH = D_MODEL - RWKV_WIDTH
GMLP_HEADS = 8
GMLP_HEAD = GMLP_WIDTH // GMLP_HEADS
CHUNK = 128
D_FF = 2816
RMS_EPS = 1e-6
LN_EPS = 1e-5
GN_EPS = 64e-5
RWKV_COLS = 3 * RWKV_WIDTH + DECAY_LORA + AAA_LORA + GATE_LORA
IN_COLS = RWKV_COLS + 2 * GMLP_WIDTH
RWKV_SPLITS = (RWKV_WIDTH, 2 * RWKV_WIDTH, 3 * RWKV_WIDTH,
               3 * RWKV_WIDTH + DECAY_LORA, 3 * RWKV_WIDTH + DECAY_LORA + AAA_LORA)

kernel_name = 'hybrid_rwkv7_chunked_gmlp_macaron_ple'


def rms_norm(x, g):
    xf = x.astype(jnp.float32)
    y = xf * lax.rsqrt(jnp.mean(xf * xf, axis=-1, keepdims=True) + RMS_EPS)
    return (y * g.astype(jnp.float32)).astype(x.dtype)


def layer_norm(x, g, b, eps):
    xf = x.astype(jnp.float32)
    mu = jnp.mean(xf, axis=-1, keepdims=True)
    var = jnp.mean(jnp.square(xf - mu), axis=-1, keepdims=True)
    y = (xf - mu) * lax.rsqrt(var + eps)
    return (y * g.astype(jnp.float32) + b.astype(jnp.float32)).astype(x.dtype)


def swiglu(x, w1, w3, w2):
    return (jax.nn.silu(x @ w1) * (x @ w3)) @ w2


def token_shift(z, mu):
    z_prev = jnp.pad(z[:, :-1], ((0, 0), (1, 0), (0, 0)))
    return z + (z_prev - z) * mu


def rwkv7_recurrence(r, decay, k, v, kk, a):
    B, T, H, N = r.shape

    def step(S, inp):
        r_t, d_t, k_t, v_t, kk_t, a_t = inp
        sa = jnp.einsum('bhvk,bhk->bhv', S, -kk_t)
        S = (S * d_t[:, :, None, :]
             + sa[..., None] * (kk_t * a_t)[:, :, None, :]
             + v_t[..., None] * k_t[:, :, None, :])
        y_t = jnp.einsum('bhvk,bhk->bhv', S, r_t)
        return S, y_t

    xs = (jnp.moveaxis(r, 1, 0), jnp.moveaxis(decay, 1, 0), jnp.moveaxis(k, 1, 0),
          jnp.moveaxis(v, 1, 0), jnp.moveaxis(kk, 1, 0), jnp.moveaxis(a, 1, 0))
    S0 = jnp.zeros((B, H, N, N), jnp.float32)
    _, y = lax.scan(step, S0, xs)
    return jnp.moveaxis(y, 0, 1)


def rwkv7_group(z, mu, w0, w_decay, a0, w_aaa, w_gate, k_k, k_a, r_k, gn_g, gn_b):
    B, T, _ = z.shape
    H, N = RWKV_HEADS, RWKV_HEAD
    z = token_shift(z, mu)
    r, k, v, zw, za, zg = jnp.split(z, RWKV_SPLITS, axis=-1)
    logw = -jax.nn.softplus(-(w0 + jnp.tanh(zw) @ w_decay)) - 0.5
    decay = jnp.exp(-jnp.exp(logw.astype(jnp.float32)))
    a = jax.nn.sigmoid(a0 + za @ w_aaa)
    g = jax.nn.sigmoid(zg) @ w_gate
    kk = (k * k_k).reshape(B, T, H, N).astype(jnp.float32)
    kk = kk / jnp.maximum(jnp.linalg.norm(kk, axis=-1, keepdims=True), 1e-12)
    k = k * (1.0 + (a - 1.0) * k_a)
    hd = lambda t: t.reshape(B, T, H, N).astype(jnp.float32)
    r4, k4, v4, a4, d4 = hd(r), hd(k), hd(v), hd(a), hd(decay)
    y = rwkv7_recurrence(r4, d4, k4, v4, kk, a4)
    y = layer_norm(y, gn_g.reshape(H, N), gn_b.reshape(H, N), GN_EPS)
    y = y + jnp.sum(r4 * k4 * r_k.astype(jnp.float32), axis=-1, keepdims=True) * v4
    return (y.reshape(B, T, RWKV_WIDTH) * g).astype(z.dtype)


def chunked_sgu_group(z, ln_g, ln_b, w_s, b_s):
    B, T, _ = z.shape
    u, v = jnp.split(jax.nn.gelu(z, approximate=False), 2, axis=-1)
    v = layer_norm(v, ln_g, ln_b, LN_EPS)
    v = v.reshape(B, T // CHUNK, CHUNK, GMLP_HEADS, GMLP_HEAD)
    causal = jnp.tril(jnp.ones((CHUNK, CHUNK), dtype=bool))
    ws = jnp.where(causal[None], w_s, jnp.zeros_like(w_s))
    mixed = jnp.einsum('hts,bcshd->bcthd', ws, v) + b_s.T[None, None, :, :, None]
    return u * mixed.reshape(B, T, GMLP_WIDTH)


def setup_inputs(seed: int = 0) -> dict:
    key = jax.random.key(seed)
    ks = jax.random.split(key, 40)
    L, D = DEPTH, D_MODEL

    def nrm(k, shape, scale):
        return jax.random.normal(k, shape, jnp.float32) * scale

    def gain(k, shape):
        return 1.0 + nrm(k, shape, 0.05)

    return {
        'x': nrm(ks[0], (BATCH, SEQ, D), 1.0),
        'p': nrm(ks[1], (DEPTH, BATCH, SEQ, PLE_DIM), 1.0),
        'norm_ffn1': gain(ks[2], (L, D)),
        'ffn1_w1': nrm(ks[3], (L, D, D_FF), D ** -0.5),
        'ffn1_w3': nrm(ks[4], (L, D, D_FF), D ** -0.5),
        'ffn1_w2': nrm(ks[5], (L, D_FF, D), D_FF ** -0.5),
        'norm_mix': gain(ks[6], (L, D)),
        'w_in': nrm(ks[7], (L, D, IN_COLS), D ** -0.5),
        'shift_mu': jax.random.uniform(ks[8], (L, RWKV_COLS), jnp.float32),
        'rwkv_w0': jax.random.uniform(ks[9], (L, RWKV_WIDTH), jnp.float32, -6.5, -1.5),
        'rwkv_w_decay': nrm(ks[10], (L, DECAY_LORA, RWKV_WIDTH), 0.1 * DECAY_LORA ** -0.5),
        'rwkv_a0': nrm(ks[11], (L, RWKV_WIDTH), 0.1),
        'rwkv_w_aaa': nrm(ks[12], (L, AAA_LORA, RWKV_WIDTH), AAA_LORA ** -0.5),
        'rwkv_w_gate': nrm(ks[13], (L, GATE_LORA, RWKV_WIDTH), GATE_LORA ** -0.5),
        'rwkv_k_k': 0.85 + nrm(ks[14], (L, RWKV_WIDTH), 0.05),
        'rwkv_k_a': 1.0 + nrm(ks[15], (L, RWKV_WIDTH), 0.05),
        'rwkv_r_k': nrm(ks[16], (L, RWKV_HEADS, RWKV_HEAD), 0.1),
        'rwkv_gn_g': gain(ks[17], (L, RWKV_WIDTH)),
        'rwkv_gn_b': nrm(ks[18], (L, RWKV_WIDTH), 0.02),
        'sgu_ln_g': gain(ks[19], (L, GMLP_WIDTH)),
        'sgu_ln_b': nrm(ks[20], (L, GMLP_WIDTH), 0.02),
        'sgu_w_s': nrm(ks[21], (L, GMLP_HEADS, CHUNK, CHUNK), 0.5 * CHUNK ** -0.5),
        'sgu_b_s': 1.0 + nrm(ks[22], (L, GMLP_HEADS, CHUNK), 0.1),
        'w_out': nrm(ks[23], (L, D, D), D ** -0.5),
        'norm_ffn2': gain(ks[24], (L, D)),
        'ffn2_w1': nrm(ks[25], (L, D, D_FF), D ** -0.5),
        'ffn2_w3': nrm(ks[26], (L, D, D_FF), D ** -0.5),
        'ffn2_w2': nrm(ks[27], (L, D_FF, D), D_FF ** -0.5),
        'norm_ple': gain(ks[28], (L, D)),
        'w_ple_gate': nrm(ks[29], (L, D, D), D ** -0.5),
        'w_ple': nrm(ks[30], (L, PLE_DIM, D), PLE_DIM ** -0.5),
        'norm_final': gain(ks[31], (D,)),
    }


def reference(x, p, norm_ffn1, ffn1_w1, ffn1_w3, ffn1_w2, norm_mix, w_in, shift_mu,
              rwkv_w0, rwkv_w_decay, rwkv_a0, rwkv_w_aaa, rwkv_w_gate, rwkv_k_k, rwkv_k_a,
              rwkv_r_k, rwkv_gn_g, rwkv_gn_b, sgu_ln_g, sgu_ln_b, sgu_w_s, sgu_b_s, w_out,
              norm_ffn2, ffn2_w1, ffn2_w3, ffn2_w2, norm_ple, w_ple_gate, w_ple, norm_final):
    h = x
    for i in range(DEPTH):
        h = h + 0.5 * swiglu(rms_norm(h, norm_ffn1[i]), ffn1_w1[i], ffn1_w3[i], ffn1_w2[i])
        n = rms_norm(h, norm_mix[i])
        z = n @ w_in[i]
        y_rwkv = rwkv7_group(z[..., :RWKV_COLS], shift_mu[i], rwkv_w0[i], rwkv_w_decay[i],
                             rwkv_a0[i], rwkv_w_aaa[i], rwkv_w_gate[i], rwkv_k_k[i],
                             rwkv_k_a[i], rwkv_r_k[i], rwkv_gn_g[i], rwkv_gn_b[i])
        y_sgu = chunked_sgu_group(z[..., RWKV_COLS:], sgu_ln_g[i], sgu_ln_b[i],
                                  sgu_w_s[i], sgu_b_s[i])
        y = jnp.concatenate([y_rwkv, y_sgu], axis=-1)
        h = h + y @ w_out[i]
        h = h + 0.5 * swiglu(rms_norm(h, norm_ffn2[i]), ffn2_w1[i], ffn2_w3[i], ffn2_w2[i])
        gate = jax.nn.sigmoid(rms_norm(h, norm_ple[i]) @ w_ple_gate[i])
        h = h + gate * (p[i] @ w_ple[i])
    return rms_norm(h, norm_final)
```

```python
import jax
import jax.numpy as jnp
from jax import lax
from jax.experimental import pallas as pl
from jax.experimental.pallas import tpu as pltpu

F32 = jnp.float32
BF16 = jnp.bfloat16

D_MODEL = 1024
D_FF = 2816
PLE_DIM = 256
RWKV_WIDTH = 512
RWKV_COLS = 1792
GMLP_WIDTH = 512
IN_COLS = 2816
HEAD = 64
LANES = 128
N_PAIRS = RWKV_WIDTH // LANES
CHUNK = 128
RWKV_CHUNK = 128
INV_BASE = 16
RMS_EPS = 1e-6
LN_EPS = 1e-5
GN_EPS = 64e-5
VMEM_LIMIT = 56 * 1024 * 1024

_NN = (((1,), (0,)), ((), ()))
_NT = (((1,), (1,)), ((), ()))
_TN = (((0,), (0,)), ((), ()))


def _dot(a, b, dn=_NN):
    return lax.dot_general(a, b, dn, preferred_element_type=F32)


def _mm1(a, b, dn=_NN):
    return _dot(a.astype(BF16), b.astype(BF16), dn)


def _split2(x):
    hi = x.astype(BF16)
    lo = (x - hi.astype(F32)).astype(BF16)
    return hi, lo


def _split3(x):
    hi = x.astype(BF16)
    r1 = x - hi.astype(F32)
    mid = r1.astype(BF16)
    lo = (r1 - mid.astype(F32)).astype(BF16)
    return hi, mid, lo


def _mm3(a, b, dn=_NN):
    ah, al = _split2(a)
    bh, bl = _split2(b)
    return _dot(ah, bh, dn) + (_dot(ah, bl, dn) + _dot(al, bh, dn))


def _mm_exact_rhs(a, b_bf16, dn=_NN):
    hi, mid, lo = _split3(a)
    return _dot(hi, b_bf16, dn) + (_dot(mid, b_bf16, dn) + _dot(lo, b_bf16, dn))


def _mm_exact_lhs(a_bf16, b, dn=_NN):
    hi, mid, lo = _split3(b)
    return _dot(a_bf16, hi, dn) + (_dot(a_bf16, mid, dn) + _dot(a_bf16, lo, dn))


def _rms(x, g):
    ms = jnp.mean(x * x, axis=-1, keepdims=True)
    return x * lax.rsqrt(ms + RMS_EPS) * g


def _ffn_kernel(h_ref, g_ref, w1_ref, w3_ref, w2_ref, o_ref, xn_ref, acc_ref):
    j = pl.program_id(1)

    @pl.when(j == 0)
    def _():
        xn_ref[...] = _rms(h_ref[...], g_ref[...]).astype(BF16)
        acc_ref[...] = jnp.zeros_like(acc_ref)

    xn = xn_ref[...]
    a = _dot(xn, w1_ref[...])
    b = _dot(xn, w3_ref[...])
    hm = (a * jax.nn.sigmoid(a) * b).astype(BF16)
    acc_ref[...] += _dot(hm, w2_ref[...])

    @pl.when(j == pl.num_programs(1) - 1)
    def _():
        o_ref[...] = h_ref[...] + 0.5 * acc_ref[...]


def _ffn(h, g, w1, w3, w2, *, tm=1024, tf=256):
    t, d = h.shape
    f = w1.shape[1]
    return pl.pallas_call(
        _ffn_kernel,
        out_shape=jax.ShapeDtypeStruct((t, d), F32),
        grid=(t // tm, f // tf),
        in_specs=[
            pl.BlockSpec((tm, d), lambda i, j: (i, 0)),
            pl.BlockSpec((1, d), lambda i, j: (0, 0)),
            pl.BlockSpec((d, tf), lambda i, j: (0, j)),
            pl.BlockSpec((d, tf), lambda i, j: (0, j)),
            pl.BlockSpec((tf, d), lambda i, j: (j, 0)),
        ],
        out_specs=pl.BlockSpec((tm, d), lambda i, j: (i, 0)),
        scratch_shapes=[pltpu.VMEM((tm, d), BF16), pltpu.VMEM((tm, d), F32)],
        compiler_params=pltpu.CompilerParams(
            dimension_semantics=("arbitrary", "arbitrary"), vmem_limit_bytes=VMEM_LIMIT),
        name="ffn",
    )(h, g, w1, w3, w2)


def _norm_matmul_kernel(h_ref, g_ref, w_ref, o_ref, xn_ref):
    @pl.when(pl.program_id(1) == 0)
    def _():
        xn_ref[...] = _rms(h_ref[...], g_ref[...]).astype(BF16)

    o_ref[...] = _dot(xn_ref[...], w_ref[...])


def _norm_matmul(h, g, w, *, tm=1024, tn=256):
    t, d = h.shape
    n = w.shape[1]
    return pl.pallas_call(
        _norm_matmul_kernel,
        out_shape=jax.ShapeDtypeStruct((t, n), F32),
        grid=(t // tm, n // tn),
        in_specs=[
            pl.BlockSpec((tm, d), lambda i, j: (i, 0)),
            pl.BlockSpec((1, d), lambda i, j: (0, 0)),
            pl.BlockSpec((d, tn), lambda i, j: (0, j)),
        ],
        out_specs=pl.BlockSpec((tm, tn), lambda i, j: (i, j)),
        scratch_shapes=[pltpu.VMEM((tm, d), BF16)],
        compiler_params=pltpu.CompilerParams(
            dimension_semantics=("arbitrary", "arbitrary"), vmem_limit_bytes=VMEM_LIMIT),
        name="norm_matmul",
    )(h, g, w)


def _out_proj_kernel(h_ref, y_ref, w_ref, o_ref):
    o_ref[...] = h_ref[...] + _dot(y_ref[...].astype(BF16), w_ref[...])


def _out_proj(h, y, w, *, tm=1024):
    t, d = h.shape
    return pl.pallas_call(
        _out_proj_kernel,
        out_shape=jax.ShapeDtypeStruct((t, d), F32),
        grid=(t // tm,),
        in_specs=[
            pl.BlockSpec((tm, d), lambda i: (i, 0)),
            pl.BlockSpec((tm, d), lambda i: (i, 0)),
            pl.BlockSpec((d, d), lambda i: (0, 0)),
        ],
        out_specs=pl.BlockSpec((tm, d), lambda i: (i, 0)),
        compiler_params=pltpu.CompilerParams(
            dimension_semantics=("arbitrary",), vmem_limit_bytes=VMEM_LIMIT),
        name="out_proj",
    )(h, y, w)


def _ple_final_kernel(h_ref, p_ref, gp_ref, wg_ref, wp_ref, gf_ref, o_ref):
    h = h_ref[...]
    gate = jax.nn.sigmoid(_dot(_rms(h, gp_ref[...]).astype(BF16), wg_ref[...]))
    e = _dot(p_ref[...].astype(BF16), wp_ref[...])
    o_ref[...] = _rms(h + gate * e, gf_ref[...])


def _ple_final(h, p, gp, wg, wp, gf, *, tm=512):
    t, d = h.shape
    pd = p.shape[1]
    return pl.pallas_call(
        _ple_final_kernel,
        out_shape=jax.ShapeDtypeStruct((t, d), F32),
        grid=(t // tm,),
        in_specs=[
            pl.BlockSpec((tm, d), lambda i: (i, 0)),
            pl.BlockSpec((tm, pd), lambda i: (i, 0)),
            pl.BlockSpec((1, d), lambda i: (0, 0)),
            pl.BlockSpec((d, d), lambda i: (0, 0)),
            pl.BlockSpec((pd, d), lambda i: (0, 0)),
            pl.BlockSpec((1, d), lambda i: (0, 0)),
        ],
        out_specs=pl.BlockSpec((tm, d), lambda i: (i, 0)),
        compiler_params=pltpu.CompilerParams(
            dimension_semantics=("arbitrary",), vmem_limit_bytes=VMEM_LIMIT),
        name="ple_final",
    )(h, p, gp, wg, wp, gf)


def _iota2(shape, dim):
    return lax.broadcasted_iota(jnp.int32, shape, dim)


def _unit_lower_inverse(a, n):
    row = _iota2((n, n), 0)
    col = _iota2((n, n), 1)

    def same_block(b):
        s = b.bit_length() - 1
        return (row >> s) == (col >> s)

    ad = jnp.where(same_block(INV_BASE), a, 0.0)
    t = jnp.where(row == col, 1.0, 0.0) + ad
    p = ad
    steps = INV_BASE.bit_length() - 2
    for _ in range(steps):
        p = _mm3(p, p)
        t = t + _mm3(t, p)
    b = INV_BASE
    while b < n:
        am = jnp.where(same_block(2 * b) & jnp.logical_not(same_block(b)), a, 0.0)
        t = t + _mm3(_mm3(t, am), t)
        b *= 2
    return t


def _rwkv_chunk(r, k2, v, kkn, bvec, logd, rkk, g, gn_g, gn_b, state_ref):
    c = r.shape[0]
    tril_incl = _iota2((c, c), 0) >= _iota2((c, c), 1)
    tril_strict = _iota2((c, c), 0) > _iota2((c, c), 1)
    lcum = _mm_exact_lhs(tril_incl.astype(BF16), logd)
    lmid = lcum[c // 2 - 1:c // 2, :]
    lend = lcum[c - 1:c, :]
    lexc = lcum - logd
    e_mid_from = jnp.exp(lmid - lcum)
    rt = r * jnp.exp(lcum - lmid)
    at = -kkn * jnp.exp(lexc - lmid)
    bt = bvec * e_mid_from
    kt = k2 * e_mid_from
    r0 = r * jnp.exp(lcum)
    a0 = -kkn * jnp.exp(lexc)
    e_end = jnp.exp(lend - lcum)
    bh = bvec * e_end
    kh = k2 * e_end
    d_end = jnp.exp(lend)

    lane = _iota2((c, LANES), 1)
    first = lane < HEAD
    blockdiag = (_iota2((LANES, LANES), 0) >= HEAD) == (_iota2((LANES, LANES), 1) >= HEAD)
    ones_bd = blockdiag.astype(BF16)
    avg_bd = (blockdiag.astype(F32) * (1.0 / HEAD)).astype(BF16)

    outs = []
    for p in range(N_PAIRS):
        sl = slice(p * LANES, (p + 1) * LANES)
        rt_p, at_p, bt_p, kt_p = rt[:, sl], at[:, sl], bt[:, sl], kt[:, sl]
        v_p = v[:, sl]
        rhs = jnp.concatenate([bt_p, kt_p], axis=0)
        a_ak, a_r, t_inv = [], [], []
        for hh in range(2):
            m = first if hh == 0 else jnp.logical_not(first)
            lhs = jnp.concatenate([jnp.where(m, at_p, 0.0), jnp.where(m, rt_p, 0.0)], axis=0)
            gram = _mm3(lhs, rhs, _NT)
            a_ab = jnp.where(tril_strict, gram[:c, :c], 0.0)
            a_ak.append(jnp.where(tril_strict, gram[:c, c:], 0.0))
            a_r.append(jnp.concatenate(
                [jnp.where(tril_incl, gram[c:, :c], 0.0),
                 jnp.where(tril_incl, gram[c:, c:], 0.0)], axis=1))
            t_inv.append(_unit_lower_inverse(a_ab, c))
        s = state_ref[p]
        sm = _mm3(jnp.concatenate([a0[:, sl], r0[:, sl]], axis=0), s, _NT)
        x = sm[:c] + jnp.where(first, _mm3(a_ak[0], v_p), _mm3(a_ak[1], v_p))
        u = jnp.where(first, _mm3(t_inv[0], x), _mm3(t_inv[1], x))
        uv = jnp.concatenate([u, v_p], axis=0)
        y = sm[c:] + jnp.where(first, _mm3(a_r[0], uv), _mm3(a_r[1], uv))
        upd = _mm3(uv, jnp.concatenate([bh[:, sl], kh[:, sl]], axis=0), _TN)
        state_ref[p] = s * d_end[:, sl] + jnp.where(blockdiag, upd, 0.0)
        mean = _mm_exact_rhs(y, avg_bd)
        yc = y - mean
        var = _mm_exact_rhs(yc * yc, avg_bd)
        yn = yc * lax.rsqrt(var + GN_EPS) * gn_g[:, sl] + gn_b[:, sl]
        bonus = _mm_exact_rhs(rkk[:, sl], ones_bd) * v_p
        outs.append((yn + bonus) * g[:, sl])
    return outs


def _mixer_kernel(z_ref, mu_ref, w0_ref, wdec_ref, a0_ref, waaa_ref, wgate_ref, kk_ref,
                  ka_ref, rk_ref, gng_ref, gnb_ref, lng_ref, lnb_ref, ws_ref, bs_ref,
                  o_ref, state_ref, prev_ref, wst_ref):
    @pl.when(pl.program_id(0) == 0)
    def _():
        state_ref[...] = jnp.zeros_like(state_ref)
        prev_ref[...] = jnp.zeros_like(prev_ref)
        causal = _iota2((CHUNK, CHUNK), 0) >= _iota2((CHUNK, CHUNK), 1)
        for h in range(ws_ref.shape[0]):
            wst_ref[h] = jnp.where(causal, ws_ref[h], 0.0).astype(BF16)

    zr = z_ref[:, :RWKV_COLS]
    zp = pltpu.roll(zr, 1, 0)
    zp = jnp.where(_iota2(zr.shape, 0) == 0, prev_ref[...], zp)
    prev_ref[...] = zr[CHUNK - 1:CHUNK, :]
    zs = zr + (zp - zr) * mu_ref[...]
    w = RWKV_WIDTH
    r, k, v = zs[:, 0:w], zs[:, w:2 * w], zs[:, 2 * w:3 * w]
    zwa = zs[:, 3 * w:3 * w + LANES]
    zg = zs[:, 3 * w + LANES:]
    lw = w0_ref[...] + _mm1(jnp.tanh(zwa), wdec_ref[...])
    nlw = -lw
    softplus = jnp.maximum(nlw, 0.0) + jnp.log1p(jnp.exp(-jnp.abs(nlw)))
    logd = -jnp.exp(-softplus - 0.5)
    a = jax.nn.sigmoid(a0_ref[...] + _mm1(zwa, waaa_ref[...]))
    g = _mm1(jax.nn.sigmoid(zg), wgate_ref[...])
    kk = k * kk_ref[...]
    blockdiag = (_iota2((LANES, LANES), 0) >= HEAD) == (_iota2((LANES, LANES), 1) >= HEAD)
    ones_bd = blockdiag.astype(BF16)
    kk2 = kk * kk
    n2 = jnp.concatenate(
        [_mm_exact_rhs(kk2[:, p * LANES:(p + 1) * LANES], ones_bd) for p in range(N_PAIRS)],
        axis=1)
    kkn = kk / jnp.maximum(jnp.sqrt(n2), 1e-12)
    k2 = k * (1.0 + (a - 1.0) * ka_ref[...])
    bvec = kkn * a
    rkk = r * k2 * rk_ref[...]
    for sub in range(CHUNK // RWKV_CHUNK):
        rs = slice(sub * RWKV_CHUNK, (sub + 1) * RWKV_CHUNK)
        outs = _rwkv_chunk(r[rs], k2[rs], v[rs], kkn[rs], bvec[rs], logd[rs], rkk[rs], g[rs],
                           gng_ref[...], gnb_ref[...], state_ref)
        for p, o in enumerate(outs):
            o_ref[rs, p * LANES:(p + 1) * LANES] = o

    zg2 = z_ref[:, RWKV_COLS:]
    ge = 0.5 * zg2 * (1.0 + lax.erf(zg2 * 0.7071067811865476))
    u, vv = ge[:, :GMLP_WIDTH], ge[:, GMLP_WIDTH:]
    m = jnp.mean(vv, axis=-1, keepdims=True)
    vc = vv - m
    var = jnp.mean(vc * vc, axis=-1, keepdims=True)
    vn = (vc * lax.rsqrt(var + LN_EPS) * lng_ref[...] + lnb_ref[...]).astype(BF16)
    first = _iota2((CHUNK, LANES), 1) < HEAD
    for p in range(GMLP_WIDTH // LANES):
        sl = slice(p * LANES, (p + 1) * LANES)
        vp = vn[:, sl]
        mixed = jnp.where(first, _dot(wst_ref[2 * p], vp), _dot(wst_ref[2 * p + 1], vp))
        o_ref[:, RWKV_WIDTH + p * LANES:RWKV_WIDTH + (p + 1) * LANES] = (
            u[:, sl] * (mixed + bs_ref[:, sl]))


def _mixer(z, mu, w0, wdec, a0, waaa, wgate, k_k, k_a, r_k, gn_g, gn_b, ln_g, ln_b, w_s, b_s):
    t = z.shape[0]
    n_heads = w_s.shape[0]

    def full(arr):
        nd = arr.ndim
        return pl.BlockSpec(arr.shape, lambda i: (0,) * nd)

    params = (mu, w0, wdec, a0, waaa, wgate, k_k, k_a, r_k, gn_g, gn_b, ln_g, ln_b, w_s, b_s)
    return pl.pallas_call(
        _mixer_kernel,
        out_shape=jax.ShapeDtypeStruct((t, D_MODEL), F32),
        grid=(t // CHUNK,),
        in_specs=[pl.BlockSpec((CHUNK, IN_COLS), lambda i: (i, 0))] + [full(a) for a in params],
        out_specs=pl.BlockSpec((CHUNK, D_MODEL), lambda i: (i, 0)),
        scratch_shapes=[
            pltpu.VMEM((N_PAIRS, LANES, LANES), F32),
            pltpu.VMEM((1, RWKV_COLS), F32),
            pltpu.VMEM((n_heads, CHUNK, CHUNK), BF16),
        ],
        compiler_params=pltpu.CompilerParams(
            dimension_semantics=("arbitrary",), vmem_limit_bytes=VMEM_LIMIT),
        name="mixer",
    )(z, *params)


def kernel(x, p, norm_ffn1, ffn1_w1, ffn1_w3, ffn1_w2, norm_mix, w_in, shift_mu, rwkv_w0,
           rwkv_w_decay, rwkv_a0, rwkv_w_aaa, rwkv_w_gate, rwkv_k_k, rwkv_k_a, rwkv_r_k,
           rwkv_gn_g, rwkv_gn_b, sgu_ln_g, sgu_ln_b, sgu_w_s, sgu_b_s, w_out, norm_ffn2,
           ffn2_w1, ffn2_w3, ffn2_w2, norm_ple, w_ple_gate, w_ple, norm_final):
    assert x.shape[0] == 1 and p.shape[0] == 1, "one batch row and one layer, as the problem states"
    row = lambda a: a.reshape(1, -1)
    bf = lambda a: a.astype(BF16)
    h = x[0]
    h = _ffn(h, row(norm_ffn1[0]), bf(ffn1_w1[0]), bf(ffn1_w3[0]), bf(ffn1_w2[0]))
    z = _norm_matmul(h, row(norm_mix[0]), bf(w_in[0]))
    lora = rwkv_w_decay.shape[1]
    pad = jnp.zeros((LANES - lora, RWKV_WIDTH), F32)
    wdec = jnp.concatenate([rwkv_w_decay[0], pad], axis=0)
    waaa = jnp.concatenate([pad, rwkv_w_aaa[0]], axis=0)
    bias = jnp.repeat(sgu_b_s[0].T, HEAD, axis=1)
    y = _mixer(z, row(shift_mu[0]), row(rwkv_w0[0]), bf(wdec), row(rwkv_a0[0]), bf(waaa),
               bf(rwkv_w_gate[0]), row(rwkv_k_k[0]), row(rwkv_k_a[0]), row(rwkv_r_k[0]),
               row(rwkv_gn_g[0]), row(rwkv_gn_b[0]), row(sgu_ln_g[0]), row(sgu_ln_b[0]),
               sgu_w_s[0], bias)
    h = _out_proj(h, y, bf(w_out[0]))
    h = _ffn(h, row(norm_ffn2[0]), bf(ffn2_w1[0]), bf(ffn2_w3[0]), bf(ffn2_w2[0]))
    h = _ple_final(h, p[0, 0], row(norm_ple[0]), bf(w_ple_gate[0]), bf(w_ple[0]), row(norm_final))
    return h[None]
```

```python
import jax
import jax.numpy as jnp
from jax import lax
from jax.experimental import pallas as pl
from jax.experimental.pallas import tpu as pltpu

F32 = jnp.float32
BF16 = jnp.bfloat16

D_MODEL = 1024
D_FF = 2816
PLE_DIM = 256
RWKV_WIDTH = 512
RWKV_COLS = 1792
GMLP_WIDTH = 512
IN_COLS = 2816
HEAD = 64
LANES = 128
N_PAIRS = RWKV_WIDTH // LANES
CHUNK = 128
RWKV_CHUNK = 128
INV_BASE = 16
RMS_EPS = 1e-6
LN_EPS = 1e-5
GN_EPS = 64e-5
VMEM_LIMIT = 56 * 1024 * 1024

_NN = (((1,), (0,)), ((), ()))
_NT = (((1,), (1,)), ((), ()))
_TN = (((0,), (0,)), ((), ()))


def _dot(a, b, dn=_NN):
    return lax.dot_general(a, b, dn, preferred_element_type=F32)


def _mm1(a, b, dn=_NN):
    return _dot(a.astype(BF16), b.astype(BF16), dn)


def _split2(x):
    hi = x.astype(BF16)
    lo = (x - hi.astype(F32)).astype(BF16)
    return hi, lo


def _hilo_cols(x):
    return jnp.concatenate(_split2(x), axis=1)


def _hilo_rows(x):
    return jnp.concatenate(_split2(x), axis=0)


def _rms(x, g):
    ms = jnp.mean(x * x, axis=-1, keepdims=True)
    return x * lax.rsqrt(ms + RMS_EPS) * g


def _ffn_kernel(h_ref, g_ref, w1_ref, w3_ref, w2_ref, o_ref, xn_ref, acc_ref):
    j = pl.program_id(1)

    @pl.when(j == 0)
    def _():
        xn_ref[...] = _rms(h_ref[...], g_ref[...]).astype(BF16)
        acc_ref[...] = jnp.zeros_like(acc_ref)

    xn = xn_ref[...]
    a = _dot(xn, w1_ref[...])
    b = _dot(xn, w3_ref[...])
    hm = (a * jax.nn.sigmoid(a) * b).astype(BF16)
    acc_ref[...] += _dot(hm, w2_ref[...])

    @pl.when(j == pl.num_programs(1) - 1)
    def _():
        o_ref[...] = h_ref[...] + 0.5 * acc_ref[...]


def _ffn(h, g, w1, w3, w2, *, tm=1024, tf=256):
    t, d = h.shape
    f = w1.shape[1]
    return pl.pallas_call(
        _ffn_kernel,
        out_shape=jax.ShapeDtypeStruct((t, d), F32),
        grid=(t // tm, f // tf),
        in_specs=[
            pl.BlockSpec((tm, d), lambda i, j: (i, 0)),
            pl.BlockSpec((1, d), lambda i, j: (0, 0)),
            pl.BlockSpec((d, tf), lambda i, j: (0, j)),
            pl.BlockSpec((d, tf), lambda i, j: (0, j)),
            pl.BlockSpec((tf, d), lambda i, j: (j, 0)),
        ],
        out_specs=pl.BlockSpec((tm, d), lambda i, j: (i, 0)),
        scratch_shapes=[pltpu.VMEM((tm, d), BF16), pltpu.VMEM((tm, d), F32)],
        compiler_params=pltpu.CompilerParams(
            dimension_semantics=("arbitrary", "arbitrary"), vmem_limit_bytes=VMEM_LIMIT),
        name="ffn",
    )(h, g, w1, w3, w2)


def _norm_matmul_kernel(h_ref, g_ref, w_ref, o_ref, xn_ref):
    @pl.when(pl.program_id(1) == 0)
    def _():
        xn_ref[...] = _rms(h_ref[...], g_ref[...]).astype(BF16)

    o_ref[...] = _dot(xn_ref[...], w_ref[...])


def _norm_matmul(h, g, w, *, tm=1024, tn=1408):
    t, d = h.shape
    n = w.shape[1]
    return pl.pallas_call(
        _norm_matmul_kernel,
        out_shape=jax.ShapeDtypeStruct((t, n), F32),
        grid=(t // tm, n // tn),
        in_specs=[
            pl.BlockSpec((tm, d), lambda i, j: (i, 0)),
            pl.BlockSpec((1, d), lambda i, j: (0, 0)),
            pl.BlockSpec((d, tn), lambda i, j: (0, j)),
        ],
        out_specs=pl.BlockSpec((tm, tn), lambda i, j: (i, j)),
        scratch_shapes=[pltpu.VMEM((tm, d), BF16)],
        compiler_params=pltpu.CompilerParams(
            dimension_semantics=("arbitrary", "arbitrary"), vmem_limit_bytes=VMEM_LIMIT),
        name="norm_matmul",
    )(h, g, w)


def _out_proj_kernel(h_ref, y_ref, w_ref, o_ref):
    o_ref[...] = h_ref[...] + _dot(y_ref[...].astype(BF16), w_ref[...])


def _out_proj(h, y, w, *, tm=1024):
    t, d = h.shape
    return pl.pallas_call(
        _out_proj_kernel,
        out_shape=jax.ShapeDtypeStruct((t, d), F32),
        grid=(t // tm,),
        in_specs=[
            pl.BlockSpec((tm, d), lambda i: (i, 0)),
            pl.BlockSpec((tm, d), lambda i: (i, 0)),
            pl.BlockSpec((d, d), lambda i: (0, 0)),
        ],
        out_specs=pl.BlockSpec((tm, d), lambda i: (i, 0)),
        compiler_params=pltpu.CompilerParams(
            dimension_semantics=("arbitrary",), vmem_limit_bytes=VMEM_LIMIT),
        name="out_proj",
    )(h, y, w)


def _ple_final_kernel(h_ref, p_ref, gp_ref, wg_ref, wp_ref, gf_ref, o_ref):
    h = h_ref[...]
    gate = jax.nn.sigmoid(_dot(_rms(h, gp_ref[...]).astype(BF16), wg_ref[...]))
    e = _dot(p_ref[...].astype(BF16), wp_ref[...])
    o_ref[...] = _rms(h + gate * e, gf_ref[...])


def _ple_final(h, p, gp, wg, wp, gf, *, tm=512):
    t, d = h.shape
    pd = p.shape[1]
    return pl.pallas_call(
        _ple_final_kernel,
        out_shape=jax.ShapeDtypeStruct((t, d), F32),
        grid=(t // tm,),
        in_specs=[
            pl.BlockSpec((tm, d), lambda i: (i, 0)),
            pl.BlockSpec((tm, pd), lambda i: (i, 0)),
            pl.BlockSpec((1, d), lambda i: (0, 0)),
            pl.BlockSpec((d, d), lambda i: (0, 0)),
            pl.BlockSpec((pd, d), lambda i: (0, 0)),
            pl.BlockSpec((1, d), lambda i: (0, 0)),
        ],
        out_specs=pl.BlockSpec((tm, d), lambda i: (i, 0)),
        compiler_params=pltpu.CompilerParams(
            dimension_semantics=("arbitrary",), vmem_limit_bytes=VMEM_LIMIT),
        name="ple_final",
    )(h, p, gp, wg, wp, gf)


def _iota2(shape, dim):
    return lax.broadcasted_iota(jnp.int32, shape, dim)


def _stack_heads(x, first):
    zero = jnp.zeros_like(x)
    return jnp.concatenate([jnp.where(first, x, zero), jnp.where(first, zero, x)], axis=0)


def _blockdiag2(xcat, left):
    zero = jnp.zeros_like(xcat)
    return jnp.concatenate([jnp.where(left, xcat, zero), jnp.where(left, zero, xcat)], axis=0)


def _unit_lower_inverse_pairs(acats, c):
    row = _iota2((c, 2 * c), 0)
    lane = _iota2((c, 2 * c), 1)
    col = lane & (c - 1)
    left = lane < c

    def same_block(b):
        s = b.bit_length() - 1
        return (row >> s) == (col >> s)

    def bd(x):
        return _blockdiag2(x.astype(BF16), left)

    eye = jnp.where(row == col, 1.0, 0.0)
    pws = [jnp.where(same_block(INV_BASE), a, 0.0) for a in acats]
    ts = [eye + pw for pw in pws]
    for _ in range(INV_BASE.bit_length() - 2):
        pws = [_dot(pw.astype(BF16), bd(pw)) for pw in pws]
        ts = [t + _dot(t.astype(BF16), bd(pw)) for t, pw in zip(ts, pws)]
    b = INV_BASE
    while b < c:
        off = same_block(2 * b) & jnp.logical_not(same_block(b))
        tas = [_dot(t.astype(BF16), bd(jnp.where(off, a, 0.0))) for t, a in zip(ts, acats)]
        ts = [t + _dot(ta.astype(BF16), bd(t)) for t, ta in zip(ts, tas)]
        b *= 2
    return ts


def _rwkv_chunk(r, k2, v, kkn, bvec, logd, rkk, g, gn_g, gn_b, state_ref):
    c = r.shape[0]
    row2 = _iota2((c, 2 * c), 0)
    col2 = _iota2((c, 2 * c), 1) & (c - 1)
    incl2 = row2 >= col2
    strict2 = row2 > col2
    lcum = _dot(incl2.astype(BF16), _hilo_rows(logd))
    lmid = lcum[c // 2 - 1:c // 2, :]
    lend = lcum[c - 1:c, :]
    lexc = lcum - logd
    e_mid_from = jnp.exp(lmid - lcum)
    rt = (r * jnp.exp(lcum - lmid)).astype(BF16)
    at = (-kkn * jnp.exp(lexc - lmid)).astype(BF16)
    bt = (bvec * e_mid_from).astype(BF16)
    kt = (k2 * e_mid_from).astype(BF16)
    r0 = (r * jnp.exp(lcum)).astype(BF16)
    a0 = (-kkn * jnp.exp(lexc)).astype(BF16)
    e_end = jnp.exp(lend - lcum)
    bh = (bvec * e_end).astype(BF16)
    kh = (k2 * e_end).astype(BF16)
    d_end = jnp.exp(lend)
    v_b = v.astype(BF16)

    first = _iota2((c, LANES), 1) < HEAD
    blockdiag = (_iota2((LANES, LANES), 0) >= HEAD) == (_iota2((LANES, LANES), 1) >= HEAD)
    ones_bd = blockdiag.astype(BF16)
    avg_bd = (blockdiag.astype(F32) * (1.0 / HEAD)).astype(BF16)
    zero_bd = jnp.zeros((LANES, LANES), BF16)
    avg_2 = jnp.concatenate([avg_bd, avg_bd], axis=0)
    stat_bd = jnp.concatenate([jnp.concatenate([avg_bd, zero_bd], axis=1),
                               jnp.concatenate([zero_bd, ones_bd], axis=1)], axis=0)

    pairs = range(N_PAIRS)
    sls = [slice(p * LANES, (p + 1) * LANES) for p in pairs]

    grams = [
        _dot(jnp.concatenate([at[:, sl], rt[:, sl]], axis=0),
             jnp.concatenate([_stack_heads(bt[:, sl], first), _stack_heads(kt[:, sl], first)],
                             axis=0), _NT)
        for sl in sls]
    a_abs = [jnp.where(strict2, gm[:c, :2 * c], 0.0) for gm in grams]
    a_rbs = [jnp.where(incl2, gm[c:, :2 * c], 0.0).astype(BF16) for gm in grams]
    akvs = [
        _dot(jnp.concatenate([jnp.where(strict2, gm[:c, 2 * c:], 0.0),
                              jnp.where(incl2, gm[c:, 2 * c:], 0.0)], axis=0).astype(BF16),
             _stack_heads(v_b[:, sl], first))
        for gm, sl in zip(grams, sls)]
    t_invs = [t.astype(BF16) for t in _unit_lower_inverse_pairs(a_abs, c)]

    states = [state_ref[p] for p in pairs]
    sms = []
    for s, sl in zip(states, sls):
        s_b = s.astype(BF16)
        s_2 = jnp.concatenate([jnp.concatenate([s_b, zero_bd], axis=1),
                               jnp.concatenate([zero_bd, s_b], axis=1)], axis=0)
        sms.append(_dot(jnp.concatenate([a0[:, sl], r0[:, sl]], axis=1), s_2, _NT))
    xs = [sm[:, :LANES] + akv[:c] for sm, akv in zip(sms, akvs)]
    us = [_dot(t, _stack_heads(x.astype(BF16), first)).astype(BF16) for t, x in zip(t_invs, xs)]
    for p, (s, u, sl) in enumerate(zip(states, us, sls)):
        upd = _dot(jnp.concatenate([u, v_b[:, sl]], axis=0),
                   jnp.concatenate([bh[:, sl], kh[:, sl]], axis=0), _TN)
        state_ref[p] = s * d_end[:, sl] + jnp.where(blockdiag, upd, 0.0)
    ys = [sm[:, LANES:] + akv[c:] + _dot(a_rb, _stack_heads(u, first))
          for sm, akv, a_rb, u in zip(sms, akvs, a_rbs, us)]

    means = [_dot(_hilo_cols(y), avg_2) for y in ys]
    ycs = [y - m for y, m in zip(ys, means)]
    stats = [_dot(jnp.concatenate([yc * yc, rkk[:, sl]], axis=1).astype(BF16), stat_bd)
             for yc, sl in zip(ycs, sls)]
    return [(yc * lax.rsqrt(st[:, :LANES] + GN_EPS) * gn_g[:, sl] + gn_b[:, sl]
             + st[:, LANES:] * v[:, sl]) * g[:, sl]
            for yc, st, sl in zip(ycs, stats, sls)]


def _mixer_kernel(z_ref, mu_ref, w0_ref, wdec_ref, a0_ref, waaa_ref, wgate_ref, kk_ref,
                  ka_ref, rk_ref, gng_ref, gnb_ref, lng_ref, lnb_ref, ws_ref, bs_ref,
                  o_ref, state_ref, prev_ref, wst_ref):
    @pl.when(pl.program_id(0) == 0)
    def _():
        state_ref[...] = jnp.zeros_like(state_ref)
        prev_ref[...] = jnp.zeros_like(prev_ref)
        causal = _iota2((CHUNK, CHUNK), 0) >= _iota2((CHUNK, CHUNK), 1)
        for h in range(ws_ref.shape[0]):
            wst_ref[h // 2, :, (h % 2) * CHUNK:(h % 2 + 1) * CHUNK] = (
                jnp.where(causal, ws_ref[h], 0.0).astype(BF16))

    zr = z_ref[:, :RWKV_COLS]
    zp = pltpu.roll(zr, 1, 0)
    zp = jnp.where(_iota2(zr.shape, 0) == 0, prev_ref[...], zp)
    prev_ref[...] = zr[CHUNK - 1:CHUNK, :]
    zs = zr + (zp - zr) * mu_ref[...]
    w = RWKV_WIDTH
    r, k, v = zs[:, 0:w], zs[:, w:2 * w], zs[:, 2 * w:3 * w]
    zwa = zs[:, 3 * w:3 * w + LANES]
    zg = zs[:, 3 * w + LANES:]
    lw = w0_ref[...] + _mm1(jnp.tanh(zwa), wdec_ref[...])
    nlw = -lw
    softplus = jnp.maximum(nlw, 0.0) + jnp.log1p(jnp.exp(-jnp.abs(nlw)))
    logd = -jnp.exp(-softplus - 0.5)
    a = jax.nn.sigmoid(a0_ref[...] + _mm1(zwa, waaa_ref[...]))
    g = _mm1(jax.nn.sigmoid(zg), wgate_ref[...])
    kk = k * kk_ref[...]
    head_of = lambda n, d: _iota2((n, n), d) >> (HEAD.bit_length() - 1)
    ones_2 = (head_of(2 * LANES, 0) == head_of(2 * LANES, 1)).astype(BF16)
    kk2 = (kk * kk).astype(BF16)
    n2 = jnp.concatenate(
        [_dot(kk2[:, q * 2 * LANES:(q + 1) * 2 * LANES], ones_2) for q in range(N_PAIRS // 2)],
        axis=1)
    kkn = kk / jnp.maximum(jnp.sqrt(n2), 1e-12)
    k2 = k * (1.0 + (a - 1.0) * ka_ref[...])
    bvec = kkn * a
    rkk = r * k2 * rk_ref[...]
    for sub in range(CHUNK // RWKV_CHUNK):
        rs = slice(sub * RWKV_CHUNK, (sub + 1) * RWKV_CHUNK)
        outs = _rwkv_chunk(r[rs], k2[rs], v[rs], kkn[rs], bvec[rs], logd[rs], rkk[rs], g[rs],
                           gng_ref[...], gnb_ref[...], state_ref)
        for p, o in enumerate(outs):
            o_ref[rs, p * LANES:(p + 1) * LANES] = o

    zg2 = z_ref[:, RWKV_COLS:]
    ge = 0.5 * zg2 * (1.0 + lax.erf(zg2 * 0.7071067811865476))
    u, vv = ge[:, :GMLP_WIDTH], ge[:, GMLP_WIDTH:]
    m = jnp.mean(vv, axis=-1, keepdims=True)
    vc = vv - m
    var = jnp.mean(vc * vc, axis=-1, keepdims=True)
    vn = (vc * lax.rsqrt(var + LN_EPS) * lng_ref[...] + lnb_ref[...]).astype(BF16)
    first = _iota2((CHUNK, LANES), 1) < HEAD
    for p in range(GMLP_WIDTH // LANES):
        sl = slice(p * LANES, (p + 1) * LANES)
        mixed = _dot(wst_ref[p], _stack_heads(vn[:, sl], first))
        o_ref[:, RWKV_WIDTH + p * LANES:RWKV_WIDTH + (p + 1) * LANES] = (
            u[:, sl] * (mixed + bs_ref[:, sl]))


def _mixer(z, mu, w0, wdec, a0, waaa, wgate, k_k, k_a, r_k, gn_g, gn_b, ln_g, ln_b, w_s, b_s):
    t = z.shape[0]
    n_heads = w_s.shape[0]

    def full(arr):
        nd = arr.ndim
        return pl.BlockSpec(arr.shape, lambda i: (0,) * nd)

    params = (mu, w0, wdec, a0, waaa, wgate, k_k, k_a, r_k, gn_g, gn_b, ln_g, ln_b, w_s, b_s)
    return pl.pallas_call(
        _mixer_kernel,
        out_shape=jax.ShapeDtypeStruct((t, D_MODEL), F32),
        grid=(t // CHUNK,),
        in_specs=[pl.BlockSpec((CHUNK, IN_COLS), lambda i: (i, 0))] + [full(a) for a in params],
        out_specs=pl.BlockSpec((CHUNK, D_MODEL), lambda i: (i, 0)),
        scratch_shapes=[
            pltpu.VMEM((N_PAIRS, LANES, LANES), F32),
            pltpu.VMEM((1, RWKV_COLS), F32),
            pltpu.VMEM((n_heads // 2, CHUNK, 2 * CHUNK), BF16),
        ],
        compiler_params=pltpu.CompilerParams(
            dimension_semantics=("arbitrary",), vmem_limit_bytes=VMEM_LIMIT),
        name="mixer",
    )(z, *params)


def kernel(x, p, norm_ffn1, ffn1_w1, ffn1_w3, ffn1_w2, norm_mix, w_in, shift_mu, rwkv_w0,
           rwkv_w_decay, rwkv_a0, rwkv_w_aaa, rwkv_w_gate, rwkv_k_k, rwkv_k_a, rwkv_r_k,
           rwkv_gn_g, rwkv_gn_b, sgu_ln_g, sgu_ln_b, sgu_w_s, sgu_b_s, w_out, norm_ffn2,
           ffn2_w1, ffn2_w3, ffn2_w2, norm_ple, w_ple_gate, w_ple, norm_final):
    assert x.shape[0] == 1 and p.shape[0] == 1, "one batch row and one layer, as the problem states"
    row = lambda a: a.reshape(1, -1)
    bf = lambda a: a.astype(BF16)
    h = x[0]
    h = _ffn(h, row(norm_ffn1[0]), bf(ffn1_w1[0]), bf(ffn1_w3[0]), bf(ffn1_w2[0]))
    z = _norm_matmul(h, row(norm_mix[0]), bf(w_in[0]))
    lora = rwkv_w_decay.shape[1]
    pad = jnp.zeros((LANES - lora, RWKV_WIDTH), F32)
    wdec = jnp.concatenate([rwkv_w_decay[0], pad], axis=0)
    waaa = jnp.concatenate([pad, rwkv_w_aaa[0]], axis=0)
    bias = jnp.repeat(sgu_b_s[0].T, HEAD, axis=1)
    y = _mixer(z, row(shift_mu[0]), row(rwkv_w0[0]), bf(wdec), row(rwkv_a0[0]), bf(waaa),
               bf(rwkv_w_gate[0]), row(rwkv_k_k[0]), row(rwkv_k_a[0]), row(rwkv_r_k[0]),
               row(rwkv_gn_g[0]), row(rwkv_gn_b[0]), row(sgu_ln_g[0]), row(sgu_ln_b[0]),
               sgu_w_s[0], bias)
    h = _out_proj(h, y, bf(w_out[0]))
    h = _ffn(h, row(norm_ffn2[0]), bf(ffn2_w1[0]), bf(ffn2_w3[0]), bf(ffn2_w2[0]))
    h = _ple_final(h, p[0, 0], row(norm_ple[0]), bf(w_ple_gate[0]), bf(w_ple[0]), row(norm_final))
    return h[None]
```

```python
import jax
import jax.numpy as jnp
from jax import lax
from jax.experimental import pallas as pl
from jax.experimental.pallas import tpu as pltpu

F32 = jnp.float32
BF16 = jnp.bfloat16

D_MODEL = 1024
D_FF = 2816
PLE_DIM = 256
RWKV_WIDTH = 512
RWKV_COLS = 1792
GMLP_WIDTH = 512
IN_COLS = 2816
HEAD = 64
LANES = 128
N_PAIRS = RWKV_WIDTH // LANES
CHUNK = 128
RWKV_CHUNK = 128
INV_BASE = 16
RMS_EPS = 1e-6
LN_EPS = 1e-5
GN_EPS = 64e-5
VMEM_LIMIT = 56 * 1024 * 1024

_NN = (((1,), (0,)), ((), ()))
_NT = (((1,), (1,)), ((), ()))
_TN = (((0,), (0,)), ((), ()))


def _dot(a, b, dn=_NN):
    return lax.dot_general(a, b, dn, preferred_element_type=F32)


def _mm1(a, b, dn=_NN):
    return _dot(a.astype(BF16), b.astype(BF16), dn)


def _split2(x):
    hi = x.astype(BF16)
    lo = (x - hi.astype(F32)).astype(BF16)
    return hi, lo


def _hilo_cols(x):
    return jnp.concatenate(_split2(x), axis=1)


def _hilo_rows(x):
    return jnp.concatenate(_split2(x), axis=0)


def _rms(x, g):
    ms = jnp.mean(x * x, axis=-1, keepdims=True)
    return x * lax.rsqrt(ms + RMS_EPS) * g


FF_CHUNK = 256


def _swiglu_half_step(h, g_ref, w1_ref, w3_ref, w2_ref, hm_ref):
    xn = _rms(h, g_ref[...]).astype(BF16)
    for j in range(hm_ref.shape[1] // FF_CHUNK):
        cols = slice(j * FF_CHUNK, (j + 1) * FF_CHUNK)
        a = _dot(xn, w1_ref[:, cols])
        b = _dot(xn, w3_ref[:, cols])
        hm_ref[:, cols] = (a * jax.nn.sigmoid(a) * b).astype(BF16)
    return h + 0.5 * _dot(hm_ref[...], w2_ref[...])


def _pre_mixer_kernel(x_ref, g1_ref, w1_ref, w3_ref, w2_ref, gm_ref, win_ref,
                      h_ref, z_ref, hm_ref):
    h = _swiglu_half_step(x_ref[...], g1_ref, w1_ref, w3_ref, w2_ref, hm_ref)
    h_ref[...] = h
    z_ref[...] = _dot(_rms(h, gm_ref[...]).astype(BF16), win_ref[...])


def _post_mixer_kernel(h_ref, y_ref, p_ref, wout_ref, g2_ref, w1_ref, w3_ref, w2_ref,
                       gp_ref, wg_ref, wp_ref, gf_ref, o_ref, hm_ref):
    h = h_ref[...] + _dot(y_ref[...].astype(BF16), wout_ref[...])
    h = _swiglu_half_step(h, g2_ref, w1_ref, w3_ref, w2_ref, hm_ref)
    gate = jax.nn.sigmoid(_dot(_rms(h, gp_ref[...]).astype(BF16), wg_ref[...]))
    e = _dot(p_ref[...].astype(BF16), wp_ref[...])
    o_ref[...] = _rms(h + gate * e, gf_ref[...])


def _rows(tm, width):
    return pl.BlockSpec((tm, width), lambda i: (i, 0))


def _resident(arr):
    nd = arr.ndim
    return pl.BlockSpec(arr.shape, lambda i: (0,) * nd, pipeline_mode=pl.Buffered(1))


def _dense_call(body, name, row_inputs, params, out_widths, *, tm=512):
    t = row_inputs[0].shape[0]
    outs = tuple(jax.ShapeDtypeStruct((t, w), F32) for w in out_widths)
    return pl.pallas_call(
        body,
        out_shape=outs,
        grid=(t // tm,),
        in_specs=[_rows(tm, a.shape[1]) for a in row_inputs] + [_resident(a) for a in params],
        out_specs=tuple(_rows(tm, w) for w in out_widths),
        scratch_shapes=[pltpu.VMEM((tm, D_FF), BF16)],
        compiler_params=pltpu.CompilerParams(
            dimension_semantics=("arbitrary",), vmem_limit_bytes=VMEM_LIMIT),
        name=name,
    )(*row_inputs, *params)


def _iota2(shape, dim):
    return lax.broadcasted_iota(jnp.int32, shape, dim)


def _stack_heads(x, first):
    zero = jnp.zeros_like(x)
    return jnp.concatenate([jnp.where(first, x, zero), jnp.where(first, zero, x)], axis=0)


def _blockdiag2(xcat, left):
    zero = jnp.zeros_like(xcat)
    return jnp.concatenate([jnp.where(left, xcat, zero), jnp.where(left, zero, xcat)], axis=0)


def _unit_lower_inverse_pairs(acats, c):
    row = _iota2((c, 2 * c), 0)
    lane = _iota2((c, 2 * c), 1)
    col = lane & (c - 1)
    left = lane < c

    def same_block(b):
        s = b.bit_length() - 1
        return (row >> s) == (col >> s)

    def bd(x):
        return _blockdiag2(x.astype(BF16), left)

    eye = jnp.where(row == col, 1.0, 0.0)
    pws = [jnp.where(same_block(INV_BASE), a, 0.0) for a in acats]
    ts = [eye + pw for pw in pws]
    for _ in range(INV_BASE.bit_length() - 2):
        pws = [_dot(pw.astype(BF16), bd(pw)) for pw in pws]
        ts = [t + _dot(t.astype(BF16), bd(pw)) for t, pw in zip(ts, pws)]
    b = INV_BASE
    while b < c:
        off = same_block(2 * b) & jnp.logical_not(same_block(b))
        tas = [_dot(t.astype(BF16), bd(jnp.where(off, a, 0.0))) for t, a in zip(ts, acats)]
        ts = [t + _dot(ta.astype(BF16), bd(t)) for t, ta in zip(ts, tas)]
        b *= 2
    return ts


def _rwkv_chunk(r, k2, v, kkn, bvec, logd, rkk, g, gn_g, gn_b, state_ref):
    c = r.shape[0]
    row2 = _iota2((c, 2 * c), 0)
    col2 = _iota2((c, 2 * c), 1) & (c - 1)
    incl2 = row2 >= col2
    strict2 = row2 > col2
    lcum = _dot(incl2.astype(BF16), _hilo_rows(logd))
    lmid = lcum[c // 2 - 1:c // 2, :]
    lend = lcum[c - 1:c, :]
    lexc = lcum - logd
    e_mid_from = jnp.exp(lmid - lcum)
    rt = (r * jnp.exp(lcum - lmid)).astype(BF16)
    at = (-kkn * jnp.exp(lexc - lmid)).astype(BF16)
    bt = (bvec * e_mid_from).astype(BF16)
    kt = (k2 * e_mid_from).astype(BF16)
    r0 = (r * jnp.exp(lcum)).astype(BF16)
    a0 = (-kkn * jnp.exp(lexc)).astype(BF16)
    e_end = jnp.exp(lend - lcum)
    bh = (bvec * e_end).astype(BF16)
    kh = (k2 * e_end).astype(BF16)
    d_end = jnp.exp(lend)
    v_b = v.astype(BF16)

    first = _iota2((c, LANES), 1) < HEAD
    blockdiag = (_iota2((LANES, LANES), 0) >= HEAD) == (_iota2((LANES, LANES), 1) >= HEAD)
    ones_bd = blockdiag.astype(BF16)
    avg_bd = (blockdiag.astype(F32) * (1.0 / HEAD)).astype(BF16)
    zero_bd = jnp.zeros((LANES, LANES), BF16)
    avg_2 = jnp.concatenate([avg_bd, avg_bd], axis=0)
    stat_bd = jnp.concatenate([jnp.concatenate([avg_bd, zero_bd], axis=1),
                               jnp.concatenate([zero_bd, ones_bd], axis=1)], axis=0)

    pairs = range(N_PAIRS)
    sls = [slice(p * LANES, (p + 1) * LANES) for p in pairs]

    grams = [
        _dot(jnp.concatenate([at[:, sl], rt[:, sl]], axis=0),
             jnp.concatenate([_stack_heads(bt[:, sl], first), _stack_heads(kt[:, sl], first)],
                             axis=0), _NT)
        for sl in sls]
    a_abs = [jnp.where(strict2, gm[:c, :2 * c], 0.0) for gm in grams]
    a_rbs = [jnp.where(incl2, gm[c:, :2 * c], 0.0).astype(BF16) for gm in grams]
    akvs = [
        _dot(jnp.concatenate([jnp.where(strict2, gm[:c, 2 * c:], 0.0),
                              jnp.where(incl2, gm[c:, 2 * c:], 0.0)], axis=0).astype(BF16),
             _stack_heads(v_b[:, sl], first))
        for gm, sl in zip(grams, sls)]
    t_invs = [t.astype(BF16) for t in _unit_lower_inverse_pairs(a_abs, c)]

    states = [state_ref[p] for p in pairs]
    sms = []
    for s, sl in zip(states, sls):
        s_b = s.astype(BF16)
        s_2 = jnp.concatenate([jnp.concatenate([s_b, zero_bd], axis=1),
                               jnp.concatenate([zero_bd, s_b], axis=1)], axis=0)
        sms.append(_dot(jnp.concatenate([a0[:, sl], r0[:, sl]], axis=1), s_2, _NT))
    xs = [sm[:, :LANES] + akv[:c] for sm, akv in zip(sms, akvs)]
    us = [_dot(t, _stack_heads(x.astype(BF16), first)).astype(BF16) for t, x in zip(t_invs, xs)]
    for p, (s, u, sl) in enumerate(zip(states, us, sls)):
        upd = _dot(jnp.concatenate([u, v_b[:, sl]], axis=0),
                   jnp.concatenate([bh[:, sl], kh[:, sl]], axis=0), _TN)
        state_ref[p] = s * d_end[:, sl] + jnp.where(blockdiag, upd, 0.0)
    ys = [sm[:, LANES:] + akv[c:] + _dot(a_rb, _stack_heads(u, first))
          for sm, akv, a_rb, u in zip(sms, akvs, a_rbs, us)]

    means = [_dot(_hilo_cols(y), avg_2) for y in ys]
    ycs = [y - m for y, m in zip(ys, means)]
    stats = [_dot(jnp.concatenate([yc * yc, rkk[:, sl]], axis=1).astype(BF16), stat_bd)
             for yc, sl in zip(ycs, sls)]
    return [(yc * lax.rsqrt(st[:, :LANES] + GN_EPS) * gn_g[:, sl] + gn_b[:, sl]
             + st[:, LANES:] * v[:, sl]) * g[:, sl]
            for yc, st, sl in zip(ycs, stats, sls)]


def _mixer_kernel(z_ref, mu_ref, w0_ref, wdec_ref, a0_ref, waaa_ref, wgate_ref, kk_ref,
                  ka_ref, rk_ref, gng_ref, gnb_ref, lng_ref, lnb_ref, ws_ref, bs_ref,
                  o_ref, state_ref, prev_ref, wst_ref):
    @pl.when(pl.program_id(0) == 0)
    def _():
        state_ref[...] = jnp.zeros_like(state_ref)
        prev_ref[...] = jnp.zeros_like(prev_ref)
        causal = _iota2((CHUNK, CHUNK), 0) >= _iota2((CHUNK, CHUNK), 1)
        for h in range(ws_ref.shape[0]):
            wst_ref[h // 2, :, (h % 2) * CHUNK:(h % 2 + 1) * CHUNK] = (
                jnp.where(causal, ws_ref[h], 0.0).astype(BF16))

    zr = z_ref[:, :RWKV_COLS]
    zp = pltpu.roll(zr, 1, 0)
    zp = jnp.where(_iota2(zr.shape, 0) == 0, prev_ref[...], zp)
    prev_ref[...] = zr[CHUNK - 1:CHUNK, :]
    zs = zr + (zp - zr) * mu_ref[...]
    w = RWKV_WIDTH
    r, k, v = zs[:, 0:w], zs[:, w:2 * w], zs[:, 2 * w:3 * w]
    zwa = zs[:, 3 * w:3 * w + LANES]
    zg = zs[:, 3 * w + LANES:]
    lw = w0_ref[...] + _mm1(jnp.tanh(zwa), wdec_ref[...])
    nlw = -lw
    softplus = jnp.maximum(nlw, 0.0) + jnp.log1p(jnp.exp(-jnp.abs(nlw)))
    logd = -jnp.exp(-softplus - 0.5)
    a = jax.nn.sigmoid(a0_ref[...] + _mm1(zwa, waaa_ref[...]))
    g = _mm1(jax.nn.sigmoid(zg), wgate_ref[...])
    kk = k * kk_ref[...]
    head_of = lambda n, d: _iota2((n, n), d) >> (HEAD.bit_length() - 1)
    ones_2 = (head_of(2 * LANES, 0) == head_of(2 * LANES, 1)).astype(BF16)
    kk2 = (kk * kk).astype(BF16)
    n2 = jnp.concatenate(
        [_dot(kk2[:, q * 2 * LANES:(q + 1) * 2 * LANES], ones_2) for q in range(N_PAIRS // 2)],
        axis=1)
    kkn = kk / jnp.maximum(jnp.sqrt(n2), 1e-12)
    k2 = k * (1.0 + (a - 1.0) * ka_ref[...])
    bvec = kkn * a
    rkk = r * k2 * rk_ref[...]
    for sub in range(CHUNK // RWKV_CHUNK):
        rs = slice(sub * RWKV_CHUNK, (sub + 1) * RWKV_CHUNK)
        outs = _rwkv_chunk(r[rs], k2[rs], v[rs], kkn[rs], bvec[rs], logd[rs], rkk[rs], g[rs],
                           gng_ref[...], gnb_ref[...], state_ref)
        for p, o in enumerate(outs):
            o_ref[rs, p * LANES:(p + 1) * LANES] = o

    zg2 = z_ref[:, RWKV_COLS:]
    ge = 0.5 * zg2 * (1.0 + lax.erf(zg2 * 0.7071067811865476))
    u, vv = ge[:, :GMLP_WIDTH], ge[:, GMLP_WIDTH:]
    m = jnp.mean(vv, axis=-1, keepdims=True)
    vc = vv - m
    var = jnp.mean(vc * vc, axis=-1, keepdims=True)
    vn = (vc * lax.rsqrt(var + LN_EPS) * lng_ref[...] + lnb_ref[...]).astype(BF16)
    first = _iota2((CHUNK, LANES), 1) < HEAD
    for p in range(GMLP_WIDTH // LANES):
        sl = slice(p * LANES, (p + 1) * LANES)
        mixed = _dot(wst_ref[p], _stack_heads(vn[:, sl], first))
        o_ref[:, RWKV_WIDTH + p * LANES:RWKV_WIDTH + (p + 1) * LANES] = (
            u[:, sl] * (mixed + bs_ref[:, sl]))


def _mixer(z, mu, w0, wdec, a0, waaa, wgate, k_k, k_a, r_k, gn_g, gn_b, ln_g, ln_b, w_s, b_s):
    t = z.shape[0]
    n_heads = w_s.shape[0]

    def full(arr):
        nd = arr.ndim
        return pl.BlockSpec(arr.shape, lambda i: (0,) * nd)

    params = (mu, w0, wdec, a0, waaa, wgate, k_k, k_a, r_k, gn_g, gn_b, ln_g, ln_b, w_s, b_s)
    return pl.pallas_call(
        _mixer_kernel,
        out_shape=jax.ShapeDtypeStruct((t, D_MODEL), F32),
        grid=(t // CHUNK,),
        in_specs=[pl.BlockSpec((CHUNK, IN_COLS), lambda i: (i, 0))] + [full(a) for a in params],
        out_specs=pl.BlockSpec((CHUNK, D_MODEL), lambda i: (i, 0)),
        scratch_shapes=[
            pltpu.VMEM((N_PAIRS, LANES, LANES), F32),
            pltpu.VMEM((1, RWKV_COLS), F32),
            pltpu.VMEM((n_heads // 2, CHUNK, 2 * CHUNK), BF16),
        ],
        compiler_params=pltpu.CompilerParams(
            dimension_semantics=("arbitrary",), vmem_limit_bytes=VMEM_LIMIT),
        name="mixer",
    )(z, *params)


def kernel(x, p, norm_ffn1, ffn1_w1, ffn1_w3, ffn1_w2, norm_mix, w_in, shift_mu, rwkv_w0,
           rwkv_w_decay, rwkv_a0, rwkv_w_aaa, rwkv_w_gate, rwkv_k_k, rwkv_k_a, rwkv_r_k,
           rwkv_gn_g, rwkv_gn_b, sgu_ln_g, sgu_ln_b, sgu_w_s, sgu_b_s, w_out, norm_ffn2,
           ffn2_w1, ffn2_w3, ffn2_w2, norm_ple, w_ple_gate, w_ple, norm_final):
    assert x.shape[0] == 1 and p.shape[0] == 1, "one batch row and one layer, as the problem states"
    row = lambda a: a.reshape(1, -1)
    bf = lambda a: a.astype(BF16)
    h, z = _dense_call(
        _pre_mixer_kernel, "pre_mixer", [x[0]],
        [row(norm_ffn1[0]), bf(ffn1_w1[0]), bf(ffn1_w3[0]), bf(ffn1_w2[0]),
         row(norm_mix[0]), bf(w_in[0])],
        (D_MODEL, IN_COLS))
    lora = rwkv_w_decay.shape[1]
    pad = jnp.zeros((LANES - lora, RWKV_WIDTH), F32)
    wdec = jnp.concatenate([rwkv_w_decay[0], pad], axis=0)
    waaa = jnp.concatenate([pad, rwkv_w_aaa[0]], axis=0)
    bias = jnp.repeat(sgu_b_s[0].T, HEAD, axis=1)
    y = _mixer(z, row(shift_mu[0]), row(rwkv_w0[0]), bf(wdec), row(rwkv_a0[0]), bf(waaa),
               bf(rwkv_w_gate[0]), row(rwkv_k_k[0]), row(rwkv_k_a[0]), row(rwkv_r_k[0]),
               row(rwkv_gn_g[0]), row(rwkv_gn_b[0]), row(sgu_ln_g[0]), row(sgu_ln_b[0]),
               sgu_w_s[0], bias)
    (out,) = _dense_call(
        _post_mixer_kernel, "post_mixer", [h, y, p[0, 0]],
        [bf(w_out[0]), row(norm_ffn2[0]), bf(ffn2_w1[0]), bf(ffn2_w3[0]), bf(ffn2_w2[0]),
         row(norm_ple[0]), bf(w_ple_gate[0]), bf(w_ple[0]), row(norm_final)],
        (D_MODEL,))
    return out[None]
```

```python
import itertools

import jax
import jax.numpy as jnp
from jax import lax
from jax.experimental import pallas as pl
from jax.experimental.pallas import tpu as pltpu

F32 = jnp.float32
BF16 = jnp.bfloat16

D_MODEL = 1024
D_FF = 2816
PLE_DIM = 256
RWKV_WIDTH = 512
RWKV_COLS = 1792
GMLP_WIDTH = 512
IN_COLS = 2816
HEAD = 64
LANES = 128
N_PAIRS = RWKV_WIDTH // LANES
CHUNK = 128
INV_BASE = 16
RMS_EPS = 1e-6
LN_EPS = 1e-5
GN_EPS = 64e-5
VMEM_LIMIT = 56 * 1024 * 1024

_NN = (((1,), (0,)), ((), ()))
_NT = (((1,), (1,)), ((), ()))
_TN = (((0,), (0,)), ((), ()))


def _dot(a, b, dn=_NN):
    return lax.dot_general(a, b, dn, preferred_element_type=F32)


def _split2(x):
    hi = x.astype(BF16)
    lo = (x - hi.astype(F32)).astype(BF16)
    return hi, lo


def _hilo_cols(x):
    return jnp.concatenate(_split2(x), axis=1)


def _hilo_rows(x):
    return jnp.concatenate(_split2(x), axis=0)


def _rms(x, g):
    ms = jnp.mean(x * x, axis=-1, keepdims=True)
    return x * lax.rsqrt(ms + RMS_EPS) * g


FF_CHUNK = 256


def _swiglu_half_step(h, g_ref, w1_ref, w3_ref, w2_ref, hm_ref):
    xn = _rms(h, g_ref[...]).astype(BF16)
    for j in range(hm_ref.shape[1] // FF_CHUNK):
        cols = slice(j * FF_CHUNK, (j + 1) * FF_CHUNK)
        a = _dot(xn, w1_ref[:, cols])
        b = _dot(xn, w3_ref[:, cols])
        hm_ref[:, cols] = (a * jax.nn.sigmoid(a) * b).astype(BF16)
    return h + 0.5 * _dot(hm_ref[...], w2_ref[...])


def _pre_mixer_kernel(x_ref, g1_ref, w1_ref, w3_ref, w2_ref, gm_ref, win_ref,
                      h_ref, zr_ref, zg_ref, hm_ref):
    h = _swiglu_half_step(x_ref[...], g1_ref, w1_ref, w3_ref, w2_ref, hm_ref)
    h_ref[...] = h
    z = _dot(_rms(h, gm_ref[...]).astype(BF16), win_ref[...])
    zr_ref[...] = z[:, :RWKV_COLS]
    zg_ref[...] = z[:, RWKV_COLS:]


def _post_mixer_kernel(h_ref, y_ref, p_ref, wout_ref, g2_ref, w1_ref, w3_ref, w2_ref,
                       gp_ref, wg_ref, wp_ref, gf_ref, o_ref, hm_ref):
    h = h_ref[...] + _dot(y_ref[...].astype(BF16), wout_ref[...])
    h = _swiglu_half_step(h, g2_ref, w1_ref, w3_ref, w2_ref, hm_ref)
    gate = jax.nn.sigmoid(_dot(_rms(h, gp_ref[...]).astype(BF16), wg_ref[...]))
    e = _dot(p_ref[...].astype(BF16), wp_ref[...])
    o_ref[...] = _rms(h + gate * e, gf_ref[...])


def _rows(tm, width):
    return pl.BlockSpec((tm, width), lambda i: (i, 0))


def _resident(arr):
    nd = arr.ndim
    return pl.BlockSpec(arr.shape, lambda i: (0,) * nd, pipeline_mode=pl.Buffered(1))


def _dense_call(body, name, row_inputs, params, out_widths, *, tm=512):
    t = row_inputs[0].shape[0]
    outs = tuple(jax.ShapeDtypeStruct((t, w), F32) for w in out_widths)
    return pl.pallas_call(
        body,
        out_shape=outs,
        grid=(t // tm,),
        in_specs=[_rows(tm, a.shape[1]) for a in row_inputs] + [_resident(a) for a in params],
        out_specs=tuple(_rows(tm, w) for w in out_widths),
        scratch_shapes=[pltpu.VMEM((tm, D_FF), BF16)],
        compiler_params=pltpu.CompilerParams(
            dimension_semantics=("arbitrary",), vmem_limit_bytes=VMEM_LIMIT),
        name=name,
    )(*row_inputs, *params)


def _iota2(shape, dim):
    return lax.broadcasted_iota(jnp.int32, shape, dim)


def _stack_heads(x, first):
    zero = jnp.zeros_like(x)
    return jnp.concatenate([jnp.where(first, x, zero), jnp.where(first, zero, x)], axis=0)


def _nilpotent_inverses(xs, eye, index, filler):
    c = eye.shape[0]
    prods = [eye + x for x in xs]
    pows = [x.astype(BF16) for x in xs]
    pows = [_dot(p, p).astype(BF16) for p in pows]
    next(filler, None)
    covered = 4
    while covered < index:
        outs = [_dot(p, jnp.concatenate([p, t.astype(BF16)], axis=1)) for p, t in zip(pows, prods)]
        pows = [o[:, :c].astype(BF16) for o in outs]
        prods = [t + o[:, c:] for t, o in zip(prods, outs)]
        next(filler, None)
        covered *= 2
    prods = [t + _dot(p, t.astype(BF16)) for p, t in zip(pows, prods)]
    next(filler, None)
    return prods


def _unit_lower_inverses(mats, c, filler):
    row = _iota2((c, c), 0)
    col = _iota2((c, c), 1)
    shift = INV_BASE.bit_length() - 1
    diag_block = (row >> shift) == (col >> shift)
    eye = jnp.where(row == col, 1.0, 0.0)
    t_diag = _nilpotent_inverses([jnp.where(diag_block, a, 0.0) for a in mats], eye, INV_BASE,
                                 filler)
    t_diag = [t.astype(BF16) for t in t_diag]
    n_mats = [_dot(t, jnp.where(diag_block, 0.0, a).astype(BF16)) for t, a in zip(t_diag, mats)]
    next(filler, None)
    t_off = _nilpotent_inverses(n_mats, eye, c // INV_BASE, filler)
    return [_dot(t.astype(BF16), td) for t, td in zip(t_off, t_diag)]


_OPS_B = ("rt", "at", "bt", "kt", "r0", "a0", "bh", "kh", "vb")
_OPS_F = ("rkk", "v", "g")


def _rwkv_prepare_stages(zr_ref, prev_ref, mu_ref, w0_ref, wdec_ref, a0_ref, waaa_ref,
                         wgate_ref, kk_ref, ka_ref, rk_ref, opb_ref, opf_ref, dend_ref, slot):
    c = CHUNK
    first_row = _iota2((c, LANES), 0) == 0

    def shifted(col):
        cols = slice(col, col + LANES)
        z = zr_ref[:, cols]
        zp = jnp.where(first_row, prev_ref[:, cols], pltpu.roll(z, 1, 0))
        return z + (zp - z) * mu_ref[:, cols]

    w = RWKV_WIDTH
    zwa = shifted(3 * w)
    zwa_tanh = jnp.tanh(zwa).astype(BF16)
    zwa_b = zwa.astype(BF16)
    zg_sig = jax.nn.sigmoid(shifted(3 * w + LANES)).astype(BF16)
    blockdiag = (_iota2((LANES, LANES), 0) >= HEAD) == (_iota2((LANES, LANES), 1) >= HEAD)
    ones_bd = blockdiag.astype(BF16)
    row2 = _iota2((c, 2 * c), 0)
    col2 = _iota2((c, 2 * c), 1) & (c - 1)
    tril_2 = (row2 >= col2).astype(BF16)
    yield
    for p in range(N_PAIRS):
        sl = slice(p * LANES, (p + 1) * LANES)
        r, k, v = shifted(p * LANES), shifted(w + p * LANES), shifted(2 * w + p * LANES)
        lw = w0_ref[:, sl] + _dot(zwa_tanh, wdec_ref[:, sl])
        logd = -0.6065306597126334 * jax.nn.sigmoid(lw)
        a = jax.nn.sigmoid(a0_ref[:, sl] + _dot(zwa_b, waaa_ref[:, sl]))
        g = _dot(zg_sig, wgate_ref[:, sl])
        kk = k * kk_ref[:, sl]
        n2 = _dot((kk * kk).astype(BF16), ones_bd)
        kkn = kk * lax.rsqrt(jnp.maximum(n2, 1e-24))
        k2 = k * (1.0 + (a - 1.0) * ka_ref[:, sl])
        bvec = kkn * a
        opf_ref[slot, _OPS_F.index("rkk"), :, sl] = r * k2 * rk_ref[:, sl]
        opf_ref[slot, _OPS_F.index("v"), :, sl] = v
        opf_ref[slot, _OPS_F.index("g"), :, sl] = g
        yield
        lcum = _dot(tril_2, _hilo_rows(logd))
        lmid = lcum[c // 2 - 1:c // 2, :]
        lend = lcum[c - 1:c, :]
        lexc = lcum - logd
        e_mid_from = jnp.exp(lmid - lcum)
        e_end = jnp.exp(lend - lcum)
        ops = dict(
            rt=r * jnp.exp(lcum - lmid), at=-kkn * jnp.exp(lexc - lmid),
            bt=bvec * e_mid_from, kt=k2 * e_mid_from,
            r0=r * jnp.exp(lcum), a0=-kkn * jnp.exp(lexc),
            bh=bvec * e_end, kh=k2 * e_end, vb=v)
        for i, name in enumerate(_OPS_B):
            opb_ref[slot, i, :, sl] = ops[name].astype(BF16)
        dend_ref[slot, :, sl] = jnp.broadcast_to(jnp.exp(lend), (dend_ref.shape[1], LANES))
        yield
    prev_ref[...] = zr_ref[CHUNK - 1:CHUNK, :]


def _rwkv_chunk(opb_ref, opf_ref, dend_ref, slot, gn_g, gn_b, state_ref, filler):
    c = CHUNK
    row2 = _iota2((c, 2 * c), 0)
    col2 = _iota2((c, 2 * c), 1) & (c - 1)
    incl2 = row2 >= col2
    strict2 = row2 > col2
    first = _iota2((c, LANES), 1) < HEAD
    blockdiag = (_iota2((LANES, LANES), 0) >= HEAD) == (_iota2((LANES, LANES), 1) >= HEAD)
    ones_bd = blockdiag.astype(BF16)
    avg_bd = (blockdiag.astype(F32) * (1.0 / HEAD)).astype(BF16)
    zero_bd = jnp.zeros((LANES, LANES), BF16)
    avg_2 = jnp.concatenate([avg_bd, avg_bd], axis=0)
    stat_bd = jnp.concatenate([jnp.concatenate([avg_bd, zero_bd], axis=1),
                               jnp.concatenate([zero_bd, ones_bd], axis=1)], axis=0)
    pairs = range(N_PAIRS)
    sls = [slice(p * LANES, (p + 1) * LANES) for p in pairs]
    opb = lambda name, sl: opb_ref[slot, _OPS_B.index(name), :, sl]
    opf = lambda name, sl: opf_ref[slot, _OPS_F.index(name), :, sl]

    grams = [
        _dot(jnp.concatenate([opb("at", sl), opb("rt", sl)], axis=0),
             jnp.concatenate([_stack_heads(opb("bt", sl), first),
                              _stack_heads(opb("kt", sl), first)], axis=0), _NT)
        for sl in sls]
    a_abs = [jnp.where(strict2, gm[:c, :2 * c], 0.0) for gm in grams]
    t_heads = _unit_lower_inverses(
        [ab[:, hh * c:(hh + 1) * c] for ab in a_abs for hh in range(2)], c, filler)
    t_invs = [jnp.concatenate([t_heads[2 * p].astype(BF16), t_heads[2 * p + 1].astype(BF16)], axis=1)
              for p in pairs]
    a_rbs = [jnp.where(incl2, gm[c:, :2 * c], 0.0).astype(BF16) for gm in grams]
    akvs = [
        _dot(jnp.concatenate([jnp.where(strict2, gm[:c, 2 * c:], 0.0),
                              jnp.where(incl2, gm[c:, 2 * c:], 0.0)], axis=0).astype(BF16),
             _stack_heads(opb("vb", sl), first))
        for gm, sl in zip(grams, sls)]

    states = [state_ref[p] for p in pairs]
    sms = []
    for s, sl in zip(states, sls):
        s_b = s.astype(BF16)
        s_2 = jnp.concatenate([jnp.concatenate([s_b, zero_bd], axis=1),
                               jnp.concatenate([zero_bd, s_b], axis=1)], axis=0)
        sms.append(_dot(jnp.concatenate([opb("a0", sl), opb("r0", sl)], axis=1), s_2, _NT))
    xs = [sm[:, :LANES] + akv[:c] for sm, akv in zip(sms, akvs)]
    us = [_dot(t, _stack_heads(x.astype(BF16), first)).astype(BF16) for t, x in zip(t_invs, xs)]
    for p, (s, u, sl) in enumerate(zip(states, us, sls)):
        upd = _dot(jnp.concatenate([u, opb("vb", sl)], axis=0),
                   jnp.concatenate([opb("bh", sl), opb("kh", sl)], axis=0), _TN)
        state_ref[p] = s * dend_ref[slot, 0:1, sl] + jnp.where(blockdiag, upd, 0.0)
    ys = [sm[:, LANES:] + akv[c:] + _dot(a_rb, _stack_heads(u, first))
          for sm, akv, a_rb, u in zip(sms, akvs, a_rbs, us)]

    means = [_dot(_hilo_cols(y), avg_2) for y in ys]
    ycs = [y - m for y, m in zip(ys, means)]
    stats = [_dot(jnp.concatenate([yc * yc, opf("rkk", sl)], axis=1).astype(BF16), stat_bd)
             for yc, sl in zip(ycs, sls)]
    return [(yc * lax.rsqrt(st[:, :LANES] + GN_EPS) * gn_g[:, sl] + gn_b[:, sl]
             + st[:, LANES:] * opf("v", sl)) * opf("g", sl)
            for yc, st, sl in zip(ycs, stats, sls)]


def _gmlp_stages(zg_ref, lng_ref, lnb_ref, wst_ref, bs_ref, o_ref):
    gelu = lambda t: 0.5 * t * (1.0 + lax.erf(t * 0.7071067811865476))
    n_tiles = GMLP_WIDTH // LANES
    cols = lambda base, p: slice(base + p * LANES, base + (p + 1) * LANES)
    vs = []
    for p in range(n_tiles):
        vs.append(gelu(zg_ref[:, cols(GMLP_WIDTH, p)]))
        yield
    m = sum(jnp.sum(t, axis=-1, keepdims=True) for t in vs) * (1.0 / GMLP_WIDTH)
    vcs = [t - m for t in vs]
    var = sum(jnp.sum(t * t, axis=-1, keepdims=True) for t in vcs) * (1.0 / GMLP_WIDTH)
    inv = lax.rsqrt(var + LN_EPS)
    yield
    first = _iota2((CHUNK, LANES), 1) < HEAD
    for p in range(n_tiles):
        vn = (vcs[p] * inv * lng_ref[:, cols(0, p)] + lnb_ref[:, cols(0, p)]).astype(BF16)
        mixed = _dot(wst_ref[p], _stack_heads(vn, first))
        u = gelu(zg_ref[:, cols(0, p)])
        o_ref[:, cols(RWKV_WIDTH, p)] = u * (mixed + bs_ref[:, cols(0, p)])
        yield


def _interleaved(*stage_generators):
    for _ in itertools.zip_longest(*stage_generators):
        yield


def _mixer_kernel(zr_ref, zg_ref, mu_ref, w0_ref, wdec_ref, a0_ref, waaa_ref, wgate_ref, kk_ref,
                  ka_ref, rk_ref, gng_ref, gnb_ref, lng_ref, lnb_ref, ws_ref, bs_ref,
                  o_ref, state_ref, prev_ref, wst_ref, opb_ref, opf_ref, dend_ref):
    s = pl.program_id(0)

    @pl.when(s == 0)
    def _():
        state_ref[...] = jnp.zeros_like(state_ref)
        prev_ref[...] = jnp.zeros_like(prev_ref)
        opb_ref[...] = jnp.zeros_like(opb_ref)
        opf_ref[...] = jnp.zeros_like(opf_ref)
        dend_ref[...] = jnp.zeros_like(dend_ref)
        causal = _iota2((CHUNK, CHUNK), 0) >= _iota2((CHUNK, CHUNK), 1)
        for h in range(ws_ref.shape[0]):
            wst_ref[h // 2, :, (h % 2) * CHUNK:(h % 2 + 1) * CHUNK] = (
                jnp.where(causal, ws_ref[h], 0.0).astype(BF16))

    prepare = _rwkv_prepare_stages(zr_ref, prev_ref, mu_ref, w0_ref, wdec_ref, a0_ref, waaa_ref,
                                   wgate_ref, kk_ref, ka_ref, rk_ref, opb_ref, opf_ref, dend_ref,
                                   s % 2)
    gmlp = _gmlp_stages(zg_ref, lng_ref, lnb_ref, wst_ref, bs_ref, o_ref)
    filler = _interleaved(prepare, gmlp)
    outs = _rwkv_chunk(opb_ref, opf_ref, dend_ref, (s + 1) % 2, gng_ref[...], gnb_ref[...],
                       state_ref, filler)
    for p, o in enumerate(outs):
        o_ref[:, p * LANES:(p + 1) * LANES] = o
    for _ in filler:
        pass


def _mixer(zr, zg, mu, w0, wdec, a0, waaa, wgate, k_k, k_a, r_k, gn_g, gn_b, ln_g, ln_b, w_s, b_s):
    n = zr.shape[0] // CHUNK
    n_heads = w_s.shape[0]

    def full(arr):
        nd = arr.ndim
        return pl.BlockSpec(arr.shape, lambda s: (0,) * nd)

    params = (mu, w0, wdec, a0, waaa, wgate, k_k, k_a, r_k, gn_g, gn_b, ln_g, ln_b, w_s, b_s)
    return pl.pallas_call(
        _mixer_kernel,
        out_shape=jax.ShapeDtypeStruct((n * CHUNK, D_MODEL), F32),
        grid=(n + 1,),
        in_specs=[pl.BlockSpec((CHUNK, RWKV_COLS), lambda s: (jnp.minimum(s, n - 1), 0)),
                  pl.BlockSpec((CHUNK, 2 * GMLP_WIDTH), lambda s: (jnp.maximum(s - 1, 0), 0))]
                 + [full(a) for a in params],
        out_specs=pl.BlockSpec((CHUNK, D_MODEL), lambda s: (jnp.maximum(s - 1, 0), 0)),
        scratch_shapes=[
            pltpu.VMEM((N_PAIRS, LANES, LANES), F32),
            pltpu.VMEM((1, RWKV_COLS), F32),
            pltpu.VMEM((n_heads // 2, CHUNK, 2 * CHUNK), BF16),
            pltpu.VMEM((2, len(_OPS_B), CHUNK, RWKV_WIDTH), BF16),
            pltpu.VMEM((2, len(_OPS_F), CHUNK, RWKV_WIDTH), F32),
            pltpu.VMEM((2, 8, RWKV_WIDTH), F32),
        ],
        compiler_params=pltpu.CompilerParams(
            dimension_semantics=("arbitrary",), vmem_limit_bytes=VMEM_LIMIT),
        name="mixer",
    )(zr, zg, *params)


def kernel(x, p, norm_ffn1, ffn1_w1, ffn1_w3, ffn1_w2, norm_mix, w_in, shift_mu, rwkv_w0,
           rwkv_w_decay, rwkv_a0, rwkv_w_aaa, rwkv_w_gate, rwkv_k_k, rwkv_k_a, rwkv_r_k,
           rwkv_gn_g, rwkv_gn_b, sgu_ln_g, sgu_ln_b, sgu_w_s, sgu_b_s, w_out, norm_ffn2,
           ffn2_w1, ffn2_w3, ffn2_w2, norm_ple, w_ple_gate, w_ple, norm_final):
    assert x.shape[0] == 1 and p.shape[0] == 1, "one batch row and one layer, as the problem states"
    row = lambda a: a.reshape(1, -1)
    bf = lambda a: a.astype(BF16)
    h, zr, zg = _dense_call(
        _pre_mixer_kernel, "pre_mixer", [x[0]],
        [row(norm_ffn1[0]), bf(ffn1_w1[0]), bf(ffn1_w3[0]), bf(ffn1_w2[0]),
         row(norm_mix[0]), bf(w_in[0])],
        (D_MODEL, RWKV_COLS, 2 * GMLP_WIDTH))
    lora = rwkv_w_decay.shape[1]
    pad = jnp.zeros((LANES - lora, RWKV_WIDTH), F32)
    wdec = jnp.concatenate([rwkv_w_decay[0], pad], axis=0)
    waaa = jnp.concatenate([pad, rwkv_w_aaa[0]], axis=0)
    bias = jnp.repeat(sgu_b_s[0].T, HEAD, axis=1)
    y = _mixer(zr, zg, row(shift_mu[0]), row(rwkv_w0[0]), bf(wdec), row(rwkv_a0[0]), bf(waaa),
               bf(rwkv_w_gate[0]), row(rwkv_k_k[0]), row(rwkv_k_a[0]), row(rwkv_r_k[0]),
               row(rwkv_gn_g[0]), row(rwkv_gn_b[0]), row(sgu_ln_g[0]), row(sgu_ln_b[0]),
               sgu_w_s[0], bias)
    (out,) = _dense_call(
        _post_mixer_kernel, "post_mixer", [h, y, p[0, 0]],
        [bf(w_out[0]), row(norm_ffn2[0]), bf(ffn2_w1[0]), bf(ffn2_w3[0]), bf(ffn2_w2[0]),
         row(norm_ple[0]), bf(w_ple_gate[0]), bf(w_ple[0]), row(norm_final)],
        (D_MODEL,))
    return out[None]
```

```python
import itertools

import jax
import jax.numpy as jnp
from jax import lax
from jax.experimental import pallas as pl
from jax.experimental.pallas import tpu as pltpu

F32 = jnp.float32
BF16 = jnp.bfloat16

D_MODEL = 1024
D_FF = 2816
PLE_DIM = 256
RWKV_WIDTH = 512
RWKV_COLS = 1792
GMLP_WIDTH = 512
IN_COLS = 2816
HEAD = 64
LANES = 128
SUBLANES = 8
N_PAIRS = RWKV_WIDTH // LANES
CHUNK = 128
MIX_CHUNKS = 2
INV_BASE = 16
RMS_EPS = 1e-6
LN_EPS = 1e-5
GN_EPS = 64e-5
VMEM_LIMIT = 56 * 1024 * 1024

_NN = (((1,), (0,)), ((), ()))
_NT = (((1,), (1,)), ((), ()))
_TN = (((0,), (0,)), ((), ()))


def _dot(a, b, dn=_NN):
    return lax.dot_general(a, b, dn, preferred_element_type=F32)


def _split2(x):
    hi = x.astype(BF16)
    lo = (x - hi.astype(F32)).astype(BF16)
    return hi, lo


def _hilo_cols(x):
    return jnp.concatenate(_split2(x), axis=1)


def _hilo_rows(x):
    return jnp.concatenate(_split2(x), axis=0)


def _rms(x, g):
    ms = jnp.mean(x * x, axis=-1, keepdims=True)
    return x * lax.rsqrt(ms + RMS_EPS) * g


FF_CHUNK = 256


def _swiglu_half_step(h, g_ref, w1_ref, w3_ref, w2_ref, hm_ref):
    xn = _rms(h, g_ref[...]).astype(BF16)
    for j in range(hm_ref.shape[1] // FF_CHUNK):
        cols = slice(j * FF_CHUNK, (j + 1) * FF_CHUNK)
        a = _dot(xn, w1_ref[:, cols])
        b = _dot(xn, w3_ref[:, cols])
        hm_ref[:, cols] = (a * jax.nn.sigmoid(a) * b).astype(BF16)
    return h + 0.5 * _dot(hm_ref[...], w2_ref[...])


def _pre_mixer_kernel(x_ref, g1_ref, w1_ref, w3_ref, w2_ref, gm_ref, win_ref,
                      h_ref, zr_ref, zg_ref, hm_ref):
    h = _swiglu_half_step(x_ref[...], g1_ref, w1_ref, w3_ref, w2_ref, hm_ref)
    h_ref[...] = h
    z = _dot(_rms(h, gm_ref[...]).astype(BF16), win_ref[...])
    zr_ref[...] = z[:, :RWKV_COLS]
    zg_ref[...] = z[:, RWKV_COLS:]


def _post_mixer_kernel(h_ref, y_ref, p_ref, wout_ref, g2_ref, w1_ref, w3_ref, w2_ref,
                       gp_ref, wg_ref, wp_ref, gf_ref, o_ref, hm_ref):
    h = h_ref[...] + _dot(y_ref[...].astype(BF16), wout_ref[...])
    h = _swiglu_half_step(h, g2_ref, w1_ref, w3_ref, w2_ref, hm_ref)
    gate = jax.nn.sigmoid(_dot(_rms(h, gp_ref[...]).astype(BF16), wg_ref[...]))
    e = _dot(p_ref[...].astype(BF16), wp_ref[...])
    o_ref[...] = _rms(h + gate * e, gf_ref[...])


def _rows(tm, width):
    return pl.BlockSpec((tm, width), lambda i: (i, 0))


def _resident(arr):
    nd = arr.ndim
    return pl.BlockSpec(arr.shape, lambda i: (0,) * nd, pipeline_mode=pl.Buffered(1))


def _dense_call(body, name, row_inputs, params, out_widths, *, tm=512):
    t = row_inputs[0].shape[0]
    outs = tuple(jax.ShapeDtypeStruct((t, w), F32) for w in out_widths)
    return pl.pallas_call(
        body,
        out_shape=outs,
        grid=(t // tm,),
        in_specs=[_rows(tm, a.shape[1]) for a in row_inputs] + [_resident(a) for a in params],
        out_specs=tuple(_rows(tm, w) for w in out_widths),
        scratch_shapes=[pltpu.VMEM((tm, D_FF), BF16)],
        compiler_params=pltpu.CompilerParams(
            dimension_semantics=("arbitrary",), vmem_limit_bytes=VMEM_LIMIT),
        name=name,
    )(*row_inputs, *params)


def _iota2(shape, dim):
    return lax.broadcasted_iota(jnp.int32, shape, dim)


def _stack_heads(x, first):
    zero = jnp.zeros_like(x)
    return jnp.concatenate([jnp.where(first, x, zero), jnp.where(first, zero, x)], axis=0)


def _nilpotent_inverses(xs, eye, index, filler):
    c = eye.shape[0]
    prods = [eye + x for x in xs]
    pows = [x.astype(BF16) for x in xs]
    pows = [_dot(p, p).astype(BF16) for p in pows]
    next(filler, None)
    covered = 4
    while covered < index:
        outs = [_dot(p, jnp.concatenate([p, t.astype(BF16)], axis=1)) for p, t in zip(pows, prods)]
        pows = [o[:, :c].astype(BF16) for o in outs]
        prods = [t + o[:, c:] for t, o in zip(prods, outs)]
        next(filler, None)
        covered *= 2
    prods = [t + _dot(p, t.astype(BF16)) for p, t in zip(pows, prods)]
    next(filler, None)
    return prods


def _unit_lower_inverses(mats, c, filler):
    row = _iota2((c, c), 0)
    col = _iota2((c, c), 1)
    shift = INV_BASE.bit_length() - 1
    diag_block = (row >> shift) == (col >> shift)
    eye = jnp.where(row == col, 1.0, 0.0)
    t_diag = _nilpotent_inverses([jnp.where(diag_block, a, 0.0) for a in mats], eye, INV_BASE,
                                 filler)
    t_diag = [t.astype(BF16) for t in t_diag]
    n_mats = [_dot(t, jnp.where(diag_block, 0.0, a).astype(BF16)) for t, a in zip(t_diag, mats)]
    next(filler, None)
    t_off = _nilpotent_inverses(n_mats, eye, c // INV_BASE, filler)
    return [_dot(t.astype(BF16), td) for t, td in zip(t_off, t_diag)]


_OPS_B = ("rt", "at", "bt", "kt", "r0", "a0", "bh", "kh", "vb")
_OPS_F = ("rkk", "v", "g")


def _rwkv_prepare_stages(zr_ref, prev_ref, mu_ref, w0_ref, wdec_ref, a0_ref, waaa_ref,
                         wgate_ref, kk_ref, ka_ref, rk_ref, opb_ref, opf_ref, dend_ref, slot):
    c = CHUNK
    rows = zr_ref.shape[0]
    chunks = [slice(j * c, (j + 1) * c) for j in range(rows // c)]
    first_row = _iota2((rows, LANES), 0) == 0

    def shifted(col):
        cols = slice(col, col + LANES)
        z = zr_ref[:, cols]
        zp = jnp.where(first_row, prev_ref[:, cols], pltpu.roll(z, 1, 0))
        return z + (zp - z) * mu_ref[:, cols]

    w = RWKV_WIDTH
    zwa = shifted(3 * w)
    zwa_tanh = jnp.tanh(zwa).astype(BF16)
    zwa_b = zwa.astype(BF16)
    zg_sig = jax.nn.sigmoid(shifted(3 * w + LANES)).astype(BF16)
    blockdiag = (_iota2((LANES, LANES), 0) >= HEAD) == (_iota2((LANES, LANES), 1) >= HEAD)
    ones_bd = blockdiag.astype(BF16)
    row2 = _iota2((c, 2 * c), 0)
    col2 = _iota2((c, 2 * c), 1) & (c - 1)
    tril_2 = (row2 >= col2).astype(BF16)
    yield
    for p in range(N_PAIRS):
        sl = slice(p * LANES, (p + 1) * LANES)
        r, k, v = shifted(p * LANES), shifted(w + p * LANES), shifted(2 * w + p * LANES)
        lw = w0_ref[:, sl] + _dot(zwa_tanh, wdec_ref[:, sl])
        logd = -0.6065306597126334 * jax.nn.sigmoid(lw)
        a = jax.nn.sigmoid(a0_ref[:, sl] + _dot(zwa_b, waaa_ref[:, sl]))
        g = _dot(zg_sig, wgate_ref[:, sl])
        kk = k * kk_ref[:, sl]
        n2 = _dot((kk * kk).astype(BF16), ones_bd)
        kkn = kk * lax.rsqrt(jnp.maximum(n2, 1e-24))
        k2 = k * (1.0 + (a - 1.0) * ka_ref[:, sl])
        bvec = kkn * a
        opf_ref[slot, _OPS_F.index("rkk"), :, sl] = r * k2 * rk_ref[:, sl]
        opf_ref[slot, _OPS_F.index("v"), :, sl] = v
        opf_ref[slot, _OPS_F.index("g"), :, sl] = g
        yield
        for j, rs in enumerate(chunks):
            lcum = _dot(tril_2, _hilo_rows(logd[rs]))
            lmid = lcum[c // 2 - 1:c // 2, :]
            lend = lcum[c - 1:c, :]
            lexc = lcum - logd[rs]
            e_mid_from = jnp.exp(lmid - lcum)
            e_end = jnp.exp(lend - lcum)
            ops = dict(
                rt=r[rs] * jnp.exp(lcum - lmid), at=-kkn[rs] * jnp.exp(lexc - lmid),
                bt=bvec[rs] * e_mid_from, kt=k2[rs] * e_mid_from,
                r0=r[rs] * jnp.exp(lcum), a0=-kkn[rs] * jnp.exp(lexc),
                bh=bvec[rs] * e_end, kh=k2[rs] * e_end, vb=v[rs])
            for i, name in enumerate(_OPS_B):
                opb_ref[slot, i, rs, sl] = ops[name].astype(BF16)
            dend_ref[slot, j * SUBLANES:(j + 1) * SUBLANES, sl] = jnp.broadcast_to(
                jnp.exp(lend), (SUBLANES, LANES))
        yield
    prev_ref[...] = zr_ref[rows - 1:rows, :]


def _rwkv_chunks(opb_ref, opf_ref, dend_ref, slot, gn_g, gn_b, state_ref, o_ref, filler):
    c = CHUNK
    row2 = _iota2((c, 2 * c), 0)
    col2 = _iota2((c, 2 * c), 1) & (c - 1)
    incl2 = row2 >= col2
    strict2 = row2 > col2
    first = _iota2((c, LANES), 1) < HEAD
    blockdiag = (_iota2((LANES, LANES), 0) >= HEAD) == (_iota2((LANES, LANES), 1) >= HEAD)
    ones_bd = blockdiag.astype(BF16)
    avg_bd = (blockdiag.astype(F32) * (1.0 / HEAD)).astype(BF16)
    zero_bd = jnp.zeros((LANES, LANES), BF16)
    avg_2 = jnp.concatenate([avg_bd, avg_bd], axis=0)
    stat_bd = jnp.concatenate([jnp.concatenate([avg_bd, zero_bd], axis=1),
                               jnp.concatenate([zero_bd, ones_bd], axis=1)], axis=0)
    pairs = range(N_PAIRS)
    sls = [slice(p * LANES, (p + 1) * LANES) for p in pairs]
    chunks = [slice(j * c, (j + 1) * c) for j in range(opb_ref.shape[2] // c)]
    tiles = [(j, rs, p, sl) for j, rs in enumerate(chunks) for p, sl in enumerate(sls)]
    opb = lambda name, rs, sl: opb_ref[slot, _OPS_B.index(name), rs, sl]
    opf = lambda name, rs, sl: opf_ref[slot, _OPS_F.index(name), rs, sl]

    grams = [
        _dot(jnp.concatenate([opb("at", rs, sl), opb("rt", rs, sl)], axis=0),
             jnp.concatenate([_stack_heads(opb("bt", rs, sl), first),
                              _stack_heads(opb("kt", rs, sl), first)], axis=0), _NT)
        for _, rs, _, sl in tiles]
    a_abs = [jnp.where(strict2, gm[:c, :2 * c], 0.0) for gm in grams]
    t_heads = _unit_lower_inverses(
        [ab[:, hh * c:(hh + 1) * c] for ab in a_abs for hh in range(2)], c, filler)
    t_invs = [jnp.concatenate([t_heads[2 * i].astype(BF16), t_heads[2 * i + 1].astype(BF16)], axis=1)
              for i in range(len(tiles))]
    a_rbs = [jnp.where(incl2, gm[c:, :2 * c], 0.0).astype(BF16) for gm in grams]
    akvs = [
        _dot(jnp.concatenate([jnp.where(strict2, gm[:c, 2 * c:], 0.0),
                              jnp.where(incl2, gm[c:, 2 * c:], 0.0)], axis=0).astype(BF16),
             _stack_heads(opb("vb", rs, sl), first))
        for gm, (_, rs, _, sl) in zip(grams, tiles)]

    states = [state_ref[p] for p in pairs]
    ys = []
    for j, rs in enumerate(chunks):
        mine = slice(j * N_PAIRS, (j + 1) * N_PAIRS)
        sms = []
        for s, sl in zip(states, sls):
            s_b = s.astype(BF16)
            s_2 = jnp.concatenate([jnp.concatenate([s_b, zero_bd], axis=1),
                                   jnp.concatenate([zero_bd, s_b], axis=1)], axis=0)
            sms.append(_dot(jnp.concatenate([opb("a0", rs, sl), opb("r0", rs, sl)], axis=1),
                            s_2, _NT))
        xs = [sm[:, :LANES] + akv[:c] for sm, akv in zip(sms, akvs[mine])]
        us = [_dot(t, _stack_heads(x.astype(BF16), first)).astype(BF16)
              for t, x in zip(t_invs[mine], xs)]
        upds = [_dot(jnp.concatenate([u, opb("vb", rs, sl)], axis=0),
                     jnp.concatenate([opb("bh", rs, sl), opb("kh", rs, sl)], axis=0), _TN)
                for u, sl in zip(us, sls)]
        states = [s * dend_ref[slot, j * SUBLANES:j * SUBLANES + 1, sl]
                  + jnp.where(blockdiag, upd, 0.0) for s, upd, sl in zip(states, upds, sls)]
        ys += [sm[:, LANES:] + akv[c:] + _dot(a_rb, _stack_heads(u, first))
               for sm, akv, a_rb, u in zip(sms, akvs[mine], a_rbs[mine], us)]
    for p, s in enumerate(states):
        state_ref[p] = s

    means = [_dot(_hilo_cols(y), avg_2) for y in ys]
    ycs = [y - m for y, m in zip(ys, means)]
    stats = [_dot(jnp.concatenate([yc * yc, opf("rkk", rs, sl)], axis=1).astype(BF16), stat_bd)
             for yc, (_, rs, _, sl) in zip(ycs, tiles)]
    for yc, st, (_, rs, _, sl) in zip(ycs, stats, tiles):
        o_ref[rs, sl] = ((yc * lax.rsqrt(st[:, :LANES] + GN_EPS) * gn_g[:, sl] + gn_b[:, sl]
                          + st[:, LANES:] * opf("v", rs, sl)) * opf("g", rs, sl))


def _gmlp_stages(zg_ref, lng_ref, lnb_ref, wst_ref, bs_ref, o_ref):
    gelu = lambda t: 0.5 * t * (1.0 + lax.erf(t * 0.7071067811865476))
    n_tiles = GMLP_WIDTH // LANES
    cols = lambda base, p: slice(base + p * LANES, base + (p + 1) * LANES)
    rows = zg_ref.shape[0]
    chunks = [slice(j * CHUNK, (j + 1) * CHUNK) for j in range(rows // CHUNK)]
    vs = []
    for p in range(n_tiles):
        vs.append(gelu(zg_ref[:, cols(GMLP_WIDTH, p)]))
        yield
    m = sum(jnp.sum(t, axis=-1, keepdims=True) for t in vs) * (1.0 / GMLP_WIDTH)
    vcs = [t - m for t in vs]
    var = sum(jnp.sum(t * t, axis=-1, keepdims=True) for t in vcs) * (1.0 / GMLP_WIDTH)
    inv = lax.rsqrt(var + LN_EPS)
    yield
    first = _iota2((CHUNK, LANES), 1) < HEAD
    for p in range(n_tiles):
        vn = (vcs[p] * inv * lng_ref[:, cols(0, p)] + lnb_ref[:, cols(0, p)]).astype(BF16)
        u = gelu(zg_ref[:, cols(0, p)])
        for rs in chunks:
            mixed = _dot(wst_ref[p], _stack_heads(vn[rs], first))
            o_ref[rs, cols(RWKV_WIDTH, p)] = u[rs] * (mixed + bs_ref[:, cols(0, p)])
        yield


def _interleaved(*stage_generators):
    for _ in itertools.zip_longest(*stage_generators):
        yield


def _mixer_kernel(zr_ref, zg_ref, mu_ref, w0_ref, wdec_ref, a0_ref, waaa_ref, wgate_ref, kk_ref,
                  ka_ref, rk_ref, gng_ref, gnb_ref, lng_ref, lnb_ref, ws_ref, bs_ref,
                  o_ref, state_ref, prev_ref, wst_ref, opb_ref, opf_ref, dend_ref):
    s = pl.program_id(0)

    @pl.when(s == 0)
    def _():
        state_ref[...] = jnp.zeros_like(state_ref)
        prev_ref[...] = jnp.zeros_like(prev_ref)
        opb_ref[...] = jnp.zeros_like(opb_ref)
        opf_ref[...] = jnp.zeros_like(opf_ref)
        dend_ref[...] = jnp.zeros_like(dend_ref)
        causal = _iota2((CHUNK, CHUNK), 0) >= _iota2((CHUNK, CHUNK), 1)
        for h in range(ws_ref.shape[0]):
            wst_ref[h // 2, :, (h % 2) * CHUNK:(h % 2 + 1) * CHUNK] = (
                jnp.where(causal, ws_ref[h], 0.0).astype(BF16))

    prepare = _rwkv_prepare_stages(zr_ref, prev_ref, mu_ref, w0_ref, wdec_ref, a0_ref, waaa_ref,
                                   wgate_ref, kk_ref, ka_ref, rk_ref, opb_ref, opf_ref, dend_ref,
                                   s % 2)
    gmlp = _gmlp_stages(zg_ref, lng_ref, lnb_ref, wst_ref, bs_ref, o_ref)
    filler = _interleaved(prepare, gmlp)
    _rwkv_chunks(opb_ref, opf_ref, dend_ref, (s + 1) % 2, gng_ref[...], gnb_ref[...],
                 state_ref, o_ref, filler)
    for _ in filler:
        pass


def _mixer(zr, zg, mu, w0, wdec, a0, waaa, wgate, k_k, k_a, r_k, gn_g, gn_b, ln_g, ln_b, w_s, b_s):
    rows = MIX_CHUNKS * CHUNK
    n = zr.shape[0] // rows
    n_heads = w_s.shape[0]

    def full(arr):
        nd = arr.ndim
        return pl.BlockSpec(arr.shape, lambda s: (0,) * nd)

    params = (mu, w0, wdec, a0, waaa, wgate, k_k, k_a, r_k, gn_g, gn_b, ln_g, ln_b, w_s, b_s)
    return pl.pallas_call(
        _mixer_kernel,
        out_shape=jax.ShapeDtypeStruct((n * rows, D_MODEL), F32),
        grid=(n + 1,),
        in_specs=[pl.BlockSpec((rows, RWKV_COLS), lambda s: (jnp.minimum(s, n - 1), 0)),
                  pl.BlockSpec((rows, 2 * GMLP_WIDTH), lambda s: (jnp.maximum(s - 1, 0), 0))]
                 + [full(a) for a in params],
        out_specs=pl.BlockSpec((rows, D_MODEL), lambda s: (jnp.maximum(s - 1, 0), 0)),
        scratch_shapes=[
            pltpu.VMEM((N_PAIRS, LANES, LANES), F32),
            pltpu.VMEM((1, RWKV_COLS), F32),
            pltpu.VMEM((n_heads // 2, CHUNK, 2 * CHUNK), BF16),
            pltpu.VMEM((2, len(_OPS_B), rows, RWKV_WIDTH), BF16),
            pltpu.VMEM((2, len(_OPS_F), rows, RWKV_WIDTH), F32),
            pltpu.VMEM((2, MIX_CHUNKS * SUBLANES, RWKV_WIDTH), F32),
        ],
        compiler_params=pltpu.CompilerParams(
            dimension_semantics=("arbitrary",), vmem_limit_bytes=VMEM_LIMIT),
        name="mixer",
    )(zr, zg, *params)


def kernel(x, p, norm_ffn1, ffn1_w1, ffn1_w3, ffn1_w2, norm_mix, w_in, shift_mu, rwkv_w0,
           rwkv_w_decay, rwkv_a0, rwkv_w_aaa, rwkv_w_gate, rwkv_k_k, rwkv_k_a, rwkv_r_k,
           rwkv_gn_g, rwkv_gn_b, sgu_ln_g, sgu_ln_b, sgu_w_s, sgu_b_s, w_out, norm_ffn2,
           ffn2_w1, ffn2_w3, ffn2_w2, norm_ple, w_ple_gate, w_ple, norm_final):
    assert x.shape[0] == 1 and p.shape[0] == 1, "one batch row and one layer, as the problem states"
    row = lambda a: a.reshape(1, -1)
    bf = lambda a: a.astype(BF16)
    h, zr, zg = _dense_call(
        _pre_mixer_kernel, "pre_mixer", [x[0]],
        [row(norm_ffn1[0]), bf(ffn1_w1[0]), bf(ffn1_w3[0]), bf(ffn1_w2[0]),
         row(norm_mix[0]), bf(w_in[0])],
        (D_MODEL, RWKV_COLS, 2 * GMLP_WIDTH))
    lora = rwkv_w_decay.shape[1]
    pad = jnp.zeros((LANES - lora, RWKV_WIDTH), F32)
    wdec = jnp.concatenate([rwkv_w_decay[0], pad], axis=0)
    waaa = jnp.concatenate([pad, rwkv_w_aaa[0]], axis=0)
    bias = jnp.repeat(sgu_b_s[0].T, HEAD, axis=1)
    y = _mixer(zr, zg, row(shift_mu[0]), row(rwkv_w0[0]), bf(wdec), row(rwkv_a0[0]), bf(waaa),
               bf(rwkv_w_gate[0]), row(rwkv_k_k[0]), row(rwkv_k_a[0]), row(rwkv_r_k[0]),
               row(rwkv_gn_g[0]), row(rwkv_gn_b[0]), row(sgu_ln_g[0]), row(sgu_ln_b[0]),
               sgu_w_s[0], bias)
    (out,) = _dense_call(
        _post_mixer_kernel, "post_mixer", [h, y, p[0, 0]],
        [bf(w_out[0]), row(norm_ffn2[0]), bf(ffn2_w1[0]), bf(ffn2_w3[0]), bf(ffn2_w2[0]),
         row(norm_ple[0]), bf(w_ple_gate[0]), bf(w_ple[0]), row(norm_final)],
        (D_MODEL,))
    return out[None]
```

```python
import itertools

import jax
import jax.numpy as jnp
from jax import lax
from jax.experimental import pallas as pl
from jax.experimental.pallas import tpu as pltpu

F32 = jnp.float32
BF16 = jnp.bfloat16

D_MODEL = 1024
D_FF = 2816
PLE_DIM = 256
RWKV_WIDTH = 512
RWKV_COLS = 1792
GMLP_WIDTH = 512
IN_COLS = 2816
HEAD = 64
LANES = 128
SUBLANES = 8
N_PAIRS = RWKV_WIDTH // LANES
CHUNK = 128
RWKV_CHUNK = 64
MIX_CHUNKS = 2
INV_BASE = 16
RMS_EPS = 1e-6
LN_EPS = 1e-5
GN_EPS = 64e-5
VMEM_LIMIT = 56 * 1024 * 1024

_NN = (((1,), (0,)), ((), ()))
_NT = (((1,), (1,)), ((), ()))
_TN = (((0,), (0,)), ((), ()))


def _dot(a, b, dn=_NN):
    return lax.dot_general(a, b, dn, preferred_element_type=F32)


def _split2(x):
    hi = x.astype(BF16)
    lo = (x - hi.astype(F32)).astype(BF16)
    return hi, lo


def _hilo_cols(x):
    return jnp.concatenate(_split2(x), axis=1)


def _hilo_rows(x):
    return jnp.concatenate(_split2(x), axis=0)


def _rms(x, g):
    ms = jnp.mean(x * x, axis=-1, keepdims=True)
    return x * lax.rsqrt(ms + RMS_EPS) * g


FF_CHUNK = 256


def _swiglu_half_step(h, g_ref, w1_ref, w3_ref, w2_ref, hm_ref):
    xn = _rms(h, g_ref[...]).astype(BF16)
    for j in range(hm_ref.shape[1] // FF_CHUNK):
        cols = slice(j * FF_CHUNK, (j + 1) * FF_CHUNK)
        a = _dot(xn, w1_ref[:, cols])
        b = _dot(xn, w3_ref[:, cols])
        hm_ref[:, cols] = (a * jax.nn.sigmoid(a) * b).astype(BF16)
    return h + 0.5 * _dot(hm_ref[...], w2_ref[...])


def _pre_mixer_kernel(x_ref, g1_ref, w1_ref, w3_ref, w2_ref, gm_ref, win_ref,
                      h_ref, zr_ref, zg_ref, hm_ref):
    h = _swiglu_half_step(x_ref[...], g1_ref, w1_ref, w3_ref, w2_ref, hm_ref)
    h_ref[...] = h
    z = _dot(_rms(h, gm_ref[...]).astype(BF16), win_ref[...])
    zr_ref[...] = z[:, :RWKV_COLS]
    zg_ref[...] = z[:, RWKV_COLS:]


def _post_mixer_kernel(h_ref, y_ref, p_ref, wout_ref, g2_ref, w1_ref, w3_ref, w2_ref,
                       gp_ref, wg_ref, wp_ref, gf_ref, o_ref, hm_ref):
    h = h_ref[...] + _dot(y_ref[...].astype(BF16), wout_ref[...])
    h = _swiglu_half_step(h, g2_ref, w1_ref, w3_ref, w2_ref, hm_ref)
    gate = jax.nn.sigmoid(_dot(_rms(h, gp_ref[...]).astype(BF16), wg_ref[...]))
    e = _dot(p_ref[...].astype(BF16), wp_ref[...])
    o_ref[...] = _rms(h + gate * e, gf_ref[...])


def _rows(tm, width):
    return pl.BlockSpec((tm, width), lambda i: (i, 0))


def _resident(arr):
    nd = arr.ndim
    return pl.BlockSpec(arr.shape, lambda i: (0,) * nd, pipeline_mode=pl.Buffered(1))


def _dense_call(body, name, row_inputs, params, out_widths, *, tm=512):
    t = row_inputs[0].shape[0]
    outs = tuple(jax.ShapeDtypeStruct((t, w), F32) for w in out_widths)
    return pl.pallas_call(
        body,
        out_shape=outs,
        grid=(t // tm,),
        in_specs=[_rows(tm, a.shape[1]) for a in row_inputs] + [_resident(a) for a in params],
        out_specs=tuple(_rows(tm, w) for w in out_widths),
        scratch_shapes=[pltpu.VMEM((tm, D_FF), BF16)],
        compiler_params=pltpu.CompilerParams(
            dimension_semantics=("arbitrary",), vmem_limit_bytes=VMEM_LIMIT),
        name=name,
    )(*row_inputs, *params)


def _iota2(shape, dim):
    return lax.broadcasted_iota(jnp.int32, shape, dim)


def _stack_heads(x, first):
    zero = jnp.zeros_like(x)
    return jnp.concatenate([jnp.where(first, x, zero), jnp.where(first, zero, x)], axis=0)


def _nilpotent_inverses(xs, eye, index, bd, filler):
    w = eye.shape[1]
    prods = [eye + x for x in xs]
    pows = [x.astype(BF16) for x in xs]
    pows = [_dot(p, bd(p)).astype(BF16) for p in pows]
    next(filler, None)
    covered = 4
    while covered < index:
        outs = [_dot(p, jnp.concatenate([bd(p), bd(t.astype(BF16))], axis=1))
                for p, t in zip(pows, prods)]
        pows = [o[:, :w].astype(BF16) for o in outs]
        prods = [t + o[:, w:] for t, o in zip(prods, outs)]
        next(filler, None)
        covered *= 2
    prods = [t + _dot(p, bd(t.astype(BF16))) for p, t in zip(pows, prods)]
    next(filler, None)
    return prods


def _unit_lower_inverses(mats, c, filler):
    row = _iota2((c, 2 * c), 0)
    lane = _iota2((c, 2 * c), 1)
    col = lane & (c - 1)
    left = lane < c
    shift = INV_BASE.bit_length() - 1
    diag_block = (row >> shift) == (col >> shift)
    eye = jnp.where(row == col, 1.0, 0.0)

    def bd(x):
        zero = jnp.zeros_like(x)
        return jnp.concatenate([jnp.where(left, x, zero), jnp.where(left, zero, x)], axis=0)

    t_diag = _nilpotent_inverses([jnp.where(diag_block, a, 0.0) for a in mats], eye, INV_BASE,
                                 bd, filler)
    t_diag = [t.astype(BF16) for t in t_diag]
    n_mats = [_dot(t, bd(jnp.where(diag_block, 0.0, a).astype(BF16)))
              for t, a in zip(t_diag, mats)]
    next(filler, None)
    t_off = _nilpotent_inverses(n_mats, eye, c // INV_BASE, bd, filler)
    return [_dot(t.astype(BF16), bd(td)) for t, td in zip(t_off, t_diag)]


_OPS_B = ("rt", "at", "bt", "kt", "r0", "a0", "bh", "kh", "vb")
_OPS_F = ("rkk", "v", "g")


def _rwkv_prepare_stages(zr_ref, prev_ref, mu_ref, w0_ref, wdec_ref, a0_ref, waaa_ref,
                         wgate_ref, kk_ref, ka_ref, rk_ref, opb_ref, opf_ref, dend_ref, slot):
    c = RWKV_CHUNK
    rows = zr_ref.shape[0]
    chunks = [slice(j * c, (j + 1) * c) for j in range(rows // c)]
    first_row = _iota2((rows, LANES), 0) == 0

    def shifted(col):
        cols = slice(col, col + LANES)
        z = zr_ref[:, cols]
        zp = jnp.where(first_row, prev_ref[:, cols], pltpu.roll(z, 1, 0))
        return z + (zp - z) * mu_ref[:, cols]

    w = RWKV_WIDTH
    zwa = shifted(3 * w)
    zwa_tanh = jnp.tanh(zwa).astype(BF16)
    zwa_b = zwa.astype(BF16)
    zg_sig = jax.nn.sigmoid(shifted(3 * w + LANES)).astype(BF16)
    blockdiag = (_iota2((LANES, LANES), 0) >= HEAD) == (_iota2((LANES, LANES), 1) >= HEAD)
    ones_bd = blockdiag.astype(BF16)
    row2 = _iota2((c, 2 * c), 0)
    col2 = _iota2((c, 2 * c), 1) & (c - 1)
    tril_2 = (row2 >= col2).astype(BF16)
    yield
    for p in range(N_PAIRS):
        sl = slice(p * LANES, (p + 1) * LANES)
        r, k, v = shifted(p * LANES), shifted(w + p * LANES), shifted(2 * w + p * LANES)
        lw = w0_ref[:, sl] + _dot(zwa_tanh, wdec_ref[:, sl])
        logd = -0.6065306597126334 * jax.nn.sigmoid(lw)
        a = jax.nn.sigmoid(a0_ref[:, sl] + _dot(zwa_b, waaa_ref[:, sl]))
        g = _dot(zg_sig, wgate_ref[:, sl])
        kk = k * kk_ref[:, sl]
        n2 = _dot((kk * kk).astype(BF16), ones_bd)
        kkn = kk * lax.rsqrt(jnp.maximum(n2, 1e-24))
        k2 = k * (1.0 + (a - 1.0) * ka_ref[:, sl])
        bvec = kkn * a
        opf_ref[slot, _OPS_F.index("rkk"), :, sl] = r * k2 * rk_ref[:, sl]
        opf_ref[slot, _OPS_F.index("v"), :, sl] = v
        opf_ref[slot, _OPS_F.index("g"), :, sl] = g
        yield
        for j, rs in enumerate(chunks):
            lcum = _dot(tril_2, _hilo_rows(logd[rs]))
            lmid = lcum[c // 2 - 1:c // 2, :]
            lend = lcum[c - 1:c, :]
            lexc = lcum - logd[rs]
            e_mid_from = jnp.exp(lmid - lcum)
            e_end = jnp.exp(lend - lcum)
            ops = dict(
                rt=r[rs] * jnp.exp(lcum - lmid), at=-kkn[rs] * jnp.exp(lexc - lmid),
                bt=bvec[rs] * e_mid_from, kt=k2[rs] * e_mid_from,
                r0=r[rs] * jnp.exp(lcum), a0=-kkn[rs] * jnp.exp(lexc),
                bh=bvec[rs] * e_end, kh=k2[rs] * e_end, vb=v[rs])
            for i, name in enumerate(_OPS_B):
                opb_ref[slot, i, rs, sl] = ops[name].astype(BF16)
            dend_ref[slot, j * SUBLANES:(j + 1) * SUBLANES, sl] = jnp.broadcast_to(
                jnp.exp(lend), (SUBLANES, LANES))
        yield
    prev_ref[...] = zr_ref[rows - 1:rows, :]


def _rwkv_chunks(opb_ref, opf_ref, dend_ref, slot, gn_g, gn_b, state_ref, o_ref, filler):
    c = RWKV_CHUNK
    row2 = _iota2((c, 2 * c), 0)
    col2 = _iota2((c, 2 * c), 1) & (c - 1)
    incl2 = row2 >= col2
    strict2 = row2 > col2
    first = _iota2((c, LANES), 1) < HEAD
    blockdiag = (_iota2((LANES, LANES), 0) >= HEAD) == (_iota2((LANES, LANES), 1) >= HEAD)
    ones_bd = blockdiag.astype(BF16)
    avg_bd = (blockdiag.astype(F32) * (1.0 / HEAD)).astype(BF16)
    zero_bd = jnp.zeros((LANES, LANES), BF16)
    avg_2 = jnp.concatenate([avg_bd, avg_bd], axis=0)
    stat_bd = jnp.concatenate([jnp.concatenate([avg_bd, zero_bd], axis=1),
                               jnp.concatenate([zero_bd, ones_bd], axis=1)], axis=0)
    pairs = range(N_PAIRS)
    sls = [slice(p * LANES, (p + 1) * LANES) for p in pairs]
    chunks = [slice(j * c, (j + 1) * c) for j in range(opb_ref.shape[2] // c)]
    tiles = [(j, rs, p, sl) for j, rs in enumerate(chunks) for p, sl in enumerate(sls)]
    opb = lambda name, rs, sl: opb_ref[slot, _OPS_B.index(name), rs, sl]
    opf = lambda name, rs, sl: opf_ref[slot, _OPS_F.index(name), rs, sl]

    grams = [
        _dot(jnp.concatenate([opb("at", rs, sl), opb("rt", rs, sl)], axis=0),
             jnp.concatenate([_stack_heads(opb("bt", rs, sl), first),
                              _stack_heads(opb("kt", rs, sl), first)], axis=0), _NT)
        for _, rs, _, sl in tiles]
    a_abs = [jnp.where(strict2, gm[:c, :2 * c], 0.0) for gm in grams]
    t_invs = [t.astype(BF16) for t in _unit_lower_inverses(a_abs, c, filler)]
    a_rbs = [jnp.where(incl2, gm[c:, :2 * c], 0.0).astype(BF16) for gm in grams]
    akvs = [
        _dot(jnp.concatenate([jnp.where(strict2, gm[:c, 2 * c:], 0.0),
                              jnp.where(incl2, gm[c:, 2 * c:], 0.0)], axis=0).astype(BF16),
             _stack_heads(opb("vb", rs, sl), first))
        for gm, (_, rs, _, sl) in zip(grams, tiles)]

    ta0s = [_dot(t, _stack_heads(opb("a0", rs, sl), first)).astype(BF16)
            for t, (_, rs, _, sl) in zip(t_invs, tiles)]
    tavs = [_dot(t, _stack_heads(akv[:c].astype(BF16), first)).astype(BF16)
            for t, akv in zip(t_invs, akvs)]
    next(filler, None)
    m_offs = [jnp.where(blockdiag, _dot(ta0, opb("bh", rs, sl), _TN), 0.0).astype(BF16)
              for ta0, (_, rs, _, sl) in zip(ta0s, tiles)]
    consts = [jnp.where(blockdiag,
                        _dot(jnp.concatenate([tav, opb("vb", rs, sl)], axis=0),
                             jnp.concatenate([opb("bh", rs, sl), opb("kh", rs, sl)], axis=0), _TN),
                        0.0)
              for tav, (_, rs, _, sl) in zip(tavs, tiles)]
    next(filler, None)
    ras = [(opb("r0", rs, sl).astype(F32) + _dot(a_rb, _stack_heads(ta0, first))).astype(BF16)
           for a_rb, ta0, (_, rs, _, sl) in zip(a_rbs, ta0s, tiles)]
    y_consts = [akv[c:] + _dot(a_rb, _stack_heads(tav, first))
                for akv, a_rb, tav in zip(akvs, a_rbs, tavs)]
    next(filler, None)

    states = [state_ref[p] for p in pairs]
    ys = []
    for j in range(len(chunks)):
        mine = slice(j * N_PAIRS, (j + 1) * N_PAIRS)
        s_bs = [s.astype(BF16) for s in states]
        ys += [_dot(ra, s_b, _NT) + yc for ra, yc, s_b in zip(ras[mine], y_consts[mine], s_bs)]
        states = [s * dend_ref[slot, j * SUBLANES:j * SUBLANES + 1, sl] + _dot(s_b, m) + cm
                  for s, s_b, m, cm, sl in zip(states, s_bs, m_offs[mine], consts[mine], sls)]
    for p, s in enumerate(states):
        state_ref[p] = s

    means = [_dot(_hilo_cols(y), avg_2) for y in ys]
    ycs = [y - m for y, m in zip(ys, means)]
    stats = [_dot(jnp.concatenate([yc * yc, opf("rkk", rs, sl)], axis=1).astype(BF16), stat_bd)
             for yc, (_, rs, _, sl) in zip(ycs, tiles)]
    for yc, st, (_, rs, _, sl) in zip(ycs, stats, tiles):
        o_ref[rs, sl] = ((yc * lax.rsqrt(st[:, :LANES] + GN_EPS) * gn_g[:, sl] + gn_b[:, sl]
                          + st[:, LANES:] * opf("v", rs, sl)) * opf("g", rs, sl))


def _gmlp_stages(zg_ref, lng_ref, lnb_ref, wst_ref, bs_ref, o_ref):
    gelu = lambda t: 0.5 * t * (1.0 + lax.erf(t * 0.7071067811865476))
    n_tiles = GMLP_WIDTH // LANES
    cols = lambda base, p: slice(base + p * LANES, base + (p + 1) * LANES)
    rows = zg_ref.shape[0]
    chunks = [slice(j * CHUNK, (j + 1) * CHUNK) for j in range(rows // CHUNK)]
    vs = []
    for p in range(n_tiles):
        vs.append(gelu(zg_ref[:, cols(GMLP_WIDTH, p)]))
        yield
    m = sum(jnp.sum(t, axis=-1, keepdims=True) for t in vs) * (1.0 / GMLP_WIDTH)
    vcs = [t - m for t in vs]
    var = sum(jnp.sum(t * t, axis=-1, keepdims=True) for t in vcs) * (1.0 / GMLP_WIDTH)
    inv = lax.rsqrt(var + LN_EPS)
    yield
    first = _iota2((CHUNK, LANES), 1) < HEAD
    for p in range(n_tiles):
        vn = (vcs[p] * inv * lng_ref[:, cols(0, p)] + lnb_ref[:, cols(0, p)]).astype(BF16)
        u = gelu(zg_ref[:, cols(0, p)])
        for rs in chunks:
            mixed = _dot(wst_ref[p], _stack_heads(vn[rs], first))
            o_ref[rs, cols(RWKV_WIDTH, p)] = u[rs] * (mixed + bs_ref[:, cols(0, p)])
        yield


def _interleaved(*stage_generators):
    for _ in itertools.zip_longest(*stage_generators):
        yield


def _mixer_kernel(zr_ref, zg_ref, mu_ref, w0_ref, wdec_ref, a0_ref, waaa_ref, wgate_ref, kk_ref,
                  ka_ref, rk_ref, gng_ref, gnb_ref, lng_ref, lnb_ref, ws_ref, bs_ref,
                  o_ref, state_ref, prev_ref, wst_ref, opb_ref, opf_ref, dend_ref):
    s = pl.program_id(0)

    @pl.when(s == 0)
    def _():
        state_ref[...] = jnp.zeros_like(state_ref)
        prev_ref[...] = jnp.zeros_like(prev_ref)
        opb_ref[...] = jnp.zeros_like(opb_ref)
        opf_ref[...] = jnp.zeros_like(opf_ref)
        dend_ref[...] = jnp.zeros_like(dend_ref)
        causal = _iota2((CHUNK, CHUNK), 0) >= _iota2((CHUNK, CHUNK), 1)
        for h in range(ws_ref.shape[0]):
            wst_ref[h // 2, :, (h % 2) * CHUNK:(h % 2 + 1) * CHUNK] = (
                jnp.where(causal, ws_ref[h], 0.0).astype(BF16))

    prepare = _rwkv_prepare_stages(zr_ref, prev_ref, mu_ref, w0_ref, wdec_ref, a0_ref, waaa_ref,
                                   wgate_ref, kk_ref, ka_ref, rk_ref, opb_ref, opf_ref, dend_ref,
                                   s % 2)
    gmlp = _gmlp_stages(zg_ref, lng_ref, lnb_ref, wst_ref, bs_ref, o_ref)
    filler = _interleaved(prepare, gmlp)
    _rwkv_chunks(opb_ref, opf_ref, dend_ref, (s + 1) % 2, gng_ref[...], gnb_ref[...],
                 state_ref, o_ref, filler)
    for _ in filler:
        pass


def _mixer(zr, zg, mu, w0, wdec, a0, waaa, wgate, k_k, k_a, r_k, gn_g, gn_b, ln_g, ln_b, w_s, b_s):
    rows = MIX_CHUNKS * CHUNK
    n = zr.shape[0] // rows
    n_heads = w_s.shape[0]

    def full(arr):
        nd = arr.ndim
        return pl.BlockSpec(arr.shape, lambda s: (0,) * nd)

    params = (mu, w0, wdec, a0, waaa, wgate, k_k, k_a, r_k, gn_g, gn_b, ln_g, ln_b, w_s, b_s)
    return pl.pallas_call(
        _mixer_kernel,
        out_shape=jax.ShapeDtypeStruct((n * rows, D_MODEL), F32),
        grid=(n + 1,),
        in_specs=[pl.BlockSpec((rows, RWKV_COLS), lambda s: (jnp.minimum(s, n - 1), 0)),
                  pl.BlockSpec((rows, 2 * GMLP_WIDTH), lambda s: (jnp.maximum(s - 1, 0), 0))]
                 + [full(a) for a in params],
        out_specs=pl.BlockSpec((rows, D_MODEL), lambda s: (jnp.maximum(s - 1, 0), 0)),
        scratch_shapes=[
            pltpu.VMEM((N_PAIRS, LANES, LANES), F32),
            pltpu.VMEM((1, RWKV_COLS), F32),
            pltpu.VMEM((n_heads // 2, CHUNK, 2 * CHUNK), BF16),
            pltpu.VMEM((2, len(_OPS_B), rows, RWKV_WIDTH), BF16),
            pltpu.VMEM((2, len(_OPS_F), rows, RWKV_WIDTH), F32),
            pltpu.VMEM((2, rows // RWKV_CHUNK * SUBLANES, RWKV_WIDTH), F32),
        ],
        compiler_params=pltpu.CompilerParams(
            dimension_semantics=("arbitrary",), vmem_limit_bytes=VMEM_LIMIT),
        name="mixer",
    )(zr, zg, *params)


def kernel(x, p, norm_ffn1, ffn1_w1, ffn1_w3, ffn1_w2, norm_mix, w_in, shift_mu, rwkv_w0,
           rwkv_w_decay, rwkv_a0, rwkv_w_aaa, rwkv_w_gate, rwkv_k_k, rwkv_k_a, rwkv_r_k,
           rwkv_gn_g, rwkv_gn_b, sgu_ln_g, sgu_ln_b, sgu_w_s, sgu_b_s, w_out, norm_ffn2,
           ffn2_w1, ffn2_w3, ffn2_w2, norm_ple, w_ple_gate, w_ple, norm_final):
    assert x.shape[0] == 1 and p.shape[0] == 1, "one batch row and one layer, as the problem states"
    row = lambda a: a.reshape(1, -1)
    bf = lambda a: a.astype(BF16)
    h, zr, zg = _dense_call(
        _pre_mixer_kernel, "pre_mixer", [x[0]],
        [row(norm_ffn1[0]), bf(ffn1_w1[0]), bf(ffn1_w3[0]), bf(ffn1_w2[0]),
         row(norm_mix[0]), bf(w_in[0])],
        (D_MODEL, RWKV_COLS, 2 * GMLP_WIDTH))
    lora = rwkv_w_decay.shape[1]
    pad = jnp.zeros((LANES - lora, RWKV_WIDTH), F32)
    wdec = jnp.concatenate([rwkv_w_decay[0], pad], axis=0)
    waaa = jnp.concatenate([pad, rwkv_w_aaa[0]], axis=0)
    bias = jnp.repeat(sgu_b_s[0].T, HEAD, axis=1)
    y = _mixer(zr, zg, row(shift_mu[0]), row(rwkv_w0[0]), bf(wdec), row(rwkv_a0[0]), bf(waaa),
               bf(rwkv_w_gate[0]), row(rwkv_k_k[0]), row(rwkv_k_a[0]), row(rwkv_r_k[0]),
               row(rwkv_gn_g[0]), row(rwkv_gn_b[0]), row(sgu_ln_g[0]), row(sgu_ln_b[0]),
               sgu_w_s[0], bias)
    (out,) = _dense_call(
        _post_mixer_kernel, "post_mixer", [h, y, p[0, 0]],
        [bf(w_out[0]), row(norm_ffn2[0]), bf(ffn2_w1[0]), bf(ffn2_w3[0]), bf(ffn2_w2[0]),
         row(norm_ple[0]), bf(w_ple_gate[0]), bf(w_ple[0]), row(norm_final)],
        (D_MODEL,))
    return out[None]
```

```python
import itertools

import jax
import jax.numpy as jnp
from jax import lax
from jax.experimental import pallas as pl
from jax.experimental.pallas import tpu as pltpu

F32 = jnp.float32
BF16 = jnp.bfloat16

D_MODEL = 1024
D_FF = 2816
PLE_DIM = 256
RWKV_WIDTH = 512
RWKV_COLS = 1792
GMLP_WIDTH = 512
IN_COLS = 2816
HEAD = 64
LANES = 128
SUBLANES = 8
N_PAIRS = RWKV_WIDTH // LANES
CHUNK = 128
RWKV_CHUNK = 64
MIX_CHUNKS = 2
INV_BASE = 16
RMS_EPS = 1e-6
LN_EPS = 1e-5
GN_EPS = 64e-5
VMEM_LIMIT = 56 * 1024 * 1024

_NN = (((1,), (0,)), ((), ()))
_NT = (((1,), (1,)), ((), ()))
_TN = (((0,), (0,)), ((), ()))


def _dot(a, b, dn=_NN):
    return lax.dot_general(a, b, dn, preferred_element_type=F32)


def _split2(x):
    hi = x.astype(BF16)
    lo = (x - hi.astype(F32)).astype(BF16)
    return hi, lo


def _hilo_cols(x):
    return jnp.concatenate(_split2(x), axis=1)


def _hilo_rows(x):
    return jnp.concatenate(_split2(x), axis=0)


def _rms(x, g):
    ms = jnp.mean(x * x, axis=-1, keepdims=True)
    return x * lax.rsqrt(ms + RMS_EPS) * g


FF_CHUNK = 256


def _swiglu_half_step(h, g_ref, w1_ref, w3_ref, w2_ref, hm_ref):
    xn = _rms(h, g_ref[...]).astype(BF16)
    for j in range(hm_ref.shape[1] // FF_CHUNK):
        cols = slice(j * FF_CHUNK, (j + 1) * FF_CHUNK)
        a = _dot(xn, w1_ref[:, cols])
        b = _dot(xn, w3_ref[:, cols])
        hm_ref[:, cols] = (a * jax.nn.sigmoid(a) * b).astype(BF16)
    return h + 0.5 * _dot(hm_ref[...], w2_ref[...])


def _pre_mixer_kernel(x_ref, g1_ref, w1_ref, w3_ref, w2_ref, gm_ref, win_ref,
                      h_ref, zr_ref, zg_ref, hm_ref):
    h = _swiglu_half_step(x_ref[...], g1_ref, w1_ref, w3_ref, w2_ref, hm_ref)
    h_ref[...] = h
    z = _dot(_rms(h, gm_ref[...]).astype(BF16), win_ref[...])
    zr_ref[...] = z[:, :RWKV_COLS]
    zg_ref[...] = z[:, RWKV_COLS:]


def _post_mixer_kernel(h_ref, y_ref, p_ref, wout_ref, g2_ref, w1_ref, w3_ref, w2_ref,
                       gp_ref, wg_ref, wp_ref, gf_ref, o_ref, hm_ref):
    h = h_ref[...] + _dot(y_ref[...], wout_ref[...])
    h = _swiglu_half_step(h, g2_ref, w1_ref, w3_ref, w2_ref, hm_ref)
    gate = jax.nn.sigmoid(_dot(_rms(h, gp_ref[...]).astype(BF16), wg_ref[...]))
    e = _dot(p_ref[...].astype(BF16), wp_ref[...])
    o_ref[...] = _rms(h + gate * e, gf_ref[...])


def _rows(tm, width):
    return pl.BlockSpec((tm, width), lambda i: (i, 0))


def _resident(arr):
    nd = arr.ndim
    return pl.BlockSpec(arr.shape, lambda i: (0,) * nd, pipeline_mode=pl.Buffered(1))


def _dense_call(body, name, row_inputs, params, out_widths, *, tm=512):
    t = row_inputs[0].shape[0]
    outs = tuple(jax.ShapeDtypeStruct((t, w), F32) for w in out_widths)
    return pl.pallas_call(
        body,
        out_shape=outs,
        grid=(t // tm,),
        in_specs=[_rows(tm, a.shape[1]) for a in row_inputs] + [_resident(a) for a in params],
        out_specs=tuple(_rows(tm, w) for w in out_widths),
        scratch_shapes=[pltpu.VMEM((tm, D_FF), BF16)],
        compiler_params=pltpu.CompilerParams(
            dimension_semantics=("arbitrary",), vmem_limit_bytes=VMEM_LIMIT),
        name=name,
    )(*row_inputs, *params)


def _iota2(shape, dim):
    return lax.broadcasted_iota(jnp.int32, shape, dim)


def _stack_heads(x, first):
    zero = jnp.zeros_like(x)
    return jnp.concatenate([jnp.where(first, x, zero), jnp.where(first, zero, x)], axis=0)


def _nilpotent_inverses(xs, eye, index, bd, filler):
    w = eye.shape[1]
    pows = [x.astype(BF16) for x in xs]
    prods = [(eye + p).astype(BF16) for p in pows]
    pows = [_dot(p, bd(p)).astype(BF16) for p in pows]
    next(filler, None)
    covered = 4
    while covered < index:
        outs = [_dot(p, jnp.concatenate([bd(p), bd(t)], axis=1)) for p, t in zip(pows, prods)]
        pows = [o[:, :w].astype(BF16) for o in outs]
        prods = [(t + o[:, w:]).astype(BF16) for t, o in zip(prods, outs)]
        next(filler, None)
        covered *= 2
    prods = [(t + _dot(p, bd(t))).astype(BF16) for p, t in zip(pows, prods)]
    next(filler, None)
    return prods


def _unit_lower_inverses(mats, c, filler):
    row = _iota2((c, 2 * c), 0)
    lane = _iota2((c, 2 * c), 1)
    col = lane & (c - 1)
    left = lane < c
    shift = INV_BASE.bit_length() - 1
    diag_block = (row >> shift) == (col >> shift)
    eye = jnp.where(row == col, 1.0, 0.0)

    def bd(x):
        zero = jnp.zeros_like(x)
        return jnp.concatenate([jnp.where(left, x, zero), jnp.where(left, zero, x)], axis=0)

    t_diag = _nilpotent_inverses([jnp.where(diag_block, a, 0.0) for a in mats], eye, INV_BASE,
                                 bd, filler)
    n_mats = [_dot(t, bd(jnp.where(diag_block, 0.0, a))) for t, a in zip(t_diag, mats)]
    next(filler, None)
    t_off = _nilpotent_inverses(n_mats, eye, c // INV_BASE, bd, filler)
    return [_dot(t, bd(td)).astype(BF16) for t, td in zip(t_off, t_diag)]


_OPS_B = ("rt", "at", "bt", "kt", "r0", "a0", "bh", "kh", "vb", "rkk")
_OPS_F = ("v", "g")


def _rwkv_prepare_stages(zr_ref, prev_ref, mu_ref, w0_ref, wdec_ref, a0_ref, waaa_ref,
                         wgate_ref, kk_ref, ka_ref, rk_ref, opb_ref, opf_ref, dend_ref, slot):
    c = RWKV_CHUNK
    rows = zr_ref.shape[0]
    chunks = [slice(j * c, (j + 1) * c) for j in range(rows // c)]
    first_row = _iota2((rows, LANES), 0) == 0

    def shifted(col):
        cols = slice(col, col + LANES)
        z = zr_ref[:, cols]
        zp = jnp.where(first_row, prev_ref[:, cols], pltpu.roll(z, 1, 0))
        return z + (zp - z) * mu_ref[:, cols]

    w = RWKV_WIDTH
    zwa = shifted(3 * w)
    zwa_tanh = jnp.tanh(zwa).astype(BF16)
    zwa_b = zwa.astype(BF16)
    zg_sig = jax.nn.sigmoid(shifted(3 * w + LANES)).astype(BF16)
    blockdiag = (_iota2((LANES, LANES), 0) >= HEAD) == (_iota2((LANES, LANES), 1) >= HEAD)
    ones_bd = blockdiag.astype(BF16)
    row2 = _iota2((c, 2 * c), 0)
    col2 = _iota2((c, 2 * c), 1) & (c - 1)
    tril_2 = (row2 >= col2).astype(BF16)
    yield
    for p in range(N_PAIRS):
        sl = slice(p * LANES, (p + 1) * LANES)
        r, k, v = shifted(p * LANES), shifted(w + p * LANES), shifted(2 * w + p * LANES)
        lw = w0_ref[:, sl] + _dot(zwa_tanh, wdec_ref[:, sl])
        logd = -0.6065306597126334 * jax.nn.sigmoid(lw)
        a = jax.nn.sigmoid(a0_ref[:, sl] + _dot(zwa_b, waaa_ref[:, sl]))
        g = _dot(zg_sig, wgate_ref[:, sl])
        kk = k * kk_ref[:, sl]
        n2 = _dot((kk * kk).astype(BF16), ones_bd)
        kkn = kk * lax.rsqrt(jnp.maximum(n2, 1e-24))
        k2 = k * (1.0 + (a - 1.0) * ka_ref[:, sl])
        bvec = kkn * a
        opb_ref[slot, _OPS_B.index("rkk"), :, sl] = (r * k2 * rk_ref[:, sl]).astype(BF16)
        opf_ref[slot, _OPS_F.index("v"), :, sl] = v
        opf_ref[slot, _OPS_F.index("g"), :, sl] = g
        yield
        for j, rs in enumerate(chunks):
            lcum = _dot(tril_2, _hilo_rows(logd[rs]))
            lmid = lcum[c // 2 - 1:c // 2, :]
            lend = lcum[c - 1:c, :]
            lexc = lcum - logd[rs]
            e_mid_from = jnp.exp(lmid - lcum)
            e_end = jnp.exp(lend - lcum)
            ops = dict(
                rt=r[rs] * jnp.exp(lcum - lmid), at=-kkn[rs] * jnp.exp(lexc - lmid),
                bt=bvec[rs] * e_mid_from, kt=k2[rs] * e_mid_from,
                r0=r[rs] * jnp.exp(lcum), a0=-kkn[rs] * jnp.exp(lexc),
                bh=bvec[rs] * e_end, kh=k2[rs] * e_end, vb=v[rs])
            for name, value in ops.items():
                opb_ref[slot, _OPS_B.index(name), rs, sl] = value.astype(BF16)
            dend_ref[slot, j * SUBLANES:(j + 1) * SUBLANES, sl] = jnp.broadcast_to(
                jnp.exp(lend), (SUBLANES, LANES))
        yield
    prev_ref[...] = zr_ref[rows - 1:rows, :]


def _rwkv_chunks(opb_ref, opf_ref, dend_ref, slot, gn_g, gn_b, state_ref, o_ref, filler):
    c = RWKV_CHUNK
    row2 = _iota2((c, 2 * c), 0)
    col2 = _iota2((c, 2 * c), 1) & (c - 1)
    incl2 = row2 >= col2
    strict2 = row2 > col2
    first = _iota2((c, LANES), 1) < HEAD
    blockdiag = (_iota2((LANES, LANES), 0) >= HEAD) == (_iota2((LANES, LANES), 1) >= HEAD)
    ones_bd = blockdiag.astype(BF16)
    avg_bd = (blockdiag.astype(F32) * (1.0 / HEAD)).astype(BF16)
    zero_bd = jnp.zeros((LANES, LANES), BF16)
    avg_2 = jnp.concatenate([avg_bd, avg_bd], axis=0)
    stat_bd = jnp.concatenate([jnp.concatenate([avg_bd, zero_bd], axis=1),
                               jnp.concatenate([zero_bd, ones_bd], axis=1)], axis=0)
    pairs = range(N_PAIRS)
    sls = [slice(p * LANES, (p + 1) * LANES) for p in pairs]
    chunks = [slice(j * c, (j + 1) * c) for j in range(opb_ref.shape[2] // c)]
    tiles = [(j, rs, p, sl) for j, rs in enumerate(chunks) for p, sl in enumerate(sls)]
    opb = lambda name, rs, sl: opb_ref[slot, _OPS_B.index(name), rs, sl]
    opf = lambda name, rs, sl: opf_ref[slot, _OPS_F.index(name), rs, sl]

    grams = [
        _dot(jnp.concatenate([opb("at", rs, sl), opb("rt", rs, sl)], axis=0),
             jnp.concatenate([_stack_heads(opb("bt", rs, sl), first),
                              _stack_heads(opb("kt", rs, sl), first)], axis=0), _NT)
        for _, rs, _, sl in tiles]
    a_abs = [jnp.where(strict2, gm[:c, :2 * c], 0.0).astype(BF16) for gm in grams]
    t_invs = _unit_lower_inverses(a_abs, c, filler)
    a_rbs = [jnp.where(incl2, gm[c:, :2 * c], 0.0).astype(BF16) for gm in grams]
    akvs = [
        _dot(jnp.concatenate([jnp.where(strict2, gm[:c, 2 * c:], 0.0),
                              jnp.where(incl2, gm[c:, 2 * c:], 0.0)], axis=0).astype(BF16),
             _stack_heads(opb("vb", rs, sl), first))
        for gm, (_, rs, _, sl) in zip(grams, tiles)]

    ta0s = [_dot(t, _stack_heads(opb("a0", rs, sl), first)).astype(BF16)
            for t, (_, rs, _, sl) in zip(t_invs, tiles)]
    tavs = [_dot(t, _stack_heads(akv[:c].astype(BF16), first)).astype(BF16)
            for t, akv in zip(t_invs, akvs)]
    next(filler, None)
    m_offs = [jnp.where(blockdiag, _dot(ta0, opb("bh", rs, sl), _TN), 0.0).astype(BF16)
              for ta0, (_, rs, _, sl) in zip(ta0s, tiles)]
    consts = [jnp.where(blockdiag,
                        _dot(jnp.concatenate([tav, opb("vb", rs, sl)], axis=0),
                             jnp.concatenate([opb("bh", rs, sl), opb("kh", rs, sl)], axis=0), _TN),
                        0.0)
              for tav, (_, rs, _, sl) in zip(tavs, tiles)]
    next(filler, None)
    ras = [(opb("r0", rs, sl).astype(F32) + _dot(a_rb, _stack_heads(ta0, first))).astype(BF16)
           for a_rb, ta0, (_, rs, _, sl) in zip(a_rbs, ta0s, tiles)]
    y_consts = [akv[c:] + _dot(a_rb, _stack_heads(tav, first))
                for akv, a_rb, tav in zip(akvs, a_rbs, tavs)]
    next(filler, None)

    states = [state_ref[p] for p in pairs]
    ys = []
    for j in range(len(chunks)):
        mine = slice(j * N_PAIRS, (j + 1) * N_PAIRS)
        s_bs = [s.astype(BF16) for s in states]
        ys += [_dot(ra, s_b, _NT) + yc for ra, yc, s_b in zip(ras[mine], y_consts[mine], s_bs)]
        states = [s * dend_ref[slot, j * SUBLANES:j * SUBLANES + 1, sl] + _dot(s_b, m) + cm
                  for s, s_b, m, cm, sl in zip(states, s_bs, m_offs[mine], consts[mine], sls)]
    for p, s in enumerate(states):
        state_ref[p] = s

    means = [_dot(_hilo_cols(y), avg_2) for y in ys]
    ycs = [y - m for y, m in zip(ys, means)]
    stats = [_dot(jnp.concatenate([(yc * yc).astype(BF16), opb("rkk", rs, sl)], axis=1), stat_bd)
             for yc, (_, rs, _, sl) in zip(ycs, tiles)]
    for yc, st, (_, rs, _, sl) in zip(ycs, stats, tiles):
        o_ref[rs, sl] = ((yc * lax.rsqrt(st[:, :LANES] + GN_EPS) * gn_g[:, sl] + gn_b[:, sl]
                          + st[:, LANES:] * opf("v", rs, sl)) * opf("g", rs, sl)
                         ).astype(o_ref.dtype)


def _gmlp_stages(zg_ref, lng_ref, lnb_ref, wst_ref, bs_ref, o_ref):
    gelu = lambda t: 0.5 * t * (1.0 + lax.erf(t * 0.7071067811865476))
    n_tiles = GMLP_WIDTH // LANES
    cols = lambda base, p: slice(base + p * LANES, base + (p + 1) * LANES)
    rows = zg_ref.shape[0]
    chunks = [slice(j * CHUNK, (j + 1) * CHUNK) for j in range(rows // CHUNK)]
    vs = []
    for p in range(n_tiles):
        vs.append(gelu(zg_ref[:, cols(GMLP_WIDTH, p)]))
        yield
    m = sum(jnp.sum(t, axis=-1, keepdims=True) for t in vs) * (1.0 / GMLP_WIDTH)
    vcs = [t - m for t in vs]
    var = sum(jnp.sum(t * t, axis=-1, keepdims=True) for t in vcs) * (1.0 / GMLP_WIDTH)
    inv = lax.rsqrt(var + LN_EPS)
    yield
    first = _iota2((CHUNK, LANES), 1) < HEAD
    for p in range(n_tiles):
        vn = (vcs[p] * inv * lng_ref[:, cols(0, p)] + lnb_ref[:, cols(0, p)]).astype(BF16)
        u = gelu(zg_ref[:, cols(0, p)])
        for rs in chunks:
            mixed = _dot(wst_ref[p], _stack_heads(vn[rs], first))
            o_ref[rs, cols(RWKV_WIDTH, p)] = (
                u[rs] * (mixed + bs_ref[:, cols(0, p)])).astype(o_ref.dtype)
        yield


def _interleaved(*stage_generators):
    for _ in itertools.zip_longest(*stage_generators):
        yield


def _mixer_kernel(zr_ref, zg_ref, mu_ref, w0_ref, wdec_ref, a0_ref, waaa_ref, wgate_ref, kk_ref,
                  ka_ref, rk_ref, gng_ref, gnb_ref, lng_ref, lnb_ref, ws_ref, bs_ref,
                  o_ref, state_ref, prev_ref, wst_ref, opb_ref, opf_ref, dend_ref):
    s = pl.program_id(0)

    @pl.when(s == 0)
    def _():
        state_ref[...] = jnp.zeros_like(state_ref)
        prev_ref[...] = jnp.zeros_like(prev_ref)
        opb_ref[...] = jnp.zeros_like(opb_ref)
        opf_ref[...] = jnp.zeros_like(opf_ref)
        dend_ref[...] = jnp.zeros_like(dend_ref)
        causal = _iota2((CHUNK, CHUNK), 0) >= _iota2((CHUNK, CHUNK), 1)
        for h in range(ws_ref.shape[0]):
            wst_ref[h // 2, :, (h % 2) * CHUNK:(h % 2 + 1) * CHUNK] = (
                jnp.where(causal, ws_ref[h], 0.0).astype(BF16))

    prepare = _rwkv_prepare_stages(zr_ref, prev_ref, mu_ref, w0_ref, wdec_ref, a0_ref, waaa_ref,
                                   wgate_ref, kk_ref, ka_ref, rk_ref, opb_ref, opf_ref, dend_ref,
                                   s % 2)
    gmlp = _gmlp_stages(zg_ref, lng_ref, lnb_ref, wst_ref, bs_ref, o_ref)
    filler = _interleaved(prepare, gmlp)
    _rwkv_chunks(opb_ref, opf_ref, dend_ref, (s + 1) % 2, gng_ref[...], gnb_ref[...],
                 state_ref, o_ref, filler)
    for _ in filler:
        pass


def _mixer(zr, zg, mu, w0, wdec, a0, waaa, wgate, k_k, k_a, r_k, gn_g, gn_b, ln_g, ln_b, w_s, b_s):
    rows = MIX_CHUNKS * CHUNK
    n = zr.shape[0] // rows
    n_heads = w_s.shape[0]

    def full(arr):
        nd = arr.ndim
        return pl.BlockSpec(arr.shape, lambda s: (0,) * nd)

    params = (mu, w0, wdec, a0, waaa, wgate, k_k, k_a, r_k, gn_g, gn_b, ln_g, ln_b, w_s, b_s)
    return pl.pallas_call(
        _mixer_kernel,
        out_shape=jax.ShapeDtypeStruct((n * rows, D_MODEL), BF16),
        grid=(n + 1,),
        in_specs=[pl.BlockSpec((rows, RWKV_COLS), lambda s: (jnp.minimum(s, n - 1), 0)),
                  pl.BlockSpec((rows, 2 * GMLP_WIDTH), lambda s: (jnp.maximum(s - 1, 0), 0))]
                 + [full(a) for a in params],
        out_specs=pl.BlockSpec((rows, D_MODEL), lambda s: (jnp.maximum(s - 1, 0), 0)),
        scratch_shapes=[
            pltpu.VMEM((N_PAIRS, LANES, LANES), F32),
            pltpu.VMEM((1, RWKV_COLS), F32),
            pltpu.VMEM((n_heads // 2, CHUNK, 2 * CHUNK), BF16),
            pltpu.VMEM((2, len(_OPS_B), rows, RWKV_WIDTH), BF16),
            pltpu.VMEM((2, len(_OPS_F), rows, RWKV_WIDTH), F32),
            pltpu.VMEM((2, rows // RWKV_CHUNK * SUBLANES, RWKV_WIDTH), F32),
        ],
        compiler_params=pltpu.CompilerParams(
            dimension_semantics=("arbitrary",), vmem_limit_bytes=VMEM_LIMIT),
        name="mixer",
    )(zr, zg, *params)


def kernel(x, p, norm_ffn1, ffn1_w1, ffn1_w3, ffn1_w2, norm_mix, w_in, shift_mu, rwkv_w0,
           rwkv_w_decay, rwkv_a0, rwkv_w_aaa, rwkv_w_gate, rwkv_k_k, rwkv_k_a, rwkv_r_k,
           rwkv_gn_g, rwkv_gn_b, sgu_ln_g, sgu_ln_b, sgu_w_s, sgu_b_s, w_out, norm_ffn2,
           ffn2_w1, ffn2_w3, ffn2_w2, norm_ple, w_ple_gate, w_ple, norm_final):
    assert x.shape[0] == 1 and p.shape[0] == 1, "one batch row and one layer, as the problem states"
    row = lambda a: a.reshape(1, -1)
    bf = lambda a: a.astype(BF16)
    h, zr, zg = _dense_call(
        _pre_mixer_kernel, "pre_mixer", [x[0]],
        [row(norm_ffn1[0]), bf(ffn1_w1[0]), bf(ffn1_w3[0]), bf(ffn1_w2[0]),
         row(norm_mix[0]), bf(w_in[0])],
        (D_MODEL, RWKV_COLS, 2 * GMLP_WIDTH))
    lora = rwkv_w_decay.shape[1]
    pad = jnp.zeros((LANES - lora, RWKV_WIDTH), F32)
    wdec = jnp.concatenate([rwkv_w_decay[0], pad], axis=0)
    waaa = jnp.concatenate([pad, rwkv_w_aaa[0]], axis=0)
    bias = jnp.repeat(sgu_b_s[0].T, HEAD, axis=1)
    y = _mixer(zr, zg, row(shift_mu[0]), row(rwkv_w0[0]), bf(wdec), row(rwkv_a0[0]), bf(waaa),
               bf(rwkv_w_gate[0]), row(rwkv_k_k[0]), row(rwkv_k_a[0]), row(rwkv_r_k[0]),
               row(rwkv_gn_g[0]), row(rwkv_gn_b[0]), row(sgu_ln_g[0]), row(sgu_ln_b[0]),
               sgu_w_s[0], bias)
    (out,) = _dense_call(
        _post_mixer_kernel, "post_mixer", [h, y, p[0, 0]],
        [bf(w_out[0]), row(norm_ffn2[0]), bf(ffn2_w1[0]), bf(ffn2_w3[0]), bf(ffn2_w2[0]),
         row(norm_ple[0]), bf(w_ple_gate[0]), bf(w_ple[0]), row(norm_final)],
        (D_MODEL,))
    return out[None]
```

```python
import itertools

import jax
import jax.numpy as jnp
from jax import lax
from jax.experimental import pallas as pl
from jax.experimental.pallas import tpu as pltpu

F32 = jnp.float32
BF16 = jnp.bfloat16

D_MODEL = 1024
D_FF = 2816
PLE_DIM = 256
RWKV_WIDTH = 512
RWKV_COLS = 1792
GMLP_WIDTH = 512
IN_COLS = 2816
HEAD = 64
LANES = 128
SUBLANES = 8
N_PAIRS = RWKV_WIDTH // LANES
CHUNK = 128
RWKV_CHUNK = 64
MIX_CHUNKS = 2
INV_BASE = 16
RMS_EPS = 1e-6
LN_EPS = 1e-5
GN_EPS = 64e-5
VMEM_LIMIT = 56 * 1024 * 1024

_NN = (((1,), (0,)), ((), ()))
_NT = (((1,), (1,)), ((), ()))
_TN = (((0,), (0,)), ((), ()))


def _dot(a, b, dn=_NN):
    return lax.dot_general(a, b, dn, preferred_element_type=F32)


def _split2(x):
    hi = x.astype(BF16)
    lo = (x - hi.astype(F32)).astype(BF16)
    return hi, lo


def _hilo_cols(x):
    return jnp.concatenate(_split2(x), axis=1)


def _hilo_rows(x):
    return jnp.concatenate(_split2(x), axis=0)


def _rms(x, g):
    ms = jnp.mean(x * x, axis=-1, keepdims=True)
    return x * lax.rsqrt(ms + RMS_EPS) * g


FF_CHUNK = 256


def _swiglu_hidden(xn, w1_ref, w3_ref, hm_ref, filler):
    for j in range(hm_ref.shape[1] // FF_CHUNK):
        cols = slice(j * FF_CHUNK, (j + 1) * FF_CHUNK)
        a = _dot(xn, w1_ref[:, cols])
        b = _dot(xn, w3_ref[:, cols])
        hm_ref[:, cols] = (a * jax.nn.sigmoid(a) * b).astype(BF16)
        next(filler, None)


def _rms_stages(x_ref, g_ref, out_ref, slot, rows_per_stage=64):
    for start in range(0, x_ref.shape[0], rows_per_stage):
        rs = slice(start, start + rows_per_stage)
        out_ref[slot, rs, :] = _rms(x_ref[rs, :], g_ref[...]).astype(BF16)
        yield


def _pre_mixer_kernel(x_ref, xnext_ref, g1_ref, w1_ref, w3_ref, w2_ref, gm_ref, win_ref,
                      h_ref, zr_ref, zg_ref, hm_ref, xn1_ref, xn2_ref):
    i = pl.program_id(0)
    last = pl.num_programs(0) - 1

    def project(slot):
        z = _dot(xn2_ref[slot], win_ref[...])
        zr_ref[...] = z[:, :RWKV_COLS]
        zg_ref[...] = z[:, RWKV_COLS:]

    @pl.when(i == 0)
    def _():
        xn1_ref[0] = _rms(x_ref[...], g1_ref[...]).astype(BF16)
        xn2_ref[1] = jnp.zeros(xn2_ref.shape[1:], BF16)

    @pl.when(i < last)
    def _():
        filler = _rms_stages(xnext_ref, g1_ref, xn1_ref, (i + 1) % 2)
        _swiglu_hidden(xn1_ref[i % 2], w1_ref, w3_ref, hm_ref, filler)
        for _ in filler:
            pass
        acc = _dot(hm_ref[...], w2_ref[...])
        project((i + 1) % 2)
        h = x_ref[...] + 0.5 * acc
        h_ref[...] = h
        xn2_ref[i % 2] = _rms(h, gm_ref[...]).astype(BF16)

    @pl.when(i == last)
    def _():
        project((i + 1) % 2)


def _post_mixer_kernel(h_ref, y_ref, p_ref, wout_ref, g2_ref, w1_ref, w3_ref, w2_ref,
                       gp_ref, wg_ref, wp_ref, gf_ref, o_ref, hm_ref):
    h = h_ref[...] + _dot(y_ref[...], wout_ref[...])
    _swiglu_hidden(_rms(h, g2_ref[...]).astype(BF16), w1_ref, w3_ref, hm_ref, iter(()))
    h = h + 0.5 * _dot(hm_ref[...], w2_ref[...])
    gate = jax.nn.sigmoid(_dot(_rms(h, gp_ref[...]).astype(BF16), wg_ref[...]))
    e = _dot(p_ref[...].astype(BF16), wp_ref[...])
    o_ref[...] = _rms(h + gate * e, gf_ref[...])


def _rows(tm, width, tile_of_step=lambda i: i):
    return pl.BlockSpec((tm, width), lambda i: (tile_of_step(i), 0))


def _resident(arr):
    nd = arr.ndim
    return pl.BlockSpec(arr.shape, lambda i: (0,) * nd, pipeline_mode=pl.Buffered(1))


def _pre_mixer(x, params, *, tm=512):
    t, d = x.shape
    n = t // tm
    this = lambda i: jnp.minimum(i, n - 1)
    nxt = lambda i: jnp.minimum(i + 1, n - 1)
    prev = lambda i: jnp.maximum(i - 1, 0)
    return pl.pallas_call(
        _pre_mixer_kernel,
        out_shape=(jax.ShapeDtypeStruct((t, d), F32),
                   jax.ShapeDtypeStruct((t, RWKV_COLS), F32),
                   jax.ShapeDtypeStruct((t, 2 * GMLP_WIDTH), F32)),
        grid=(n + 1,),
        in_specs=[_rows(tm, d, this), _rows(tm, d, nxt)] + [_resident(a) for a in params],
        out_specs=(_rows(tm, d, this), _rows(tm, RWKV_COLS, prev),
                   _rows(tm, 2 * GMLP_WIDTH, prev)),
        scratch_shapes=[pltpu.VMEM((tm, D_FF), BF16),
                        pltpu.VMEM((2, tm, d), BF16),
                        pltpu.VMEM((2, tm, d), BF16)],
        compiler_params=pltpu.CompilerParams(
            dimension_semantics=("arbitrary",), vmem_limit_bytes=VMEM_LIMIT),
        name="pre_mixer",
    )(x, x, *params)


def _post_mixer(h, y, p, params, *, tm=512):
    t, d = h.shape
    return pl.pallas_call(
        _post_mixer_kernel,
        out_shape=jax.ShapeDtypeStruct((t, d), F32),
        grid=(t // tm,),
        in_specs=[_rows(tm, a.shape[1]) for a in (h, y, p)] + [_resident(a) for a in params],
        out_specs=_rows(tm, d),
        scratch_shapes=[pltpu.VMEM((tm, D_FF), BF16)],
        compiler_params=pltpu.CompilerParams(
            dimension_semantics=("arbitrary",), vmem_limit_bytes=VMEM_LIMIT),
        name="post_mixer",
    )(h, y, p, *params)


def _iota2(shape, dim):
    return lax.broadcasted_iota(jnp.int32, shape, dim)


def _stack_heads(x, first):
    zero = jnp.zeros_like(x)
    return jnp.concatenate([jnp.where(first, x, zero), jnp.where(first, zero, x)], axis=0)


def _nilpotent_inverses(xs, eye, index, bd, filler):
    w = eye.shape[1]
    pows = [x.astype(BF16) for x in xs]
    prods = [(eye + p).astype(BF16) for p in pows]
    pows = [_dot(p, bd(p)).astype(BF16) for p in pows]
    next(filler, None)
    covered = 4
    while covered < index:
        outs = [_dot(p, jnp.concatenate([bd(p), bd(t)], axis=1)) for p, t in zip(pows, prods)]
        pows = [o[:, :w].astype(BF16) for o in outs]
        prods = [(t + o[:, w:]).astype(BF16) for t, o in zip(prods, outs)]
        next(filler, None)
        covered *= 2
    prods = [(t + _dot(p, bd(t))).astype(BF16) for p, t in zip(pows, prods)]
    next(filler, None)
    return prods


def _unit_lower_inverses(mats, c, filler):
    row = _iota2((c, 2 * c), 0)
    lane = _iota2((c, 2 * c), 1)
    col = lane & (c - 1)
    left = lane < c
    shift = INV_BASE.bit_length() - 1
    diag_block = (row >> shift) == (col >> shift)
    eye = jnp.where(row == col, 1.0, 0.0)

    def bd(x):
        zero = jnp.zeros_like(x)
        return jnp.concatenate([jnp.where(left, x, zero), jnp.where(left, zero, x)], axis=0)

    t_diag = _nilpotent_inverses([jnp.where(diag_block, a, 0.0) for a in mats], eye, INV_BASE,
                                 bd, filler)
    n_mats = [_dot(t, bd(jnp.where(diag_block, 0.0, a))) for t, a in zip(t_diag, mats)]
    next(filler, None)
    t_off = _nilpotent_inverses(n_mats, eye, c // INV_BASE, bd, filler)
    return [_dot(t, bd(td)).astype(BF16) for t, td in zip(t_off, t_diag)]


_OPS_B = ("rt", "at", "bt", "kt", "r0", "a0", "bh", "kh", "vb", "rkk")
_OPS_F = ("v", "g")


def _rwkv_prepare_stages(zr_ref, prev_ref, mu_ref, w0_ref, wdec_ref, a0_ref, waaa_ref,
                         wgate_ref, kk_ref, ka_ref, rk_ref, opb_ref, opf_ref, dend_ref, slot):
    c = RWKV_CHUNK
    rows = zr_ref.shape[0]
    chunks = [slice(j * c, (j + 1) * c) for j in range(rows // c)]
    first_row = _iota2((rows, LANES), 0) == 0

    def shifted(col):
        cols = slice(col, col + LANES)
        z = zr_ref[:, cols]
        zp = jnp.where(first_row, prev_ref[:, cols], pltpu.roll(z, 1, 0))
        return z + (zp - z) * mu_ref[:, cols]

    w = RWKV_WIDTH
    zwa = shifted(3 * w)
    zwa_tanh = jnp.tanh(zwa).astype(BF16)
    zwa_b = zwa.astype(BF16)
    zg_sig = jax.nn.sigmoid(shifted(3 * w + LANES)).astype(BF16)
    blockdiag = (_iota2((LANES, LANES), 0) >= HEAD) == (_iota2((LANES, LANES), 1) >= HEAD)
    ones_bd = blockdiag.astype(BF16)
    row2 = _iota2((c, 2 * c), 0)
    col2 = _iota2((c, 2 * c), 1) & (c - 1)
    tril_2 = (row2 >= col2).astype(BF16)
    yield
    for p in range(N_PAIRS):
        sl = slice(p * LANES, (p + 1) * LANES)
        r, k, v = shifted(p * LANES), shifted(w + p * LANES), shifted(2 * w + p * LANES)
        lw = w0_ref[:, sl] + _dot(zwa_tanh, wdec_ref[:, sl])
        logd = -0.6065306597126334 * jax.nn.sigmoid(lw)
        a = jax.nn.sigmoid(a0_ref[:, sl] + _dot(zwa_b, waaa_ref[:, sl]))
        g = _dot(zg_sig, wgate_ref[:, sl])
        kk = k * kk_ref[:, sl]
        n2 = _dot((kk * kk).astype(BF16), ones_bd)
        kkn = kk * lax.rsqrt(jnp.maximum(n2, 1e-24))
        k2 = k * (1.0 + (a - 1.0) * ka_ref[:, sl])
        bvec = kkn * a
        opb_ref[slot, _OPS_B.index("rkk"), :, sl] = (r * k2 * rk_ref[:, sl]).astype(BF16)
        opf_ref[slot, _OPS_F.index("v"), :, sl] = v
        opf_ref[slot, _OPS_F.index("g"), :, sl] = g
        yield
        for j, rs in enumerate(chunks):
            lcum = _dot(tril_2, _hilo_rows(logd[rs]))
            lmid = lcum[c // 2 - 1:c // 2, :]
            lend = lcum[c - 1:c, :]
            lexc = lcum - logd[rs]
            e_mid_from = jnp.exp(lmid - lcum)
            e_end = jnp.exp(lend - lcum)
            ops = dict(
                rt=r[rs] * jnp.exp(lcum - lmid), at=-kkn[rs] * jnp.exp(lexc - lmid),
                bt=bvec[rs] * e_mid_from, kt=k2[rs] * e_mid_from,
                r0=r[rs] * jnp.exp(lcum), a0=-kkn[rs] * jnp.exp(lexc),
                bh=bvec[rs] * e_end, kh=k2[rs] * e_end, vb=v[rs])
            for name, value in ops.items():
                opb_ref[slot, _OPS_B.index(name), rs, sl] = value.astype(BF16)
            dend_ref[slot, j * SUBLANES:(j + 1) * SUBLANES, sl] = jnp.broadcast_to(
                jnp.exp(lend), (SUBLANES, LANES))
        yield
    prev_ref[...] = zr_ref[rows - 1:rows, :]


def _rwkv_chunks(opb_ref, opf_ref, dend_ref, slot, gn_g, gn_b, state_ref, o_ref, filler):
    c = RWKV_CHUNK
    row2 = _iota2((c, 2 * c), 0)
    col2 = _iota2((c, 2 * c), 1) & (c - 1)
    incl2 = row2 >= col2
    strict2 = row2 > col2
    first = _iota2((c, LANES), 1) < HEAD
    blockdiag = (_iota2((LANES, LANES), 0) >= HEAD) == (_iota2((LANES, LANES), 1) >= HEAD)
    ones_bd = blockdiag.astype(BF16)
    avg_bd = (blockdiag.astype(F32) * (1.0 / HEAD)).astype(BF16)
    zero_bd = jnp.zeros((LANES, LANES), BF16)
    avg_2 = jnp.concatenate([avg_bd, avg_bd], axis=0)
    stat_bd = jnp.concatenate([jnp.concatenate([avg_bd, zero_bd], axis=1),
                               jnp.concatenate([zero_bd, ones_bd], axis=1)], axis=0)
    pairs = range(N_PAIRS)
    sls = [slice(p * LANES, (p + 1) * LANES) for p in pairs]
    chunks = [slice(j * c, (j + 1) * c) for j in range(opb_ref.shape[2] // c)]
    tiles = [(j, rs, p, sl) for j, rs in enumerate(chunks) for p, sl in enumerate(sls)]
    opb = lambda name, rs, sl: opb_ref[slot, _OPS_B.index(name), rs, sl]
    opf = lambda name, rs, sl: opf_ref[slot, _OPS_F.index(name), rs, sl]

    grams = [
        _dot(jnp.concatenate([opb("at", rs, sl), opb("rt", rs, sl)], axis=0),
             jnp.concatenate([_stack_heads(opb("bt", rs, sl), first),
                              _stack_heads(opb("kt", rs, sl), first)], axis=0), _NT)
        for _, rs, _, sl in tiles]
    a_abs = [jnp.where(strict2, gm[:c, :2 * c], 0.0).astype(BF16) for gm in grams]
    t_invs = _unit_lower_inverses(a_abs, c, filler)
    a_rbs = [jnp.where(incl2, gm[c:, :2 * c], 0.0).astype(BF16) for gm in grams]
    akvs = [
        _dot(jnp.concatenate([jnp.where(strict2, gm[:c, 2 * c:], 0.0),
                              jnp.where(incl2, gm[c:, 2 * c:], 0.0)], axis=0).astype(BF16),
             _stack_heads(opb("vb", rs, sl), first))
        for gm, (_, rs, _, sl) in zip(grams, tiles)]

    ta0s = [_dot(t, _stack_heads(opb("a0", rs, sl), first)).astype(BF16)
            for t, (_, rs, _, sl) in zip(t_invs, tiles)]
    tavs = [_dot(t, _stack_heads(akv[:c].astype(BF16), first)).astype(BF16)
            for t, akv in zip(t_invs, akvs)]
    next(filler, None)
    m_offs = [jnp.where(blockdiag, _dot(ta0, opb("bh", rs, sl), _TN), 0.0).astype(BF16)
              for ta0, (_, rs, _, sl) in zip(ta0s, tiles)]
    consts = [jnp.where(blockdiag,
                        _dot(jnp.concatenate([tav, opb("vb", rs, sl)], axis=0),
                             jnp.concatenate([opb("bh", rs, sl), opb("kh", rs, sl)], axis=0), _TN),
                        0.0)
              for tav, (_, rs, _, sl) in zip(tavs, tiles)]
    next(filler, None)
    ras = [(opb("r0", rs, sl).astype(F32) + _dot(a_rb, _stack_heads(ta0, first))).astype(BF16)
           for a_rb, ta0, (_, rs, _, sl) in zip(a_rbs, ta0s, tiles)]
    y_consts = [akv[c:] + _dot(a_rb, _stack_heads(tav, first))
                for akv, a_rb, tav in zip(akvs, a_rbs, tavs)]
    next(filler, None)

    states = [state_ref[p] for p in pairs]
    ys = []
    for j in range(len(chunks)):
        mine = slice(j * N_PAIRS, (j + 1) * N_PAIRS)
        s_bs = [s.astype(BF16) for s in states]
        ys += [_dot(ra, s_b, _NT) + yc for ra, yc, s_b in zip(ras[mine], y_consts[mine], s_bs)]
        states = [s * dend_ref[slot, j * SUBLANES:j * SUBLANES + 1, sl] + _dot(s_b, m) + cm
                  for s, s_b, m, cm, sl in zip(states, s_bs, m_offs[mine], consts[mine], sls)]
    for p, s in enumerate(states):
        state_ref[p] = s

    means = [_dot(_hilo_cols(y), avg_2) for y in ys]
    ycs = [y - m for y, m in zip(ys, means)]
    stats = [_dot(jnp.concatenate([(yc * yc).astype(BF16), opb("rkk", rs, sl)], axis=1), stat_bd)
             for yc, (_, rs, _, sl) in zip(ycs, tiles)]
    for yc, st, (_, rs, _, sl) in zip(ycs, stats, tiles):
        o_ref[rs, sl] = ((yc * lax.rsqrt(st[:, :LANES] + GN_EPS) * gn_g[:, sl] + gn_b[:, sl]
                          + st[:, LANES:] * opf("v", rs, sl)) * opf("g", rs, sl)
                         ).astype(o_ref.dtype)


def _gmlp_stages(zg_ref, lng_ref, lnb_ref, wst_ref, bs_ref, o_ref):
    gelu = lambda t: 0.5 * t * (1.0 + lax.erf(t * 0.7071067811865476))
    n_tiles = GMLP_WIDTH // LANES
    cols = lambda base, p: slice(base + p * LANES, base + (p + 1) * LANES)
    rows = zg_ref.shape[0]
    chunks = [slice(j * CHUNK, (j + 1) * CHUNK) for j in range(rows // CHUNK)]
    vs = []
    for p in range(n_tiles):
        vs.append(gelu(zg_ref[:, cols(GMLP_WIDTH, p)]))
        yield
    m = sum(jnp.sum(t, axis=-1, keepdims=True) for t in vs) * (1.0 / GMLP_WIDTH)
    vcs = [t - m for t in vs]
    var = sum(jnp.sum(t * t, axis=-1, keepdims=True) for t in vcs) * (1.0 / GMLP_WIDTH)
    inv = lax.rsqrt(var + LN_EPS)
    yield
    first = _iota2((CHUNK, LANES), 1) < HEAD
    for p in range(n_tiles):
        vn = (vcs[p] * inv * lng_ref[:, cols(0, p)] + lnb_ref[:, cols(0, p)]).astype(BF16)
        u = gelu(zg_ref[:, cols(0, p)])
        for rs in chunks:
            mixed = _dot(wst_ref[p], _stack_heads(vn[rs], first))
            o_ref[rs, cols(RWKV_WIDTH, p)] = (
                u[rs] * (mixed + bs_ref[:, cols(0, p)])).astype(o_ref.dtype)
        yield


def _interleaved(*stage_generators):
    for _ in itertools.zip_longest(*stage_generators):
        yield


def _mixer_kernel(zr_ref, zg_ref, mu_ref, w0_ref, wdec_ref, a0_ref, waaa_ref, wgate_ref, kk_ref,
                  ka_ref, rk_ref, gng_ref, gnb_ref, lng_ref, lnb_ref, ws_ref, bs_ref,
                  o_ref, state_ref, prev_ref, wst_ref, opb_ref, opf_ref, dend_ref):
    s = pl.program_id(0)

    @pl.when(s == 0)
    def _():
        state_ref[...] = jnp.zeros_like(state_ref)
        prev_ref[...] = jnp.zeros_like(prev_ref)
        opb_ref[...] = jnp.zeros_like(opb_ref)
        opf_ref[...] = jnp.zeros_like(opf_ref)
        dend_ref[...] = jnp.zeros_like(dend_ref)
        causal = _iota2((CHUNK, CHUNK), 0) >= _iota2((CHUNK, CHUNK), 1)
        for h in range(ws_ref.shape[0]):
            wst_ref[h // 2, :, (h % 2) * CHUNK:(h % 2 + 1) * CHUNK] = (
                jnp.where(causal, ws_ref[h], 0.0).astype(BF16))

    prepare = _rwkv_prepare_stages(zr_ref, prev_ref, mu_ref, w0_ref, wdec_ref, a0_ref, waaa_ref,
                                   wgate_ref, kk_ref, ka_ref, rk_ref, opb_ref, opf_ref, dend_ref,
                                   s % 2)
    gmlp = _gmlp_stages(zg_ref, lng_ref, lnb_ref, wst_ref, bs_ref, o_ref)
    filler = _interleaved(prepare, gmlp)
    _rwkv_chunks(opb_ref, opf_ref, dend_ref, (s + 1) % 2, gng_ref[...], gnb_ref[...],
                 state_ref, o_ref, filler)
    for _ in filler:
        pass


def _mixer(zr, zg, mu, w0, wdec, a0, waaa, wgate, k_k, k_a, r_k, gn_g, gn_b, ln_g, ln_b, w_s, b_s):
    rows = MIX_CHUNKS * CHUNK
    n = zr.shape[0] // rows
    n_heads = w_s.shape[0]

    def full(arr):
        nd = arr.ndim
        return pl.BlockSpec(arr.shape, lambda s: (0,) * nd)

    params = (mu, w0, wdec, a0, waaa, wgate, k_k, k_a, r_k, gn_g, gn_b, ln_g, ln_b, w_s, b_s)
    return pl.pallas_call(
        _mixer_kernel,
        out_shape=jax.ShapeDtypeStruct((n * rows, D_MODEL), BF16),
        grid=(n + 1,),
        in_specs=[pl.BlockSpec((rows, RWKV_COLS), lambda s: (jnp.minimum(s, n - 1), 0)),
                  pl.BlockSpec((rows, 2 * GMLP_WIDTH), lambda s: (jnp.maximum(s - 1, 0), 0))]
                 + [full(a) for a in params],
        out_specs=pl.BlockSpec((rows, D_MODEL), lambda s: (jnp.maximum(s - 1, 0), 0)),
        scratch_shapes=[
            pltpu.VMEM((N_PAIRS, LANES, LANES), F32),
            pltpu.VMEM((1, RWKV_COLS), F32),
            pltpu.VMEM((n_heads // 2, CHUNK, 2 * CHUNK), BF16),
            pltpu.VMEM((2, len(_OPS_B), rows, RWKV_WIDTH), BF16),
            pltpu.VMEM((2, len(_OPS_F), rows, RWKV_WIDTH), F32),
            pltpu.VMEM((2, rows // RWKV_CHUNK * SUBLANES, RWKV_WIDTH), F32),
        ],
        compiler_params=pltpu.CompilerParams(
            dimension_semantics=("arbitrary",), vmem_limit_bytes=VMEM_LIMIT),
        name="mixer",
    )(zr, zg, *params)


def kernel(x, p, norm_ffn1, ffn1_w1, ffn1_w3, ffn1_w2, norm_mix, w_in, shift_mu, rwkv_w0,
           rwkv_w_decay, rwkv_a0, rwkv_w_aaa, rwkv_w_gate, rwkv_k_k, rwkv_k_a, rwkv_r_k,
           rwkv_gn_g, rwkv_gn_b, sgu_ln_g, sgu_ln_b, sgu_w_s, sgu_b_s, w_out, norm_ffn2,
           ffn2_w1, ffn2_w3, ffn2_w2, norm_ple, w_ple_gate, w_ple, norm_final):
    assert x.shape[0] == 1 and p.shape[0] == 1, "one batch row and one layer, as the problem states"
    row = lambda a: a.reshape(1, -1)
    bf = lambda a: a.astype(BF16)
    h, zr, zg = _pre_mixer(
        x[0], [row(norm_ffn1[0]), bf(ffn1_w1[0]), bf(ffn1_w3[0]), bf(ffn1_w2[0]),
               row(norm_mix[0]), bf(w_in[0])])
    lora = rwkv_w_decay.shape[1]
    pad = jnp.zeros((LANES - lora, RWKV_WIDTH), F32)
    wdec = jnp.concatenate([rwkv_w_decay[0], pad], axis=0)
    waaa = jnp.concatenate([pad, rwkv_w_aaa[0]], axis=0)
    bias = jnp.repeat(sgu_b_s[0].T, HEAD, axis=1)
    y = _mixer(zr, zg, row(shift_mu[0]), row(rwkv_w0[0]), bf(wdec), row(rwkv_a0[0]), bf(waaa),
               bf(rwkv_w_gate[0]), row(rwkv_k_k[0]), row(rwkv_k_a[0]), row(rwkv_r_k[0]),
               row(rwkv_gn_g[0]), row(rwkv_gn_b[0]), row(sgu_ln_g[0]), row(sgu_ln_b[0]),
               sgu_w_s[0], bias)
    out = _post_mixer(
        h, y, p[0, 0],
        [bf(w_out[0]), row(norm_ffn2[0]), bf(ffn2_w1[0]), bf(ffn2_w3[0]), bf(ffn2_w2[0]),
         row(norm_ple[0]), bf(w_ple_gate[0]), bf(w_ple[0]), row(norm_final)])
    return out[None]
```

```python
import itertools

import jax
import jax.numpy as jnp
from jax import lax
from jax.experimental import pallas as pl
from jax.experimental.pallas import tpu as pltpu

F32 = jnp.float32
BF16 = jnp.bfloat16

D_MODEL = 1024
D_FF = 2816
RWKV_WIDTH = 512
RWKV_COLS = 1792
GMLP_WIDTH = 512
HEAD = 64
LANES = 128
SUBLANES = 8
N_PAIRS = RWKV_WIDTH // LANES
CHUNK = 128
RWKV_CHUNK = 64
MIX_CHUNKS = 2
INV_BASE = 16
RMS_EPS = 1e-6
LN_EPS = 1e-5
GN_EPS = 64e-5
VMEM_LIMIT = 56 * 1024 * 1024

_NN = (((1,), (0,)), ((), ()))
_NT = (((1,), (1,)), ((), ()))
_TN = (((0,), (0,)), ((), ()))


def _dot(a, b, dn=_NN):
    return lax.dot_general(a, b, dn, preferred_element_type=F32)


def _split2(x):
    hi = x.astype(BF16)
    lo = (x - hi.astype(F32)).astype(BF16)
    return hi, lo


def _hilo_cols(x):
    return jnp.concatenate(_split2(x), axis=1)


def _hilo_rows(x):
    return jnp.concatenate(_split2(x), axis=0)


def _rms(x, g):
    ms = jnp.mean(x * x, axis=-1, keepdims=True)
    return x * lax.rsqrt(ms + RMS_EPS) * g


FF_CHUNK = 256


def _swiglu_hidden(xn, w1_ref, w3_ref, hm_ref, filler):
    for j in range(hm_ref.shape[1] // FF_CHUNK):
        cols = slice(j * FF_CHUNK, (j + 1) * FF_CHUNK)
        a = _dot(xn, w1_ref[:, cols])
        b = _dot(xn, w3_ref[:, cols])
        hm_ref[:, cols] = (a * jax.nn.sigmoid(a) * b).astype(BF16)
        next(filler, None)


def _rms_stages(x_ref, g_ref, out_ref, slot, rows_per_stage=64):
    for start in range(0, x_ref.shape[0], rows_per_stage):
        rs = slice(start, start + rows_per_stage)
        out_ref[slot, rs, :] = _rms(x_ref[rs, :], g_ref[...]).astype(BF16)
        yield


def _pre_mixer_kernel(x_ref, xnext_ref, g1_ref, w1_ref, w3_ref, w2_ref, gm_ref, win_ref,
                      h_ref, zr_ref, zg_ref, hm_ref, xn1_ref, xn2_ref):
    i = pl.program_id(0)
    last = pl.num_programs(0) - 1

    def project(slot):
        z = _dot(xn2_ref[slot], win_ref[...])
        zr_ref[...] = z[:, :RWKV_COLS]
        zg_ref[...] = z[:, RWKV_COLS:]

    @pl.when(i == 0)
    def _():
        xn1_ref[0] = _rms(x_ref[...], g1_ref[...]).astype(BF16)
        xn2_ref[1] = jnp.zeros(xn2_ref.shape[1:], BF16)

    @pl.when(i < last)
    def _():
        filler = _rms_stages(xnext_ref, g1_ref, xn1_ref, (i + 1) % 2)
        _swiglu_hidden(xn1_ref[i % 2], w1_ref, w3_ref, hm_ref, filler)
        for _ in filler:
            pass
        acc = _dot(hm_ref[...], w2_ref[...])
        project((i + 1) % 2)
        h = x_ref[...] + 0.5 * acc
        h_ref[...] = h
        xn2_ref[i % 2] = _rms(h, gm_ref[...]).astype(BF16)

    @pl.when(i == last)
    def _():
        project((i + 1) % 2)


def _post_mixer_kernel(h_ref, y_ref, p_ref, wout_ref, g2_ref, w1_ref, w3_ref, w2_ref,
                       gp_ref, wg_ref, wp_ref, gf_ref, o_ref, hm_ref):
    h = h_ref[...] + _dot(y_ref[...], wout_ref[...])
    _swiglu_hidden(_rms(h, g2_ref[...]).astype(BF16), w1_ref, w3_ref, hm_ref, iter(()))
    h = h + 0.5 * _dot(hm_ref[...], w2_ref[...])
    gate = jax.nn.sigmoid(_dot(_rms(h, gp_ref[...]).astype(BF16), wg_ref[...]))
    e = _dot(p_ref[...].astype(BF16), wp_ref[...])
    o_ref[...] = _rms(h + gate * e, gf_ref[...])


def _rows(tm, width, tile_of_step=lambda i: i):
    return pl.BlockSpec((tm, width), lambda i: (tile_of_step(i), 0))


def _resident(arr):
    nd = arr.ndim
    return pl.BlockSpec(arr.shape, lambda i: (0,) * nd, pipeline_mode=pl.Buffered(1))


def _pre_mixer(x, params, *, tm=512):
    t, d = x.shape
    n = t // tm
    this = lambda i: jnp.minimum(i, n - 1)
    nxt = lambda i: jnp.minimum(i + 1, n - 1)
    prev = lambda i: jnp.maximum(i - 1, 0)
    return pl.pallas_call(
        _pre_mixer_kernel,
        out_shape=(jax.ShapeDtypeStruct((t, d), F32),
                   jax.ShapeDtypeStruct((t, RWKV_COLS), F32),
                   jax.ShapeDtypeStruct((t, 2 * GMLP_WIDTH), F32)),
        grid=(n + 1,),
        in_specs=[_rows(tm, d, this), _rows(tm, d, nxt)] + [_resident(a) for a in params],
        out_specs=(_rows(tm, d, this), _rows(tm, RWKV_COLS, prev),
                   _rows(tm, 2 * GMLP_WIDTH, prev)),
        scratch_shapes=[pltpu.VMEM((tm, D_FF), BF16),
                        pltpu.VMEM((2, tm, d), BF16),
                        pltpu.VMEM((2, tm, d), BF16)],
        compiler_params=pltpu.CompilerParams(
            dimension_semantics=("arbitrary",), vmem_limit_bytes=VMEM_LIMIT),
        name="pre_mixer",
    )(x, x, *params)


def _post_mixer(h, y, p, params, *, tm=1024):
    t, d = h.shape
    return pl.pallas_call(
        _post_mixer_kernel,
        out_shape=jax.ShapeDtypeStruct((t, d), F32),
        grid=(t // tm,),
        in_specs=[_rows(tm, a.shape[1]) for a in (h, y, p)] + [_resident(a) for a in params],
        out_specs=_rows(tm, d),
        scratch_shapes=[pltpu.VMEM((tm, D_FF), BF16)],
        compiler_params=pltpu.CompilerParams(
            dimension_semantics=("arbitrary",), vmem_limit_bytes=VMEM_LIMIT),
        name="post_mixer",
    )(h, y, p, *params)


def _iota2(shape, dim):
    return lax.broadcasted_iota(jnp.int32, shape, dim)


def _stack_heads(x, first):
    zero = jnp.zeros_like(x)
    return jnp.concatenate([jnp.where(first, x, zero), jnp.where(first, zero, x)], axis=0)


def _nilpotent_inverses(xs, eye, index, bd, filler):
    w = eye.shape[1]
    pows = [x.astype(BF16) for x in xs]
    prods = [(eye + p).astype(BF16) for p in pows]
    pows = [_dot(p, bd(p)).astype(BF16) for p in pows]
    next(filler, None)
    covered = 4
    while covered < index:
        outs = [_dot(p, jnp.concatenate([bd(p), bd(t)], axis=1)) for p, t in zip(pows, prods)]
        pows = [o[:, :w].astype(BF16) for o in outs]
        prods = [(t + o[:, w:]).astype(BF16) for t, o in zip(prods, outs)]
        next(filler, None)
        covered *= 2
    prods = [(t + _dot(p, bd(t))).astype(BF16) for p, t in zip(pows, prods)]
    next(filler, None)
    return prods


def _unit_lower_inverses(mats, c, filler):
    row = _iota2((c, 2 * c), 0)
    lane = _iota2((c, 2 * c), 1)
    col = lane & (c - 1)
    left = lane < c
    shift = INV_BASE.bit_length() - 1
    diag_block = (row >> shift) == (col >> shift)
    eye = jnp.where(row == col, 1.0, 0.0)

    def bd(x):
        zero = jnp.zeros_like(x)
        return jnp.concatenate([jnp.where(left, x, zero), jnp.where(left, zero, x)], axis=0)

    t_diag = _nilpotent_inverses([jnp.where(diag_block, a, 0.0) for a in mats], eye, INV_BASE,
                                 bd, filler)
    n_mats = [_dot(t, bd(jnp.where(diag_block, 0.0, a))) for t, a in zip(t_diag, mats)]
    next(filler, None)
    t_off = _nilpotent_inverses(n_mats, eye, c // INV_BASE, bd, filler)
    return [_dot(t, bd(td)).astype(BF16) for t, td in zip(t_off, t_diag)]


_OPS_B = ("rt", "at", "bt", "kt", "r0", "a0", "bh", "kh", "vb", "rkk")
_OPS_F = ("v", "g")


def _rwkv_prepare_stages(zr_ref, prev_ref, mu_ref, w0_ref, wdec_ref, a0_ref, waaa_ref,
                         wgate_ref, kk_ref, ka_ref, rk_ref, opb_ref, opf_ref, dend_ref, slot):
    c = RWKV_CHUNK
    rows = zr_ref.shape[0]
    chunks = [slice(j * c, (j + 1) * c) for j in range(rows // c)]
    first_row = _iota2((rows, LANES), 0) == 0

    def shifted(col):
        cols = slice(col, col + LANES)
        z = zr_ref[:, cols]
        zp = jnp.where(first_row, prev_ref[:, cols], pltpu.roll(z, 1, 0))
        return z + (zp - z) * mu_ref[:, cols]

    w = RWKV_WIDTH
    zwa = shifted(3 * w)
    zwa_tanh = jnp.tanh(zwa).astype(BF16)
    zwa_b = zwa.astype(BF16)
    zg_sig = jax.nn.sigmoid(shifted(3 * w + LANES)).astype(BF16)
    blockdiag = (_iota2((LANES, LANES), 0) >= HEAD) == (_iota2((LANES, LANES), 1) >= HEAD)
    ones_bd = blockdiag.astype(BF16)
    row2 = _iota2((c, 2 * c), 0)
    col2 = _iota2((c, 2 * c), 1) & (c - 1)
    tril_2 = (row2 >= col2).astype(BF16)
    yield
    for p in range(N_PAIRS):
        sl = slice(p * LANES, (p + 1) * LANES)
        r, k, v = shifted(p * LANES), shifted(w + p * LANES), shifted(2 * w + p * LANES)
        lw = w0_ref[:, sl] + _dot(zwa_tanh, wdec_ref[:, sl])
        logd = -0.6065306597126334 * jax.nn.sigmoid(lw)
        a = jax.nn.sigmoid(a0_ref[:, sl] + _dot(zwa_b, waaa_ref[:, sl]))
        g = _dot(zg_sig, wgate_ref[:, sl])
        kk = k * kk_ref[:, sl]
        n2 = _dot((kk * kk).astype(BF16), ones_bd)
        kkn = kk * lax.rsqrt(jnp.maximum(n2, 1e-24))
        k2 = k * (1.0 + (a - 1.0) * ka_ref[:, sl])
        bvec = kkn * a
        opb_ref[slot, _OPS_B.index("rkk"), :, sl] = (r * k2 * rk_ref[:, sl]).astype(BF16)
        opf_ref[slot, _OPS_F.index("v"), :, sl] = v
        opf_ref[slot, _OPS_F.index("g"), :, sl] = g
        yield
        for j, rs in enumerate(chunks):
            lcum = _dot(tril_2, _hilo_rows(logd[rs]))
            lmid = lcum[c // 2 - 1:c // 2, :]
            lend = lcum[c - 1:c, :]
            lexc = lcum - logd[rs]
            e_mid_from = jnp.exp(lmid - lcum)
            e_end = jnp.exp(lend - lcum)
            ops = dict(
                rt=r[rs] * jnp.exp(lcum - lmid), at=-kkn[rs] * jnp.exp(lexc - lmid),
                bt=bvec[rs] * e_mid_from, kt=k2[rs] * e_mid_from,
                r0=r[rs] * jnp.exp(lcum), a0=-kkn[rs] * jnp.exp(lexc),
                bh=bvec[rs] * e_end, kh=k2[rs] * e_end, vb=v[rs])
            for name, value in ops.items():
                opb_ref[slot, _OPS_B.index(name), rs, sl] = value.astype(BF16)
            dend_ref[slot, j * SUBLANES:(j + 1) * SUBLANES, sl] = jnp.broadcast_to(
                jnp.exp(lend), (SUBLANES, LANES))
        yield
    prev_ref[...] = zr_ref[rows - 1:rows, :]


def _rwkv_chunks(opb_ref, opf_ref, dend_ref, slot, gn_g, gn_b, state_ref, o_ref, filler):
    c = RWKV_CHUNK
    row2 = _iota2((c, 2 * c), 0)
    col2 = _iota2((c, 2 * c), 1) & (c - 1)
    incl2 = row2 >= col2
    strict2 = row2 > col2
    first = _iota2((c, LANES), 1) < HEAD
    blockdiag = (_iota2((LANES, LANES), 0) >= HEAD) == (_iota2((LANES, LANES), 1) >= HEAD)
    ones_bd = blockdiag.astype(BF16)
    avg_bd = (blockdiag.astype(F32) * (1.0 / HEAD)).astype(BF16)
    zero_bd = jnp.zeros((LANES, LANES), BF16)
    avg_2 = jnp.concatenate([avg_bd, avg_bd], axis=0)
    stat_bd = jnp.concatenate([jnp.concatenate([avg_bd, zero_bd], axis=1),
                               jnp.concatenate([zero_bd, ones_bd], axis=1)], axis=0)
    pairs = range(N_PAIRS)
    sls = [slice(p * LANES, (p + 1) * LANES) for p in pairs]
    chunks = [slice(j * c, (j + 1) * c) for j in range(opb_ref.shape[2] // c)]
    tiles = [(j, rs, p, sl) for j, rs in enumerate(chunks) for p, sl in enumerate(sls)]
    opb = lambda name, rs, sl: opb_ref[slot, _OPS_B.index(name), rs, sl]
    opf = lambda name, rs, sl: opf_ref[slot, _OPS_F.index(name), rs, sl]

    grams = [
        _dot(jnp.concatenate([opb("at", rs, sl), opb("rt", rs, sl)], axis=0),
             jnp.concatenate([_stack_heads(opb("bt", rs, sl), first),
                              _stack_heads(opb("kt", rs, sl), first)], axis=0), _NT)
        for _, rs, _, sl in tiles]
    a_abs = [jnp.where(strict2, gm[:c, :2 * c], 0.0).astype(BF16) for gm in grams]
    t_invs = _unit_lower_inverses(a_abs, c, filler)
    a_rbs = [jnp.where(incl2, gm[c:, :2 * c], 0.0).astype(BF16) for gm in grams]
    akvs = [
        _dot(jnp.concatenate([jnp.where(strict2, gm[:c, 2 * c:], 0.0),
                              jnp.where(incl2, gm[c:, 2 * c:], 0.0)], axis=0).astype(BF16),
             _stack_heads(opb("vb", rs, sl), first))
        for gm, (_, rs, _, sl) in zip(grams, tiles)]

    ta0s = [_dot(t, _stack_heads(opb("a0", rs, sl), first)).astype(BF16)
            for t, (_, rs, _, sl) in zip(t_invs, tiles)]
    tavs = [_dot(t, _stack_heads(akv[:c].astype(BF16), first)).astype(BF16)
            for t, akv in zip(t_invs, akvs)]
    next(filler, None)
    m_offs = [jnp.where(blockdiag, _dot(ta0, opb("bh", rs, sl), _TN), 0.0).astype(BF16)
              for ta0, (_, rs, _, sl) in zip(ta0s, tiles)]
    consts = [jnp.where(blockdiag,
                        _dot(jnp.concatenate([tav, opb("vb", rs, sl)], axis=0),
                             jnp.concatenate([opb("bh", rs, sl), opb("kh", rs, sl)], axis=0), _TN),
                        0.0)
              for tav, (_, rs, _, sl) in zip(tavs, tiles)]
    next(filler, None)
    ras = [(opb("r0", rs, sl).astype(F32) + _dot(a_rb, _stack_heads(ta0, first))).astype(BF16)
           for a_rb, ta0, (_, rs, _, sl) in zip(a_rbs, ta0s, tiles)]
    y_consts = [akv[c:] + _dot(a_rb, _stack_heads(tav, first))
                for akv, a_rb, tav in zip(akvs, a_rbs, tavs)]
    next(filler, None)

    states = [state_ref[p] for p in pairs]
    ys = []
    for j in range(len(chunks)):
        mine = slice(j * N_PAIRS, (j + 1) * N_PAIRS)
        s_bs = [s.astype(BF16) for s in states]
        ys += [_dot(ra, s_b, _NT) + yc for ra, yc, s_b in zip(ras[mine], y_consts[mine], s_bs)]
        states = [s * dend_ref[slot, j * SUBLANES:j * SUBLANES + 1, sl] + _dot(s_b, m) + cm
                  for s, s_b, m, cm, sl in zip(states, s_bs, m_offs[mine], consts[mine], sls)]
    for p, s in enumerate(states):
        state_ref[p] = s

    means = [_dot(_hilo_cols(y), avg_2) for y in ys]
    ycs = [y - m for y, m in zip(ys, means)]
    stats = [_dot(jnp.concatenate([(yc * yc).astype(BF16), opb("rkk", rs, sl)], axis=1), stat_bd)
             for yc, (_, rs, _, sl) in zip(ycs, tiles)]
    for yc, st, (_, rs, _, sl) in zip(ycs, stats, tiles):
        o_ref[rs, sl] = ((yc * lax.rsqrt(st[:, :LANES] + GN_EPS) * gn_g[:, sl] + gn_b[:, sl]
                          + st[:, LANES:] * opf("v", rs, sl)) * opf("g", rs, sl)
                         ).astype(o_ref.dtype)


def _gmlp_stages(zg_ref, lng_ref, lnb_ref, wst_ref, bs_ref, o_ref):
    gelu = lambda t: 0.5 * t * (1.0 + lax.erf(t * 0.7071067811865476))
    n_tiles = GMLP_WIDTH // LANES
    cols = lambda base, p: slice(base + p * LANES, base + (p + 1) * LANES)
    rows = zg_ref.shape[0]
    chunks = [slice(j * CHUNK, (j + 1) * CHUNK) for j in range(rows // CHUNK)]
    vs = []
    for p in range(n_tiles):
        vs.append(gelu(zg_ref[:, cols(GMLP_WIDTH, p)]))
        yield
    m = sum(jnp.sum(t, axis=-1, keepdims=True) for t in vs) * (1.0 / GMLP_WIDTH)
    vcs = [t - m for t in vs]
    var = sum(jnp.sum(t * t, axis=-1, keepdims=True) for t in vcs) * (1.0 / GMLP_WIDTH)
    inv = lax.rsqrt(var + LN_EPS)
    yield
    first = _iota2((CHUNK, LANES), 1) < HEAD
    for p in range(n_tiles):
        vn = (vcs[p] * inv * lng_ref[:, cols(0, p)] + lnb_ref[:, cols(0, p)]).astype(BF16)
        u = gelu(zg_ref[:, cols(0, p)])
        for rs in chunks:
            mixed = _dot(wst_ref[p], _stack_heads(vn[rs], first))
            o_ref[rs, cols(RWKV_WIDTH, p)] = (
                u[rs] * (mixed + bs_ref[:, cols(0, p)])).astype(o_ref.dtype)
        yield


def _interleaved(*stage_generators):
    for _ in itertools.zip_longest(*stage_generators):
        yield


def _mixer_kernel(zr_ref, zg_ref, mu_ref, w0_ref, wdec_ref, a0_ref, waaa_ref, wgate_ref, kk_ref,
                  ka_ref, rk_ref, gng_ref, gnb_ref, lng_ref, lnb_ref, ws_ref, bs_ref,
                  o_ref, state_ref, prev_ref, wst_ref, opb_ref, opf_ref, dend_ref):
    s = pl.program_id(0)

    @pl.when(s == 0)
    def _():
        state_ref[...] = jnp.zeros_like(state_ref)
        prev_ref[...] = jnp.zeros_like(prev_ref)
        opb_ref[...] = jnp.zeros_like(opb_ref)
        opf_ref[...] = jnp.zeros_like(opf_ref)
        dend_ref[...] = jnp.zeros_like(dend_ref)
        causal = _iota2((CHUNK, CHUNK), 0) >= _iota2((CHUNK, CHUNK), 1)
        for h in range(ws_ref.shape[0]):
            wst_ref[h // 2, :, (h % 2) * CHUNK:(h % 2 + 1) * CHUNK] = (
                jnp.where(causal, ws_ref[h], 0.0).astype(BF16))

    prepare = _rwkv_prepare_stages(zr_ref, prev_ref, mu_ref, w0_ref, wdec_ref, a0_ref, waaa_ref,
                                   wgate_ref, kk_ref, ka_ref, rk_ref, opb_ref, opf_ref, dend_ref,
                                   s % 2)
    gmlp = _gmlp_stages(zg_ref, lng_ref, lnb_ref, wst_ref, bs_ref, o_ref)
    filler = _interleaved(prepare, gmlp)
    _rwkv_chunks(opb_ref, opf_ref, dend_ref, (s + 1) % 2, gng_ref[...], gnb_ref[...],
                 state_ref, o_ref, filler)
    for _ in filler:
        pass


def _mixer(zr, zg, mu, w0, wdec, a0, waaa, wgate, k_k, k_a, r_k, gn_g, gn_b, ln_g, ln_b, w_s, b_s):
    rows = MIX_CHUNKS * CHUNK
    n = zr.shape[0] // rows
    n_heads = w_s.shape[0]

    def full(arr):
        nd = arr.ndim
        return pl.BlockSpec(arr.shape, lambda s: (0,) * nd)

    params = (mu, w0, wdec, a0, waaa, wgate, k_k, k_a, r_k, gn_g, gn_b, ln_g, ln_b, w_s, b_s)
    return pl.pallas_call(
        _mixer_kernel,
        out_shape=jax.ShapeDtypeStruct((n * rows, D_MODEL), BF16),
        grid=(n + 1,),
        in_specs=[pl.BlockSpec((rows, RWKV_COLS), lambda s: (jnp.minimum(s, n - 1), 0)),
                  pl.BlockSpec((rows, 2 * GMLP_WIDTH), lambda s: (jnp.maximum(s - 1, 0), 0))]
                 + [full(a) for a in params],
        out_specs=pl.BlockSpec((rows, D_MODEL), lambda s: (jnp.maximum(s - 1, 0), 0)),
        scratch_shapes=[
            pltpu.VMEM((N_PAIRS, LANES, LANES), F32),
            pltpu.VMEM((1, RWKV_COLS), F32),
            pltpu.VMEM((n_heads // 2, CHUNK, 2 * CHUNK), BF16),
            pltpu.VMEM((2, len(_OPS_B), rows, RWKV_WIDTH), BF16),
            pltpu.VMEM((2, len(_OPS_F), rows, RWKV_WIDTH), F32),
            pltpu.VMEM((2, rows // RWKV_CHUNK * SUBLANES, RWKV_WIDTH), F32),
        ],
        compiler_params=pltpu.CompilerParams(
            dimension_semantics=("arbitrary",), vmem_limit_bytes=VMEM_LIMIT),
        name="mixer",
    )(zr, zg, *params)


def kernel(x, p, norm_ffn1, ffn1_w1, ffn1_w3, ffn1_w2, norm_mix, w_in, shift_mu, rwkv_w0,
           rwkv_w_decay, rwkv_a0, rwkv_w_aaa, rwkv_w_gate, rwkv_k_k, rwkv_k_a, rwkv_r_k,
           rwkv_gn_g, rwkv_gn_b, sgu_ln_g, sgu_ln_b, sgu_w_s, sgu_b_s, w_out, norm_ffn2,
           ffn2_w1, ffn2_w3, ffn2_w2, norm_ple, w_ple_gate, w_ple, norm_final):
    assert x.shape[0] == 1 and p.shape[0] == 1, "one batch row and one layer, as the problem states"
    row = lambda a: a.reshape(1, -1)
    bf = lambda a: a.astype(BF16)
    h, zr, zg = _pre_mixer(
        x[0], [row(norm_ffn1[0]), bf(ffn1_w1[0]), bf(ffn1_w3[0]), bf(ffn1_w2[0]),
               row(norm_mix[0]), bf(w_in[0])])
    lora = rwkv_w_decay.shape[1]
    pad = jnp.zeros((LANES - lora, RWKV_WIDTH), F32)
    wdec = jnp.concatenate([rwkv_w_decay[0], pad], axis=0)
    waaa = jnp.concatenate([pad, rwkv_w_aaa[0]], axis=0)
    bias = jnp.repeat(sgu_b_s[0].T, HEAD, axis=1)
    y = _mixer(zr, zg, row(shift_mu[0]), row(rwkv_w0[0]), bf(wdec), row(rwkv_a0[0]), bf(waaa),
               bf(rwkv_w_gate[0]), row(rwkv_k_k[0]), row(rwkv_k_a[0]), row(rwkv_r_k[0]),
               row(rwkv_gn_g[0]), row(rwkv_gn_b[0]), row(sgu_ln_g[0]), row(sgu_ln_b[0]),
               sgu_w_s[0], bias)
    out = _post_mixer(
        h, y, p[0, 0],
        [bf(w_out[0]), row(norm_ffn2[0]), bf(ffn2_w1[0]), bf(ffn2_w3[0]), bf(ffn2_w2[0]),
         row(norm_ple[0]), bf(w_ple_gate[0]), bf(w_ple[0]), row(norm_final)])
    return out[None]
```

```python
import itertools

import jax
import jax.numpy as jnp
from jax import lax
from jax.experimental import pallas as pl
from jax.experimental.pallas import tpu as pltpu

F32 = jnp.float32
BF16 = jnp.bfloat16

D_MODEL = 1024
D_FF = 2816
RWKV_WIDTH = 512
RWKV_COLS = 1792
GMLP_WIDTH = 512
HEAD = 64
LANES = 128
SUBLANES = 8
N_PAIRS = RWKV_WIDTH // LANES
CHUNK = 128
RWKV_CHUNK = 64
MIX_CHUNKS = 2
INV_BASE = 16
RMS_EPS = 1e-6
LN_EPS = 1e-5
GN_EPS = 64e-5
VMEM_LIMIT = 56 * 1024 * 1024

_NN = (((1,), (0,)), ((), ()))
_NT = (((1,), (1,)), ((), ()))
_TN = (((0,), (0,)), ((), ()))


def _dot(a, b, dn=_NN):
    return lax.dot_general(a, b, dn, preferred_element_type=F32)


def _split2(x):
    hi = x.astype(BF16)
    lo = (x - hi.astype(F32)).astype(BF16)
    return hi, lo


def _hilo_cols(x):
    return jnp.concatenate(_split2(x), axis=1)


def _hilo_rows(x):
    return jnp.concatenate(_split2(x), axis=0)


def _rms(x, g):
    ms = jnp.mean(x * x, axis=-1, keepdims=True)
    return x * lax.rsqrt(ms + RMS_EPS) * g


FF_CHUNK = 256


def _swiglu_hidden(xn, w1_ref, w3_ref, hm_ref, filler):
    for j in range(hm_ref.shape[1] // FF_CHUNK):
        cols = slice(j * FF_CHUNK, (j + 1) * FF_CHUNK)
        a = _dot(xn, w1_ref[:, cols])
        b = _dot(xn, w3_ref[:, cols])
        hm_ref[:, cols] = (a * jax.nn.sigmoid(a) * b).astype(BF16)
        next(filler, None)


def _rms_stages(x_ref, g_ref, out_ref, slot, rows_per_stage=64):
    for start in range(0, x_ref.shape[0], rows_per_stage):
        rs = slice(start, start + rows_per_stage)
        out_ref[slot, rs, :] = _rms(x_ref[rs, :], g_ref[...]).astype(BF16)
        yield


def _pre_mixer_kernel(x_ref, xnext_ref, g1_ref, w1_ref, w3_ref, w2_ref, gm_ref, win_ref, mu_ref,
                      h_ref, zr_ref, zg_ref, hm_ref, xn1_ref, xn2_ref, carry_ref):
    i = pl.program_id(0)
    last = pl.num_programs(0) - 1

    def project(slot):
        z = _dot(xn2_ref[slot], win_ref[...])
        rows = z.shape[0]
        first_row = _iota2((rows, LANES), 0) == 0
        for col in range(0, RWKV_COLS, LANES):
            cols = slice(col, col + LANES)
            zc = z[:, cols]
            zp = jnp.where(first_row, carry_ref[:, cols], pltpu.roll(zc, 1, 0))
            carry_ref[:, cols] = zc[rows - 1:rows, :]
            zr_ref[:, cols] = zc + (zp - zc) * mu_ref[:, cols]
        zg = z[:, RWKV_COLS:]
        zg_ref[...] = 0.5 * zg * (1.0 + lax.erf(zg * 0.7071067811865476))

    @pl.when(i == 0)
    def _():
        xn1_ref[0] = _rms(x_ref[...], g1_ref[...]).astype(BF16)
        xn2_ref[1] = jnp.zeros(xn2_ref.shape[1:], BF16)
        carry_ref[...] = jnp.zeros_like(carry_ref)

    @pl.when(i < last)
    def _():
        filler = _rms_stages(xnext_ref, g1_ref, xn1_ref, (i + 1) % 2)
        _swiglu_hidden(xn1_ref[i % 2], w1_ref, w3_ref, hm_ref, filler)
        for _ in filler:
            pass
        acc = _dot(hm_ref[...], w2_ref[...])
        project((i + 1) % 2)
        h = x_ref[...] + 0.5 * acc
        h_ref[...] = h
        xn2_ref[i % 2] = _rms(h, gm_ref[...]).astype(BF16)

    @pl.when(i == last)
    def _():
        project((i + 1) % 2)


def _post_mixer_kernel(h_ref, y_ref, p_ref, wout_ref, g2_ref, w1_ref, w3_ref, w2_ref,
                       gp_ref, wg_ref, wp_ref, gf_ref, o_ref, hm_ref):
    h = h_ref[...] + _dot(y_ref[...], wout_ref[...])
    _swiglu_hidden(_rms(h, g2_ref[...]).astype(BF16), w1_ref, w3_ref, hm_ref, iter(()))
    h = h + 0.5 * _dot(hm_ref[...], w2_ref[...])
    gate = jax.nn.sigmoid(_dot(_rms(h, gp_ref[...]).astype(BF16), wg_ref[...]))
    e = _dot(p_ref[...].astype(BF16), wp_ref[...])
    o_ref[...] = _rms(h + gate * e, gf_ref[...])


def _rows(tm, width, tile_of_step=lambda i: i):
    return pl.BlockSpec((tm, width), lambda i: (tile_of_step(i), 0))


def _resident(arr):
    nd = arr.ndim
    return pl.BlockSpec(arr.shape, lambda i: (0,) * nd, pipeline_mode=pl.Buffered(1))


def _pre_mixer(x, params, *, tm=512):
    t, d = x.shape
    n = t // tm
    this = lambda i: jnp.minimum(i, n - 1)
    nxt = lambda i: jnp.minimum(i + 1, n - 1)
    prev = lambda i: jnp.maximum(i - 1, 0)
    return pl.pallas_call(
        _pre_mixer_kernel,
        out_shape=(jax.ShapeDtypeStruct((t, d), F32),
                   jax.ShapeDtypeStruct((t, RWKV_COLS), F32),
                   jax.ShapeDtypeStruct((t, 2 * GMLP_WIDTH), F32)),
        grid=(n + 1,),
        in_specs=[_rows(tm, d, this), _rows(tm, d, nxt)] + [_resident(a) for a in params],
        out_specs=(_rows(tm, d, this), _rows(tm, RWKV_COLS, prev),
                   _rows(tm, 2 * GMLP_WIDTH, prev)),
        scratch_shapes=[pltpu.VMEM((tm, D_FF), BF16),
                        pltpu.VMEM((2, tm, d), BF16),
                        pltpu.VMEM((2, tm, d), BF16),
                        pltpu.VMEM((1, RWKV_COLS), F32)],
        compiler_params=pltpu.CompilerParams(
            dimension_semantics=("arbitrary",), vmem_limit_bytes=VMEM_LIMIT),
        name="pre_mixer",
    )(x, x, *params)


def _post_mixer(h, y, p, params, *, tm=1024):
    t, d = h.shape
    return pl.pallas_call(
        _post_mixer_kernel,
        out_shape=jax.ShapeDtypeStruct((t, d), F32),
        grid=(t // tm,),
        in_specs=[_rows(tm, a.shape[1]) for a in (h, y, p)] + [_resident(a) for a in params],
        out_specs=_rows(tm, d),
        scratch_shapes=[pltpu.VMEM((tm, D_FF), BF16)],
        compiler_params=pltpu.CompilerParams(
            dimension_semantics=("arbitrary",), vmem_limit_bytes=VMEM_LIMIT),
        name="post_mixer",
    )(h, y, p, *params)


def _iota2(shape, dim):
    return lax.broadcasted_iota(jnp.int32, shape, dim)


def _stack_heads(x, first):
    zero = jnp.zeros_like(x)
    return jnp.concatenate([jnp.where(first, x, zero), jnp.where(first, zero, x)], axis=0)


def _nilpotent_inverses(xs, eye, index, bd, filler):
    w = eye.shape[1]
    pows = [x.astype(BF16) for x in xs]
    prods = [(eye + p).astype(BF16) for p in pows]
    pows = [_dot(p, bd(p)).astype(BF16) for p in pows]
    next(filler, None)
    covered = 4
    while covered < index:
        outs = [_dot(p, jnp.concatenate([bd(p), bd(t)], axis=1)) for p, t in zip(pows, prods)]
        pows = [o[:, :w].astype(BF16) for o in outs]
        prods = [(t + o[:, w:]).astype(BF16) for t, o in zip(prods, outs)]
        next(filler, None)
        covered *= 2
    prods = [(t + _dot(p, bd(t))).astype(BF16) for p, t in zip(pows, prods)]
    next(filler, None)
    return prods


def _unit_lower_inverses(mats, c, filler):
    row = _iota2((c, 2 * c), 0)
    lane = _iota2((c, 2 * c), 1)
    col = lane & (c - 1)
    left = lane < c
    shift = INV_BASE.bit_length() - 1
    diag_block = (row >> shift) == (col >> shift)
    eye = jnp.where(row == col, 1.0, 0.0)

    def bd(x):
        zero = jnp.zeros_like(x)
        return jnp.concatenate([jnp.where(left, x, zero), jnp.where(left, zero, x)], axis=0)

    t_diag = _nilpotent_inverses([jnp.where(diag_block, a, 0.0) for a in mats], eye, INV_BASE,
                                 bd, filler)
    n_mats = [_dot(t, bd(jnp.where(diag_block, 0.0, a))) for t, a in zip(t_diag, mats)]
    next(filler, None)
    t_off = _nilpotent_inverses(n_mats, eye, c // INV_BASE, bd, filler)
    return [_dot(t, bd(td)).astype(BF16) for t, td in zip(t_off, t_diag)]


_OPS_B = ("rt", "at", "bt", "kt", "r0", "a0", "bh", "kh", "vb", "rkk")
_OPS_F = ("v", "g")


def _rwkv_prepare_stages(zr_ref, w0_ref, wdec_ref, a0_ref, waaa_ref,
                         wgate_ref, kk_ref, ka_ref, rk_ref, opb_ref, opf_ref, dend_ref, slot):
    c = RWKV_CHUNK
    rows = zr_ref.shape[0]
    chunks = [slice(j * c, (j + 1) * c) for j in range(rows // c)]
    shifted = lambda col: zr_ref[:, col:col + LANES]

    w = RWKV_WIDTH
    zwa = shifted(3 * w)
    zwa_tanh = jnp.tanh(zwa).astype(BF16)
    zwa_b = zwa.astype(BF16)
    zg_sig = jax.nn.sigmoid(shifted(3 * w + LANES)).astype(BF16)
    blockdiag = (_iota2((LANES, LANES), 0) >= HEAD) == (_iota2((LANES, LANES), 1) >= HEAD)
    ones_bd = blockdiag.astype(BF16)
    row2 = _iota2((c, 2 * c), 0)
    col2 = _iota2((c, 2 * c), 1) & (c - 1)
    tril_2 = (row2 >= col2).astype(BF16)
    yield
    for p in range(N_PAIRS):
        sl = slice(p * LANES, (p + 1) * LANES)
        r, k, v = shifted(p * LANES), shifted(w + p * LANES), shifted(2 * w + p * LANES)
        lw = w0_ref[:, sl] + _dot(zwa_tanh, wdec_ref[:, sl])
        logd = -0.6065306597126334 * jax.nn.sigmoid(lw)
        a = jax.nn.sigmoid(a0_ref[:, sl] + _dot(zwa_b, waaa_ref[:, sl]))
        g = _dot(zg_sig, wgate_ref[:, sl])
        kk = k * kk_ref[:, sl]
        n2 = _dot((kk * kk).astype(BF16), ones_bd)
        kkn = kk * lax.rsqrt(jnp.maximum(n2, 1e-24))
        k2 = k * (1.0 + (a - 1.0) * ka_ref[:, sl])
        bvec = kkn * a
        opb_ref[slot, _OPS_B.index("rkk"), :, sl] = (r * k2 * rk_ref[:, sl]).astype(BF16)
        opf_ref[slot, _OPS_F.index("v"), :, sl] = v
        opf_ref[slot, _OPS_F.index("g"), :, sl] = g
        yield
        for j, rs in enumerate(chunks):
            lcum = _dot(tril_2, _hilo_rows(logd[rs]))
            lmid = lcum[c // 2 - 1:c // 2, :]
            lend = lcum[c - 1:c, :]
            lexc = lcum - logd[rs]
            e_mid_from = jnp.exp(lmid - lcum)
            e_end = jnp.exp(lend - lcum)
            ops = dict(
                rt=r[rs] * jnp.exp(lcum - lmid), at=-kkn[rs] * jnp.exp(lexc - lmid),
                bt=bvec[rs] * e_mid_from, kt=k2[rs] * e_mid_from,
                r0=r[rs] * jnp.exp(lcum), a0=-kkn[rs] * jnp.exp(lexc),
                bh=bvec[rs] * e_end, kh=k2[rs] * e_end, vb=v[rs])
            for name, value in ops.items():
                opb_ref[slot, _OPS_B.index(name), rs, sl] = value.astype(BF16)
            dend_ref[slot, j * SUBLANES:(j + 1) * SUBLANES, sl] = jnp.broadcast_to(
                jnp.exp(lend), (SUBLANES, LANES))
        yield


def _rwkv_chunks(opb_ref, opf_ref, dend_ref, slot, gn_g, gn_b, state_ref, o_ref, filler):
    c = RWKV_CHUNK
    row2 = _iota2((c, 2 * c), 0)
    col2 = _iota2((c, 2 * c), 1) & (c - 1)
    incl2 = row2 >= col2
    strict2 = row2 > col2
    first = _iota2((c, LANES), 1) < HEAD
    blockdiag = (_iota2((LANES, LANES), 0) >= HEAD) == (_iota2((LANES, LANES), 1) >= HEAD)
    ones_bd = blockdiag.astype(BF16)
    avg_bd = (blockdiag.astype(F32) * (1.0 / HEAD)).astype(BF16)
    zero_bd = jnp.zeros((LANES, LANES), BF16)
    avg_2 = jnp.concatenate([avg_bd, avg_bd], axis=0)
    stat_bd = jnp.concatenate([jnp.concatenate([avg_bd, zero_bd], axis=1),
                               jnp.concatenate([zero_bd, ones_bd], axis=1)], axis=0)
    pairs = range(N_PAIRS)
    sls = [slice(p * LANES, (p + 1) * LANES) for p in pairs]
    chunks = [slice(j * c, (j + 1) * c) for j in range(opb_ref.shape[2] // c)]
    tiles = [(j, rs, p, sl) for j, rs in enumerate(chunks) for p, sl in enumerate(sls)]
    opb = lambda name, rs, sl: opb_ref[slot, _OPS_B.index(name), rs, sl]
    opf = lambda name, rs, sl: opf_ref[slot, _OPS_F.index(name), rs, sl]

    grams = [
        _dot(jnp.concatenate([opb("at", rs, sl), opb("rt", rs, sl)], axis=0),
             jnp.concatenate([_stack_heads(opb("bt", rs, sl), first),
                              _stack_heads(opb("kt", rs, sl), first)], axis=0), _NT)
        for _, rs, _, sl in tiles]
    a_abs = [jnp.where(strict2, gm[:c, :2 * c], 0.0).astype(BF16) for gm in grams]
    t_invs = _unit_lower_inverses(a_abs, c, filler)
    a_rbs = [jnp.where(incl2, gm[c:, :2 * c], 0.0).astype(BF16) for gm in grams]
    akvs = [
        _dot(jnp.concatenate([jnp.where(strict2, gm[:c, 2 * c:], 0.0),
                              jnp.where(incl2, gm[c:, 2 * c:], 0.0)], axis=0).astype(BF16),
             _stack_heads(opb("vb", rs, sl), first))
        for gm, (_, rs, _, sl) in zip(grams, tiles)]

    ta0s = [_dot(t, _stack_heads(opb("a0", rs, sl), first)).astype(BF16)
            for t, (_, rs, _, sl) in zip(t_invs, tiles)]
    tavs = [_dot(t, _stack_heads(akv[:c].astype(BF16), first)).astype(BF16)
            for t, akv in zip(t_invs, akvs)]
    next(filler, None)
    m_offs = [jnp.where(blockdiag, _dot(ta0, opb("bh", rs, sl), _TN), 0.0).astype(BF16)
              for ta0, (_, rs, _, sl) in zip(ta0s, tiles)]
    consts = [jnp.where(blockdiag,
                        _dot(jnp.concatenate([tav, opb("vb", rs, sl)], axis=0),
                             jnp.concatenate([opb("bh", rs, sl), opb("kh", rs, sl)], axis=0), _TN),
                        0.0)
              for tav, (_, rs, _, sl) in zip(tavs, tiles)]
    next(filler, None)
    ras = [(opb("r0", rs, sl).astype(F32) + _dot(a_rb, _stack_heads(ta0, first))).astype(BF16)
           for a_rb, ta0, (_, rs, _, sl) in zip(a_rbs, ta0s, tiles)]
    y_consts = [akv[c:] + _dot(a_rb, _stack_heads(tav, first))
                for akv, a_rb, tav in zip(akvs, a_rbs, tavs)]
    next(filler, None)

    states = [state_ref[p] for p in pairs]
    ys = []
    for j in range(len(chunks)):
        mine = slice(j * N_PAIRS, (j + 1) * N_PAIRS)
        s_bs = [s.astype(BF16) for s in states]
        ys += [_dot(ra, s_b, _NT) + yc for ra, yc, s_b in zip(ras[mine], y_consts[mine], s_bs)]
        states = [s * dend_ref[slot, j * SUBLANES:j * SUBLANES + 1, sl] + _dot(s_b, m) + cm
                  for s, s_b, m, cm, sl in zip(states, s_bs, m_offs[mine], consts[mine], sls)]
    for p, s in enumerate(states):
        state_ref[p] = s

    means = [_dot(_hilo_cols(y), avg_2) for y in ys]
    ycs = [y - m for y, m in zip(ys, means)]
    stats = [_dot(jnp.concatenate([(yc * yc).astype(BF16), opb("rkk", rs, sl)], axis=1), stat_bd)
             for yc, (_, rs, _, sl) in zip(ycs, tiles)]
    for yc, st, (_, rs, _, sl) in zip(ycs, stats, tiles):
        o_ref[rs, sl] = ((yc * lax.rsqrt(st[:, :LANES] + GN_EPS) * gn_g[:, sl] + gn_b[:, sl]
                          + st[:, LANES:] * opf("v", rs, sl)) * opf("g", rs, sl)
                         ).astype(o_ref.dtype)


def _gmlp_stages(zg_ref, lng_ref, lnb_ref, wst_ref, bs_ref, o_ref):
    n_tiles = GMLP_WIDTH // LANES
    cols = lambda base, p: slice(base + p * LANES, base + (p + 1) * LANES)
    rows = zg_ref.shape[0]
    chunks = [slice(j * CHUNK, (j + 1) * CHUNK) for j in range(rows // CHUNK)]
    vs = [zg_ref[:, cols(GMLP_WIDTH, p)] for p in range(n_tiles)]
    m = sum(jnp.sum(t, axis=-1, keepdims=True) for t in vs) * (1.0 / GMLP_WIDTH)
    vcs = [t - m for t in vs]
    var = sum(jnp.sum(t * t, axis=-1, keepdims=True) for t in vcs) * (1.0 / GMLP_WIDTH)
    inv = lax.rsqrt(var + LN_EPS)
    yield
    first = _iota2((CHUNK, LANES), 1) < HEAD
    for p in range(n_tiles):
        vn = (vcs[p] * inv * lng_ref[:, cols(0, p)] + lnb_ref[:, cols(0, p)]).astype(BF16)
        u = zg_ref[:, cols(0, p)]
        for rs in chunks:
            mixed = _dot(wst_ref[p], _stack_heads(vn[rs], first))
            o_ref[rs, cols(RWKV_WIDTH, p)] = (
                u[rs] * (mixed + bs_ref[:, cols(0, p)])).astype(o_ref.dtype)
        yield


def _interleaved(*stage_generators):
    for _ in itertools.zip_longest(*stage_generators):
        yield


def _mixer_kernel(zr_ref, zg_ref, w0_ref, wdec_ref, a0_ref, waaa_ref, wgate_ref, kk_ref,
                  ka_ref, rk_ref, gng_ref, gnb_ref, lng_ref, lnb_ref, ws_ref, bs_ref,
                  o_ref, state_ref, wst_ref, opb_ref, opf_ref, dend_ref):
    s = pl.program_id(0)

    @pl.when(s == 0)
    def _():
        state_ref[...] = jnp.zeros_like(state_ref)
        opb_ref[...] = jnp.zeros_like(opb_ref)
        opf_ref[...] = jnp.zeros_like(opf_ref)
        dend_ref[...] = jnp.zeros_like(dend_ref)
        causal = _iota2((CHUNK, CHUNK), 0) >= _iota2((CHUNK, CHUNK), 1)
        for h in range(ws_ref.shape[0]):
            wst_ref[h // 2, :, (h % 2) * CHUNK:(h % 2 + 1) * CHUNK] = (
                jnp.where(causal, ws_ref[h], 0.0).astype(BF16))

    prepare = _rwkv_prepare_stages(zr_ref, w0_ref, wdec_ref, a0_ref, waaa_ref, wgate_ref, kk_ref,
                                   ka_ref, rk_ref, opb_ref, opf_ref, dend_ref, s % 2)
    gmlp = _gmlp_stages(zg_ref, lng_ref, lnb_ref, wst_ref, bs_ref, o_ref)
    filler = _interleaved(prepare, gmlp)
    _rwkv_chunks(opb_ref, opf_ref, dend_ref, (s + 1) % 2, gng_ref[...], gnb_ref[...],
                 state_ref, o_ref, filler)
    for _ in filler:
        pass


def _mixer(zr, zg, w0, wdec, a0, waaa, wgate, k_k, k_a, r_k, gn_g, gn_b, ln_g, ln_b, w_s, b_s):
    rows = MIX_CHUNKS * CHUNK
    n = zr.shape[0] // rows
    n_heads = w_s.shape[0]

    def full(arr):
        nd = arr.ndim
        return pl.BlockSpec(arr.shape, lambda s: (0,) * nd)

    params = (w0, wdec, a0, waaa, wgate, k_k, k_a, r_k, gn_g, gn_b, ln_g, ln_b, w_s, b_s)
    return pl.pallas_call(
        _mixer_kernel,
        out_shape=jax.ShapeDtypeStruct((n * rows, D_MODEL), BF16),
        grid=(n + 1,),
        in_specs=[pl.BlockSpec((rows, RWKV_COLS), lambda s: (jnp.minimum(s, n - 1), 0)),
                  pl.BlockSpec((rows, 2 * GMLP_WIDTH), lambda s: (jnp.maximum(s - 1, 0), 0))]
                 + [full(a) for a in params],
        out_specs=pl.BlockSpec((rows, D_MODEL), lambda s: (jnp.maximum(s - 1, 0), 0)),
        scratch_shapes=[
            pltpu.VMEM((N_PAIRS, LANES, LANES), F32),
            pltpu.VMEM((n_heads // 2, CHUNK, 2 * CHUNK), BF16),
            pltpu.VMEM((2, len(_OPS_B), rows, RWKV_WIDTH), BF16),
            pltpu.VMEM((2, len(_OPS_F), rows, RWKV_WIDTH), F32),
            pltpu.VMEM((2, rows // RWKV_CHUNK * SUBLANES, RWKV_WIDTH), F32),
        ],
        compiler_params=pltpu.CompilerParams(
            dimension_semantics=("arbitrary",), vmem_limit_bytes=VMEM_LIMIT),
        name="mixer",
    )(zr, zg, *params)


def kernel(x, p, norm_ffn1, ffn1_w1, ffn1_w3, ffn1_w2, norm_mix, w_in, shift_mu, rwkv_w0,
           rwkv_w_decay, rwkv_a0, rwkv_w_aaa, rwkv_w_gate, rwkv_k_k, rwkv_k_a, rwkv_r_k,
           rwkv_gn_g, rwkv_gn_b, sgu_ln_g, sgu_ln_b, sgu_w_s, sgu_b_s, w_out, norm_ffn2,
           ffn2_w1, ffn2_w3, ffn2_w2, norm_ple, w_ple_gate, w_ple, norm_final):
    assert x.shape[0] == 1 and p.shape[0] == 1, "one batch row and one layer, as the problem states"
    row = lambda a: a.reshape(1, -1)
    bf = lambda a: a.astype(BF16)
    h, zr, zg = _pre_mixer(
        x[0], [row(norm_ffn1[0]), bf(ffn1_w1[0]), bf(ffn1_w3[0]), bf(ffn1_w2[0]),
               row(norm_mix[0]), bf(w_in[0]), row(shift_mu[0])])
    lora = rwkv_w_decay.shape[1]
    pad = jnp.zeros((LANES - lora, RWKV_WIDTH), F32)
    wdec = jnp.concatenate([rwkv_w_decay[0], pad], axis=0)
    waaa = jnp.concatenate([pad, rwkv_w_aaa[0]], axis=0)
    bias = jnp.repeat(sgu_b_s[0].T, HEAD, axis=1)
    y = _mixer(zr, zg, row(rwkv_w0[0]), bf(wdec), row(rwkv_a0[0]), bf(waaa),
               bf(rwkv_w_gate[0]), row(rwkv_k_k[0]), row(rwkv_k_a[0]), row(rwkv_r_k[0]),
               row(rwkv_gn_g[0]), row(rwkv_gn_b[0]), row(sgu_ln_g[0]), row(sgu_ln_b[0]),
               sgu_w_s[0], bias)
    out = _post_mixer(
        h, y, p[0, 0],
        [bf(w_out[0]), row(norm_ffn2[0]), bf(ffn2_w1[0]), bf(ffn2_w3[0]), bf(ffn2_w2[0]),
         row(norm_ple[0]), bf(w_ple_gate[0]), bf(w_ple[0]), row(norm_final)])
    return out[None]
```

```python
import itertools

import jax
import jax.numpy as jnp
from jax import lax
from jax.experimental import pallas as pl
from jax.experimental.pallas import tpu as pltpu

F32 = jnp.float32
BF16 = jnp.bfloat16

D_MODEL = 1024
D_FF = 2816
RWKV_WIDTH = 512
RWKV_COLS = 1792
GMLP_WIDTH = 512
HEAD = 64
LANES = 128
SUBLANES = 8
N_PAIRS = RWKV_WIDTH // LANES
CHUNK = 128
RWKV_CHUNK = 64
MIX_CHUNKS = 2
INV_BASE = 16
RMS_EPS = 1e-6
LN_EPS = 1e-5
GN_EPS = 64e-5
VMEM_LIMIT = 56 * 1024 * 1024

_NN = (((1,), (0,)), ((), ()))
_NT = (((1,), (1,)), ((), ()))
_TN = (((0,), (0,)), ((), ()))


def _dot(a, b, dn=_NN):
    return lax.dot_general(a, b, dn, preferred_element_type=F32)


def _split2(x):
    hi = x.astype(BF16)
    lo = (x - hi.astype(F32)).astype(BF16)
    return hi, lo


def _hilo_cols(x):
    return jnp.concatenate(_split2(x), axis=1)


def _hilo_rows(x):
    return jnp.concatenate(_split2(x), axis=0)


def _rms(x, g):
    ms = jnp.mean(x * x, axis=-1, keepdims=True)
    return x * lax.rsqrt(ms + RMS_EPS) * g


FF_CHUNK = 256


def _swiglu_hidden(xn, w1_ref, w3_ref, hm_ref, filler):
    for j in range(hm_ref.shape[1] // FF_CHUNK):
        cols = slice(j * FF_CHUNK, (j + 1) * FF_CHUNK)
        a = _dot(xn, w1_ref[:, cols])
        b = _dot(xn, w3_ref[:, cols])
        hm_ref[:, cols] = (a * jax.nn.sigmoid(a) * b).astype(BF16)
        next(filler, None)


def _rms_stages(x_ref, g_ref, out_ref, slot, rows_per_stage=64):
    for start in range(0, x_ref.shape[0], rows_per_stage):
        rs = slice(start, start + rows_per_stage)
        out_ref[slot, rs, :] = _rms(x_ref[rs, :], g_ref[...]).astype(BF16)
        yield


def _pre_mixer_kernel(x_ref, xnext_ref, g1_ref, w1_ref, w3_ref, w2_ref, gm_ref, win_ref, mu_ref,
                      h_ref, zr_ref, zg_ref, hm_ref, xn1_ref, xn2_ref, carry_ref):
    i = pl.program_id(0)
    last = pl.num_programs(0) - 1

    def project(slot):
        z = _dot(xn2_ref[slot], win_ref[...])
        rows = z.shape[0]
        first_row = _iota2((rows, LANES), 0) == 0
        for col in range(0, RWKV_COLS, LANES):
            cols = slice(col, col + LANES)
            zc = z[:, cols]
            zp = jnp.where(first_row, carry_ref[:, cols], pltpu.roll(zc, 1, 0))
            carry_ref[:, cols] = zc[rows - 1:rows, :]
            zr_ref[:, cols] = zc + (zp - zc) * mu_ref[:, cols]
        zg = z[:, RWKV_COLS:]
        zg_ref[...] = 0.5 * zg * (1.0 + lax.erf(zg * 0.7071067811865476))

    @pl.when(i == 0)
    def _():
        xn1_ref[0] = _rms(x_ref[...], g1_ref[...]).astype(BF16)
        xn2_ref[1] = jnp.zeros(xn2_ref.shape[1:], BF16)
        carry_ref[...] = jnp.zeros_like(carry_ref)

    @pl.when(i < last)
    def _():
        project((i + 1) % 2)
        filler = _rms_stages(xnext_ref, g1_ref, xn1_ref, (i + 1) % 2)
        _swiglu_hidden(xn1_ref[i % 2], w1_ref, w3_ref, hm_ref, filler)
        for _ in filler:
            pass
        acc = _dot(hm_ref[...], w2_ref[...])
        h = x_ref[...] + 0.5 * acc
        h_ref[...] = h
        xn2_ref[i % 2] = _rms(h, gm_ref[...]).astype(BF16)

    @pl.when(i == last)
    def _():
        project((i + 1) % 2)


def _post_mixer_kernel(h_ref, y_ref, p_ref, wout_ref, g2_ref, w1_ref, w3_ref, w2_ref,
                       gp_ref, wg_ref, wp_ref, gf_ref, o_ref, hm_ref):
    h = h_ref[...] + _dot(y_ref[...], wout_ref[...])
    _swiglu_hidden(_rms(h, g2_ref[...]).astype(BF16), w1_ref, w3_ref, hm_ref, iter(()))
    h = h + 0.5 * _dot(hm_ref[...], w2_ref[...])
    gate = jax.nn.sigmoid(_dot(_rms(h, gp_ref[...]).astype(BF16), wg_ref[...]))
    e = _dot(p_ref[...].astype(BF16), wp_ref[...])
    o_ref[...] = _rms(h + gate * e, gf_ref[...])


def _rows(tm, width, tile_of_step=lambda i: i):
    return pl.BlockSpec((tm, width), lambda i: (tile_of_step(i), 0))


def _resident(arr):
    nd = arr.ndim
    return pl.BlockSpec(arr.shape, lambda i: (0,) * nd, pipeline_mode=pl.Buffered(1))


def _pre_mixer(x, params, *, tm=512):
    t, d = x.shape
    n = t // tm
    this = lambda i: jnp.minimum(i, n - 1)
    nxt = lambda i: jnp.minimum(i + 1, n - 1)
    prev = lambda i: jnp.maximum(i - 1, 0)
    return pl.pallas_call(
        _pre_mixer_kernel,
        out_shape=(jax.ShapeDtypeStruct((t, d), F32),
                   jax.ShapeDtypeStruct((t, RWKV_COLS), F32),
                   jax.ShapeDtypeStruct((t, 2 * GMLP_WIDTH), F32)),
        grid=(n + 1,),
        in_specs=[_rows(tm, d, this), _rows(tm, d, nxt)] + [_resident(a) for a in params],
        out_specs=(_rows(tm, d, this), _rows(tm, RWKV_COLS, prev),
                   _rows(tm, 2 * GMLP_WIDTH, prev)),
        scratch_shapes=[pltpu.VMEM((tm, D_FF), BF16),
                        pltpu.VMEM((2, tm, d), BF16),
                        pltpu.VMEM((2, tm, d), BF16),
                        pltpu.VMEM((1, RWKV_COLS), F32)],
        compiler_params=pltpu.CompilerParams(
            dimension_semantics=("arbitrary",), vmem_limit_bytes=VMEM_LIMIT),
        name="pre_mixer",
    )(x, x, *params)


def _post_mixer(h, y, p, params, *, tm=1024):
    t, d = h.shape
    return pl.pallas_call(
        _post_mixer_kernel,
        out_shape=jax.ShapeDtypeStruct((t, d), F32),
        grid=(t // tm,),
        in_specs=[_rows(tm, a.shape[1]) for a in (h, y, p)] + [_resident(a) for a in params],
        out_specs=_rows(tm, d),
        scratch_shapes=[pltpu.VMEM((tm, D_FF), BF16)],
        compiler_params=pltpu.CompilerParams(
            dimension_semantics=("arbitrary",), vmem_limit_bytes=VMEM_LIMIT),
        name="post_mixer",
    )(h, y, p, *params)


def _iota2(shape, dim):
    return lax.broadcasted_iota(jnp.int32, shape, dim)


def _stack_heads(x, first):
    zero = jnp.zeros_like(x)
    return jnp.concatenate([jnp.where(first, x, zero), jnp.where(first, zero, x)], axis=0)


def _nilpotent_inverses(xs, eye, index, bd, filler):
    w = eye.shape[1]
    pows = [x.astype(BF16) for x in xs]
    prods = [(eye + p).astype(BF16) for p in pows]
    pows = [_dot(p, bd(p)).astype(BF16) for p in pows]
    next(filler, None)
    covered = 4
    while covered < index:
        outs = [_dot(p, jnp.concatenate([bd(p), bd(t)], axis=1)) for p, t in zip(pows, prods)]
        pows = [o[:, :w].astype(BF16) for o in outs]
        prods = [(t + o[:, w:]).astype(BF16) for t, o in zip(prods, outs)]
        next(filler, None)
        covered *= 2
    prods = [(t + _dot(p, bd(t))).astype(BF16) for p, t in zip(pows, prods)]
    next(filler, None)
    return prods


def _unit_lower_inverses(mats, c, filler):
    row = _iota2((c, 2 * c), 0)
    lane = _iota2((c, 2 * c), 1)
    col = lane & (c - 1)
    left = lane < c
    shift = INV_BASE.bit_length() - 1
    diag_block = (row >> shift) == (col >> shift)
    eye = jnp.where(row == col, 1.0, 0.0)

    def bd(x):
        zero = jnp.zeros_like(x)
        return jnp.concatenate([jnp.where(left, x, zero), jnp.where(left, zero, x)], axis=0)

    t_diag = _nilpotent_inverses([jnp.where(diag_block, a, 0.0) for a in mats], eye, INV_BASE,
                                 bd, filler)
    n_mats = [_dot(t, bd(jnp.where(diag_block, 0.0, a))) for t, a in zip(t_diag, mats)]
    next(filler, None)
    t_off = _nilpotent_inverses(n_mats, eye, c // INV_BASE, bd, filler)
    return [_dot(t, bd(td)).astype(BF16) for t, td in zip(t_off, t_diag)]


_OPS_B = ("rt", "at", "bt", "kt", "r0", "a0", "bh", "kh", "vb", "rkk")
_OPS_F = ("v", "g")


def _rwkv_prepare_stages(zr_ref, w0_ref, wdec_ref, a0_ref, waaa_ref,
                         wgate_ref, kk_ref, ka_ref, rk_ref, opb_ref, opf_ref, dend_ref, slot):
    c = RWKV_CHUNK
    rows = zr_ref.shape[0]
    chunks = [slice(j * c, (j + 1) * c) for j in range(rows // c)]
    shifted = lambda col: zr_ref[:, col:col + LANES]

    w = RWKV_WIDTH
    zwa = shifted(3 * w)
    zwa_tanh = jnp.tanh(zwa).astype(BF16)
    zwa_b = zwa.astype(BF16)
    zg_sig = jax.nn.sigmoid(shifted(3 * w + LANES)).astype(BF16)
    blockdiag = (_iota2((LANES, LANES), 0) >= HEAD) == (_iota2((LANES, LANES), 1) >= HEAD)
    ones_bd = blockdiag.astype(BF16)
    row2 = _iota2((c, 2 * c), 0)
    col2 = _iota2((c, 2 * c), 1) & (c - 1)
    tril_2 = (row2 >= col2).astype(BF16)
    yield
    for p in range(N_PAIRS):
        sl = slice(p * LANES, (p + 1) * LANES)
        r, k, v = shifted(p * LANES), shifted(w + p * LANES), shifted(2 * w + p * LANES)
        lw = w0_ref[:, sl] + _dot(zwa_tanh, wdec_ref[:, sl])
        logd = -0.6065306597126334 * jax.nn.sigmoid(lw)
        a = jax.nn.sigmoid(a0_ref[:, sl] + _dot(zwa_b, waaa_ref[:, sl]))
        g = _dot(zg_sig, wgate_ref[:, sl])
        kk = k * kk_ref[:, sl]
        n2 = _dot((kk * kk).astype(BF16), ones_bd)
        kkn = kk * lax.rsqrt(jnp.maximum(n2, 1e-24))
        k2 = k * (1.0 + (a - 1.0) * ka_ref[:, sl])
        bvec = kkn * a
        opb_ref[slot, _OPS_B.index("rkk"), :, sl] = (r * k2 * rk_ref[:, sl]).astype(BF16)
        opf_ref[slot, _OPS_F.index("v"), :, sl] = v
        opf_ref[slot, _OPS_F.index("g"), :, sl] = g
        yield
        for j, rs in enumerate(chunks):
            lcum = _dot(tril_2, _hilo_rows(logd[rs]))
            lmid = lcum[c // 2 - 1:c // 2, :]
            lend = lcum[c - 1:c, :]
            lexc = lcum - logd[rs]
            e_mid_from = jnp.exp(lmid - lcum)
            e_end = jnp.exp(lend - lcum)
            ops = dict(
                rt=r[rs] * jnp.exp(lcum - lmid), at=-kkn[rs] * jnp.exp(lexc - lmid),
                bt=bvec[rs] * e_mid_from, kt=k2[rs] * e_mid_from,
                r0=r[rs] * jnp.exp(lcum), a0=-kkn[rs] * jnp.exp(lexc),
                bh=bvec[rs] * e_end, kh=k2[rs] * e_end, vb=v[rs])
            for name, value in ops.items():
                opb_ref[slot, _OPS_B.index(name), rs, sl] = value.astype(BF16)
            dend_ref[slot, j * SUBLANES:(j + 1) * SUBLANES, sl] = jnp.broadcast_to(
                jnp.exp(lend), (SUBLANES, LANES))
        yield


def _rwkv_chunks(opb_ref, opf_ref, dend_ref, slot, gn_g, gn_b, state_ref, o_ref, filler):
    c = RWKV_CHUNK
    row2 = _iota2((c, 2 * c), 0)
    col2 = _iota2((c, 2 * c), 1) & (c - 1)
    incl2 = row2 >= col2
    strict2 = row2 > col2
    first = _iota2((c, LANES), 1) < HEAD
    blockdiag = (_iota2((LANES, LANES), 0) >= HEAD) == (_iota2((LANES, LANES), 1) >= HEAD)
    ones_bd = blockdiag.astype(BF16)
    avg_bd = (blockdiag.astype(F32) * (1.0 / HEAD)).astype(BF16)
    zero_bd = jnp.zeros((LANES, LANES), BF16)
    avg_2 = jnp.concatenate([avg_bd, avg_bd], axis=0)
    stat_bd = jnp.concatenate([jnp.concatenate([avg_bd, zero_bd], axis=1),
                               jnp.concatenate([zero_bd, ones_bd], axis=1)], axis=0)
    pairs = range(N_PAIRS)
    sls = [slice(p * LANES, (p + 1) * LANES) for p in pairs]
    chunks = [slice(j * c, (j + 1) * c) for j in range(opb_ref.shape[2] // c)]
    tiles = [(j, rs, p, sl) for j, rs in enumerate(chunks) for p, sl in enumerate(sls)]
    opb = lambda name, rs, sl: opb_ref[slot, _OPS_B.index(name), rs, sl]
    opf = lambda name, rs, sl: opf_ref[slot, _OPS_F.index(name), rs, sl]

    grams = [
        _dot(jnp.concatenate([opb("at", rs, sl), opb("rt", rs, sl)], axis=0),
             jnp.concatenate([_stack_heads(opb("bt", rs, sl), first),
                              _stack_heads(opb("kt", rs, sl), first)], axis=0), _NT)
        for _, rs, _, sl in tiles]
    a_abs = [jnp.where(strict2, gm[:c, :2 * c], 0.0).astype(BF16) for gm in grams]
    t_invs = _unit_lower_inverses(a_abs, c, filler)
    a_rbs = [jnp.where(incl2, gm[c:, :2 * c], 0.0).astype(BF16) for gm in grams]
    akvs = [
        _dot(jnp.concatenate([jnp.where(strict2, gm[:c, 2 * c:], 0.0),
                              jnp.where(incl2, gm[c:, 2 * c:], 0.0)], axis=0).astype(BF16),
             _stack_heads(opb("vb", rs, sl), first))
        for gm, (_, rs, _, sl) in zip(grams, tiles)]

    ta0s = [_dot(t, _stack_heads(opb("a0", rs, sl), first)).astype(BF16)
            for t, (_, rs, _, sl) in zip(t_invs, tiles)]
    tavs = [_dot(t, _stack_heads(akv[:c].astype(BF16), first)).astype(BF16)
            for t, akv in zip(t_invs, akvs)]
    next(filler, None)
    m_offs = [jnp.where(blockdiag, _dot(ta0, opb("bh", rs, sl), _TN), 0.0).astype(BF16)
              for ta0, (_, rs, _, sl) in zip(ta0s, tiles)]
    consts = [jnp.where(blockdiag,
                        _dot(jnp.concatenate([tav, opb("vb", rs, sl)], axis=0),
                             jnp.concatenate([opb("bh", rs, sl), opb("kh", rs, sl)], axis=0), _TN),
                        0.0)
              for tav, (_, rs, _, sl) in zip(tavs, tiles)]
    next(filler, None)
    ras = [(opb("r0", rs, sl).astype(F32) + _dot(a_rb, _stack_heads(ta0, first))).astype(BF16)
           for a_rb, ta0, (_, rs, _, sl) in zip(a_rbs, ta0s, tiles)]
    y_consts = [akv[c:] + _dot(a_rb, _stack_heads(tav, first))
                for akv, a_rb, tav in zip(akvs, a_rbs, tavs)]
    next(filler, None)

    states = [state_ref[p] for p in pairs]
    ys = []
    for j in range(len(chunks)):
        mine = slice(j * N_PAIRS, (j + 1) * N_PAIRS)
        s_bs = [s.astype(BF16) for s in states]
        ys += [_dot(ra, s_b, _NT) + yc for ra, yc, s_b in zip(ras[mine], y_consts[mine], s_bs)]
        states = [s * dend_ref[slot, j * SUBLANES:j * SUBLANES + 1, sl] + _dot(s_b, m) + cm
                  for s, s_b, m, cm, sl in zip(states, s_bs, m_offs[mine], consts[mine], sls)]
    for p, s in enumerate(states):
        state_ref[p] = s

    means = [_dot(_hilo_cols(y), avg_2) for y in ys]
    ycs = [y - m for y, m in zip(ys, means)]
    stats = [_dot(jnp.concatenate([(yc * yc).astype(BF16), opb("rkk", rs, sl)], axis=1), stat_bd)
             for yc, (_, rs, _, sl) in zip(ycs, tiles)]
    for yc, st, (_, rs, _, sl) in zip(ycs, stats, tiles):
        o_ref[rs, sl] = ((yc * lax.rsqrt(st[:, :LANES] + GN_EPS) * gn_g[:, sl] + gn_b[:, sl]
                          + st[:, LANES:] * opf("v", rs, sl)) * opf("g", rs, sl)
                         ).astype(o_ref.dtype)


def _gmlp_stages(zg_ref, lng_ref, lnb_ref, wst_ref, bs_ref, o_ref):
    n_tiles = GMLP_WIDTH // LANES
    cols = lambda base, p: slice(base + p * LANES, base + (p + 1) * LANES)
    rows = zg_ref.shape[0]
    chunks = [slice(j * CHUNK, (j + 1) * CHUNK) for j in range(rows // CHUNK)]
    vs = [zg_ref[:, cols(GMLP_WIDTH, p)] for p in range(n_tiles)]
    m = sum(jnp.sum(t, axis=-1, keepdims=True) for t in vs) * (1.0 / GMLP_WIDTH)
    vcs = [t - m for t in vs]
    var = sum(jnp.sum(t * t, axis=-1, keepdims=True) for t in vcs) * (1.0 / GMLP_WIDTH)
    inv = lax.rsqrt(var + LN_EPS)
    yield
    first = _iota2((CHUNK, LANES), 1) < HEAD
    for p in range(n_tiles):
        vn = (vcs[p] * inv * lng_ref[:, cols(0, p)] + lnb_ref[:, cols(0, p)]).astype(BF16)
        u = zg_ref[:, cols(0, p)]
        for rs in chunks:
            mixed = _dot(wst_ref[p], _stack_heads(vn[rs], first))
            o_ref[rs, cols(RWKV_WIDTH, p)] = (
                u[rs] * (mixed + bs_ref[:, cols(0, p)])).astype(o_ref.dtype)
        yield


def _interleaved(*stage_generators):
    for _ in itertools.zip_longest(*stage_generators):
        yield


def _mixer_kernel(zr_ref, zg_ref, w0_ref, wdec_ref, a0_ref, waaa_ref, wgate_ref, kk_ref,
                  ka_ref, rk_ref, gng_ref, gnb_ref, lng_ref, lnb_ref, ws_ref, bs_ref,
                  o_ref, state_ref, wst_ref, opb_ref, opf_ref, dend_ref):
    s = pl.program_id(0)

    @pl.when(s == 0)
    def _():
        state_ref[...] = jnp.zeros_like(state_ref)
        opb_ref[...] = jnp.zeros_like(opb_ref)
        opf_ref[...] = jnp.zeros_like(opf_ref)
        dend_ref[...] = jnp.zeros_like(dend_ref)
        causal = _iota2((CHUNK, CHUNK), 0) >= _iota2((CHUNK, CHUNK), 1)
        for h in range(ws_ref.shape[0]):
            wst_ref[h // 2, :, (h % 2) * CHUNK:(h % 2 + 1) * CHUNK] = (
                jnp.where(causal, ws_ref[h], 0.0).astype(BF16))

    prepare = _rwkv_prepare_stages(zr_ref, w0_ref, wdec_ref, a0_ref, waaa_ref, wgate_ref, kk_ref,
                                   ka_ref, rk_ref, opb_ref, opf_ref, dend_ref, s % 2)
    gmlp = _gmlp_stages(zg_ref, lng_ref, lnb_ref, wst_ref, bs_ref, o_ref)
    filler = _interleaved(prepare, gmlp)
    _rwkv_chunks(opb_ref, opf_ref, dend_ref, (s + 1) % 2, gng_ref[...], gnb_ref[...],
                 state_ref, o_ref, filler)
    for _ in filler:
        pass


def _mixer(zr, zg, w0, wdec, a0, waaa, wgate, k_k, k_a, r_k, gn_g, gn_b, ln_g, ln_b, w_s, b_s):
    rows = MIX_CHUNKS * CHUNK
    n = zr.shape[0] // rows
    n_heads = w_s.shape[0]

    def full(arr):
        nd = arr.ndim
        return pl.BlockSpec(arr.shape, lambda s: (0,) * nd)

    params = (w0, wdec, a0, waaa, wgate, k_k, k_a, r_k, gn_g, gn_b, ln_g, ln_b, w_s, b_s)
    return pl.pallas_call(
        _mixer_kernel,
        out_shape=jax.ShapeDtypeStruct((n * rows, D_MODEL), BF16),
        grid=(n + 1,),
        in_specs=[pl.BlockSpec((rows, RWKV_COLS), lambda s: (jnp.minimum(s, n - 1), 0)),
                  pl.BlockSpec((rows, 2 * GMLP_WIDTH), lambda s: (jnp.maximum(s - 1, 0), 0))]
                 + [full(a) for a in params],
        out_specs=pl.BlockSpec((rows, D_MODEL), lambda s: (jnp.maximum(s - 1, 0), 0)),
        scratch_shapes=[
            pltpu.VMEM((N_PAIRS, LANES, LANES), F32),
            pltpu.VMEM((n_heads // 2, CHUNK, 2 * CHUNK), BF16),
            pltpu.VMEM((2, len(_OPS_B), rows, RWKV_WIDTH), BF16),
            pltpu.VMEM((2, len(_OPS_F), rows, RWKV_WIDTH), F32),
            pltpu.VMEM((2, rows // RWKV_CHUNK * SUBLANES, RWKV_WIDTH), F32),
        ],
        compiler_params=pltpu.CompilerParams(
            dimension_semantics=("arbitrary",), vmem_limit_bytes=VMEM_LIMIT),
        name="mixer",
    )(zr, zg, *params)


def kernel(x, p, norm_ffn1, ffn1_w1, ffn1_w3, ffn1_w2, norm_mix, w_in, shift_mu, rwkv_w0,
           rwkv_w_decay, rwkv_a0, rwkv_w_aaa, rwkv_w_gate, rwkv_k_k, rwkv_k_a, rwkv_r_k,
           rwkv_gn_g, rwkv_gn_b, sgu_ln_g, sgu_ln_b, sgu_w_s, sgu_b_s, w_out, norm_ffn2,
           ffn2_w1, ffn2_w3, ffn2_w2, norm_ple, w_ple_gate, w_ple, norm_final):
    assert x.shape[0] == 1 and p.shape[0] == 1, "one batch row and one layer, as the problem states"
    row = lambda a: a.reshape(1, -1)
    bf = lambda a: a.astype(BF16)
    h, zr, zg = _pre_mixer(
        x[0], [row(norm_ffn1[0]), bf(ffn1_w1[0]), bf(ffn1_w3[0]), bf(ffn1_w2[0]),
               row(norm_mix[0]), bf(w_in[0]), row(shift_mu[0])])
    lora = rwkv_w_decay.shape[1]
    pad = jnp.zeros((LANES - lora, RWKV_WIDTH), F32)
    wdec = jnp.concatenate([rwkv_w_decay[0], pad], axis=0)
    waaa = jnp.concatenate([pad, rwkv_w_aaa[0]], axis=0)
    bias = jnp.repeat(sgu_b_s[0].T, HEAD, axis=1)
    y = _mixer(zr, zg, row(rwkv_w0[0]), bf(wdec), row(rwkv_a0[0]), bf(waaa),
               bf(rwkv_w_gate[0]), row(rwkv_k_k[0]), row(rwkv_k_a[0]), row(rwkv_r_k[0]),
               row(rwkv_gn_g[0]), row(rwkv_gn_b[0]), row(sgu_ln_g[0]), row(sgu_ln_b[0]),
               sgu_w_s[0], bias)
    out = _post_mixer(
        h, y, p[0, 0],
        [bf(w_out[0]), row(norm_ffn2[0]), bf(ffn2_w1[0]), bf(ffn2_w3[0]), bf(ffn2_w2[0]),
         row(norm_ple[0]), bf(w_ple_gate[0]), bf(w_ple[0]), row(norm_final)])
    return out[None]
```

```python
import itertools

import jax
import jax.numpy as jnp
from jax import lax
from jax.experimental import pallas as pl
from jax.experimental.pallas import tpu as pltpu

F32 = jnp.float32
BF16 = jnp.bfloat16

D_MODEL = 1024
D_FF = 2816
RWKV_WIDTH = 512
RWKV_COLS = 1792
GMLP_WIDTH = 512
HEAD = 64
LANES = 128
SUBLANES = 8
N_PAIRS = RWKV_WIDTH // LANES
CHUNK = 128
RWKV_CHUNK = 64
MIX_CHUNKS = 4
INV_BASE = 16
RMS_EPS = 1e-6
LN_EPS = 1e-5
GN_EPS = 64e-5
VMEM_LIMIT = 56 * 1024 * 1024

_NN = (((1,), (0,)), ((), ()))
_NT = (((1,), (1,)), ((), ()))
_TN = (((0,), (0,)), ((), ()))


def _dot(a, b, dn=_NN):
    return lax.dot_general(a, b, dn, preferred_element_type=F32)


def _split2(x):
    hi = x.astype(BF16)
    lo = (x - hi.astype(F32)).astype(BF16)
    return hi, lo


def _hilo_cols(x):
    return jnp.concatenate(_split2(x), axis=1)


def _hilo_rows(x):
    return jnp.concatenate(_split2(x), axis=0)


def _rms(x, g):
    ms = jnp.mean(x * x, axis=-1, keepdims=True)
    return x * lax.rsqrt(ms + RMS_EPS) * g


FF_CHUNK = 256


def _swiglu_hidden(xn, w1_ref, w3_ref, hm_ref, filler):
    for j in range(hm_ref.shape[1] // FF_CHUNK):
        cols = slice(j * FF_CHUNK, (j + 1) * FF_CHUNK)
        a = _dot(xn, w1_ref[:, cols])
        b = _dot(xn, w3_ref[:, cols])
        hm_ref[:, cols] = (a * jax.nn.sigmoid(a) * b).astype(BF16)
        next(filler, None)


def _rms_stages(x_ref, g_ref, out_ref, slot, rows_per_stage=64):
    for start in range(0, x_ref.shape[0], rows_per_stage):
        rs = slice(start, start + rows_per_stage)
        out_ref[slot, rs, :] = _rms(x_ref[rs, :], g_ref[...]).astype(BF16)
        yield


def _pre_mixer_kernel(x_ref, xnext_ref, g1_ref, w1_ref, w3_ref, w2_ref, gm_ref, win_ref, mu_ref,
                      h_ref, zr_ref, zg_ref, hm_ref, xn1_ref, xn2_ref, carry_ref):
    i = pl.program_id(0)
    last = pl.num_programs(0) - 1

    def project(slot):
        z = _dot(xn2_ref[slot], win_ref[...])
        rows = z.shape[0]
        first_row = _iota2((rows, LANES), 0) == 0
        for col in range(0, RWKV_COLS, LANES):
            cols = slice(col, col + LANES)
            zc = z[:, cols]
            zp = jnp.where(first_row, carry_ref[:, cols], pltpu.roll(zc, 1, 0))
            carry_ref[:, cols] = zc[rows - 1:rows, :]
            zr_ref[:, cols] = zc + (zp - zc) * mu_ref[:, cols]
        zg = z[:, RWKV_COLS:]
        zg_ref[...] = 0.5 * zg * (1.0 + lax.erf(zg * 0.7071067811865476))

    @pl.when(i == 0)
    def _():
        xn1_ref[0] = _rms(x_ref[...], g1_ref[...]).astype(BF16)
        xn2_ref[1] = jnp.zeros(xn2_ref.shape[1:], BF16)
        carry_ref[...] = jnp.zeros_like(carry_ref)

    @pl.when(i < last)
    def _():
        project((i + 1) % 2)
        filler = _rms_stages(xnext_ref, g1_ref, xn1_ref, (i + 1) % 2)
        _swiglu_hidden(xn1_ref[i % 2], w1_ref, w3_ref, hm_ref, filler)
        for _ in filler:
            pass
        acc = _dot(hm_ref[...], w2_ref[...])
        h = x_ref[...] + 0.5 * acc
        h_ref[...] = h
        xn2_ref[i % 2] = _rms(h, gm_ref[...]).astype(BF16)

    @pl.when(i == last)
    def _():
        project((i + 1) % 2)


def _post_mixer_kernel(h_ref, y_ref, p_ref, wout_ref, g2_ref, w1_ref, w3_ref, w2_ref,
                       gp_ref, wg_ref, wp_ref, gf_ref, o_ref, hm_ref):
    h = h_ref[...] + _dot(y_ref[...], wout_ref[...])
    _swiglu_hidden(_rms(h, g2_ref[...]).astype(BF16), w1_ref, w3_ref, hm_ref, iter(()))
    h = h + 0.5 * _dot(hm_ref[...], w2_ref[...])
    gate = jax.nn.sigmoid(_dot(_rms(h, gp_ref[...]).astype(BF16), wg_ref[...]))
    e = _dot(p_ref[...].astype(BF16), wp_ref[...])
    o_ref[...] = _rms(h + gate * e, gf_ref[...])


def _rows(tm, width, tile_of_step=lambda i: i):
    return pl.BlockSpec((tm, width), lambda i: (tile_of_step(i), 0))


def _resident(arr):
    nd = arr.ndim
    return pl.BlockSpec(arr.shape, lambda i: (0,) * nd, pipeline_mode=pl.Buffered(1))


def _pre_mixer(x, params, *, tm=512):
    t, d = x.shape
    n = t // tm
    this = lambda i: jnp.minimum(i, n - 1)
    nxt = lambda i: jnp.minimum(i + 1, n - 1)
    prev = lambda i: jnp.maximum(i - 1, 0)
    return pl.pallas_call(
        _pre_mixer_kernel,
        out_shape=(jax.ShapeDtypeStruct((t, d), F32),
                   jax.ShapeDtypeStruct((t, RWKV_COLS), F32),
                   jax.ShapeDtypeStruct((t, 2 * GMLP_WIDTH), F32)),
        grid=(n + 1,),
        in_specs=[_rows(tm, d, this), _rows(tm, d, nxt)] + [_resident(a) for a in params],
        out_specs=(_rows(tm, d, this), _rows(tm, RWKV_COLS, prev),
                   _rows(tm, 2 * GMLP_WIDTH, prev)),
        scratch_shapes=[pltpu.VMEM((tm, D_FF), BF16),
                        pltpu.VMEM((2, tm, d), BF16),
                        pltpu.VMEM((2, tm, d), BF16),
                        pltpu.VMEM((1, RWKV_COLS), F32)],
        compiler_params=pltpu.CompilerParams(
            dimension_semantics=("arbitrary",), vmem_limit_bytes=VMEM_LIMIT),
        name="pre_mixer",
    )(x, x, *params)


def _post_mixer(h, y, p, params, *, tm=1024):
    t, d = h.shape
    return pl.pallas_call(
        _post_mixer_kernel,
        out_shape=jax.ShapeDtypeStruct((t, d), F32),
        grid=(t // tm,),
        in_specs=[_rows(tm, a.shape[1]) for a in (h, y, p)] + [_resident(a) for a in params],
        out_specs=_rows(tm, d),
        scratch_shapes=[pltpu.VMEM((tm, D_FF), BF16)],
        compiler_params=pltpu.CompilerParams(
            dimension_semantics=("arbitrary",), vmem_limit_bytes=VMEM_LIMIT),
        name="post_mixer",
    )(h, y, p, *params)


def _iota2(shape, dim):
    return lax.broadcasted_iota(jnp.int32, shape, dim)


def _stack_heads(x, first):
    zero = jnp.zeros_like(x)
    return jnp.concatenate([jnp.where(first, x, zero), jnp.where(first, zero, x)], axis=0)


def _nilpotent_inverses(xs, eye, index, bd, filler):
    w = eye.shape[1]
    pows = [x.astype(BF16) for x in xs]
    prods = [(eye + p).astype(BF16) for p in pows]
    pows = [_dot(p, bd(p)).astype(BF16) for p in pows]
    next(filler, None)
    covered = 4
    while covered < index:
        outs = [_dot(p, jnp.concatenate([bd(p), bd(t)], axis=1)) for p, t in zip(pows, prods)]
        pows = [o[:, :w].astype(BF16) for o in outs]
        prods = [(t + o[:, w:]).astype(BF16) for t, o in zip(prods, outs)]
        next(filler, None)
        covered *= 2
    prods = [(t + _dot(p, bd(t))).astype(BF16) for p, t in zip(pows, prods)]
    next(filler, None)
    return prods


def _unit_lower_inverses(mats, c, filler):
    row = _iota2((c, 2 * c), 0)
    lane = _iota2((c, 2 * c), 1)
    col = lane & (c - 1)
    left = lane < c
    shift = INV_BASE.bit_length() - 1
    diag_block = (row >> shift) == (col >> shift)
    eye = jnp.where(row == col, 1.0, 0.0)

    def bd(x):
        zero = jnp.zeros_like(x)
        return jnp.concatenate([jnp.where(left, x, zero), jnp.where(left, zero, x)], axis=0)

    t_diag = _nilpotent_inverses([jnp.where(diag_block, a, 0.0) for a in mats], eye, INV_BASE,
                                 bd, filler)
    n_mats = [_dot(t, bd(jnp.where(diag_block, 0.0, a))) for t, a in zip(t_diag, mats)]
    next(filler, None)
    t_off = _nilpotent_inverses(n_mats, eye, c // INV_BASE, bd, filler)
    return [_dot(t, bd(td)).astype(BF16) for t, td in zip(t_off, t_diag)]


_OPS_B = ("rt", "at", "bt", "kt", "r0", "a0", "bh", "kh", "vb", "rkk")
_OPS_F = ("v", "g")


def _rwkv_prepare_stages(zr_ref, w0_ref, wdec_ref, a0_ref, waaa_ref,
                         wgate_ref, kk_ref, ka_ref, rk_ref, opb_ref, opf_ref, dend_ref, slot):
    c = RWKV_CHUNK
    rows = zr_ref.shape[0]
    chunks = [slice(j * c, (j + 1) * c) for j in range(rows // c)]
    shifted = lambda col: zr_ref[:, col:col + LANES]

    w = RWKV_WIDTH
    zwa = shifted(3 * w)
    zwa_tanh = jnp.tanh(zwa).astype(BF16)
    zwa_b = zwa.astype(BF16)
    zg_sig = jax.nn.sigmoid(shifted(3 * w + LANES)).astype(BF16)
    blockdiag = (_iota2((LANES, LANES), 0) >= HEAD) == (_iota2((LANES, LANES), 1) >= HEAD)
    ones_bd = blockdiag.astype(BF16)
    row2 = _iota2((c, 2 * c), 0)
    col2 = _iota2((c, 2 * c), 1) & (c - 1)
    tril_2 = (row2 >= col2).astype(BF16)
    yield
    for p in range(N_PAIRS):
        sl = slice(p * LANES, (p + 1) * LANES)
        r, k, v = shifted(p * LANES), shifted(w + p * LANES), shifted(2 * w + p * LANES)
        lw = w0_ref[:, sl] + _dot(zwa_tanh, wdec_ref[:, sl])
        logd = -0.6065306597126334 * jax.nn.sigmoid(lw)
        a = jax.nn.sigmoid(a0_ref[:, sl] + _dot(zwa_b, waaa_ref[:, sl]))
        g = _dot(zg_sig, wgate_ref[:, sl])
        kk = k * kk_ref[:, sl]
        n2 = _dot((kk * kk).astype(BF16), ones_bd)
        kkn = kk * lax.rsqrt(jnp.maximum(n2, 1e-24))
        k2 = k * (1.0 + (a - 1.0) * ka_ref[:, sl])
        bvec = kkn * a
        opb_ref[slot, _OPS_B.index("rkk"), :, sl] = (r * k2 * rk_ref[:, sl]).astype(BF16)
        opf_ref[slot, _OPS_F.index("v"), :, sl] = v
        opf_ref[slot, _OPS_F.index("g"), :, sl] = g
        yield
        for j, rs in enumerate(chunks):
            lcum = _dot(tril_2, _hilo_rows(logd[rs]))
            lmid = lcum[c // 2 - 1:c // 2, :]
            lend = lcum[c - 1:c, :]
            lexc = lcum - logd[rs]
            e_mid_from = jnp.exp(lmid - lcum)
            e_end = jnp.exp(lend - lcum)
            ops = dict(
                rt=r[rs] * jnp.exp(lcum - lmid), at=-kkn[rs] * jnp.exp(lexc - lmid),
                bt=bvec[rs] * e_mid_from, kt=k2[rs] * e_mid_from,
                r0=r[rs] * jnp.exp(lcum), a0=-kkn[rs] * jnp.exp(lexc),
                bh=bvec[rs] * e_end, kh=k2[rs] * e_end, vb=v[rs])
            for name, value in ops.items():
                opb_ref[slot, _OPS_B.index(name), rs, sl] = value.astype(BF16)
            dend_ref[slot, j * SUBLANES:(j + 1) * SUBLANES, sl] = jnp.broadcast_to(
                jnp.exp(lend), (SUBLANES, LANES))
        yield


def _rwkv_chunks(opb_ref, opf_ref, dend_ref, slot, gn_g, gn_b, state_ref, o_ref, filler):
    c = RWKV_CHUNK
    row2 = _iota2((c, 2 * c), 0)
    col2 = _iota2((c, 2 * c), 1) & (c - 1)
    incl2 = row2 >= col2
    strict2 = row2 > col2
    first = _iota2((c, LANES), 1) < HEAD
    blockdiag = (_iota2((LANES, LANES), 0) >= HEAD) == (_iota2((LANES, LANES), 1) >= HEAD)
    ones_bd = blockdiag.astype(BF16)
    avg_bd = (blockdiag.astype(F32) * (1.0 / HEAD)).astype(BF16)
    zero_bd = jnp.zeros((LANES, LANES), BF16)
    avg_2 = jnp.concatenate([avg_bd, avg_bd], axis=0)
    stat_bd = jnp.concatenate([jnp.concatenate([avg_bd, zero_bd], axis=1),
                               jnp.concatenate([zero_bd, ones_bd], axis=1)], axis=0)
    pairs = range(N_PAIRS)
    sls = [slice(p * LANES, (p + 1) * LANES) for p in pairs]
    chunks = [slice(j * c, (j + 1) * c) for j in range(opb_ref.shape[2] // c)]
    tiles = [(j, rs, p, sl) for j, rs in enumerate(chunks) for p, sl in enumerate(sls)]
    opb = lambda name, rs, sl: opb_ref[slot, _OPS_B.index(name), rs, sl]
    opf = lambda name, rs, sl: opf_ref[slot, _OPS_F.index(name), rs, sl]

    grams = [
        _dot(jnp.concatenate([opb("at", rs, sl), opb("rt", rs, sl)], axis=0),
             jnp.concatenate([_stack_heads(opb("bt", rs, sl), first),
                              _stack_heads(opb("kt", rs, sl), first)], axis=0), _NT)
        for _, rs, _, sl in tiles]
    a_abs = [jnp.where(strict2, gm[:c, :2 * c], 0.0).astype(BF16) for gm in grams]
    t_invs = _unit_lower_inverses(a_abs, c, filler)
    a_rbs = [jnp.where(incl2, gm[c:, :2 * c], 0.0).astype(BF16) for gm in grams]
    akvs = [
        _dot(jnp.concatenate([jnp.where(strict2, gm[:c, 2 * c:], 0.0),
                              jnp.where(incl2, gm[c:, 2 * c:], 0.0)], axis=0).astype(BF16),
             _stack_heads(opb("vb", rs, sl), first))
        for gm, (_, rs, _, sl) in zip(grams, tiles)]

    ta0s = [_dot(t, _stack_heads(opb("a0", rs, sl), first)).astype(BF16)
            for t, (_, rs, _, sl) in zip(t_invs, tiles)]
    tavs = [_dot(t, _stack_heads(akv[:c].astype(BF16), first)).astype(BF16)
            for t, akv in zip(t_invs, akvs)]
    next(filler, None)
    m_offs = [jnp.where(blockdiag, _dot(ta0, opb("bh", rs, sl), _TN), 0.0).astype(BF16)
              for ta0, (_, rs, _, sl) in zip(ta0s, tiles)]
    consts = [jnp.where(blockdiag,
                        _dot(jnp.concatenate([tav, opb("vb", rs, sl)], axis=0),
                             jnp.concatenate([opb("bh", rs, sl), opb("kh", rs, sl)], axis=0), _TN),
                        0.0)
              for tav, (_, rs, _, sl) in zip(tavs, tiles)]
    next(filler, None)
    ras = [(opb("r0", rs, sl).astype(F32) + _dot(a_rb, _stack_heads(ta0, first))).astype(BF16)
           for a_rb, ta0, (_, rs, _, sl) in zip(a_rbs, ta0s, tiles)]
    y_consts = [akv[c:] + _dot(a_rb, _stack_heads(tav, first))
                for akv, a_rb, tav in zip(akvs, a_rbs, tavs)]
    next(filler, None)

    states = [state_ref[p] for p in pairs]
    ys = []
    for j in range(len(chunks)):
        mine = slice(j * N_PAIRS, (j + 1) * N_PAIRS)
        s_bs = [s.astype(BF16) for s in states]
        ys += [_dot(ra, s_b, _NT) + yc for ra, yc, s_b in zip(ras[mine], y_consts[mine], s_bs)]
        states = [s * dend_ref[slot, j * SUBLANES:j * SUBLANES + 1, sl] + _dot(s_b, m) + cm
                  for s, s_b, m, cm, sl in zip(states, s_bs, m_offs[mine], consts[mine], sls)]
    for p, s in enumerate(states):
        state_ref[p] = s

    means = [_dot(_hilo_cols(y), avg_2) for y in ys]
    ycs = [y - m for y, m in zip(ys, means)]
    stats = [_dot(jnp.concatenate([(yc * yc).astype(BF16), opb("rkk", rs, sl)], axis=1), stat_bd)
             for yc, (_, rs, _, sl) in zip(ycs, tiles)]
    for yc, st, (_, rs, _, sl) in zip(ycs, stats, tiles):
        o_ref[rs, sl] = ((yc * lax.rsqrt(st[:, :LANES] + GN_EPS) * gn_g[:, sl] + gn_b[:, sl]
                          + st[:, LANES:] * opf("v", rs, sl)) * opf("g", rs, sl)
                         ).astype(o_ref.dtype)


def _gmlp_stages(zg_ref, lng_ref, lnb_ref, wst_ref, bs_ref, o_ref):
    n_tiles = GMLP_WIDTH // LANES
    cols = lambda base, p: slice(base + p * LANES, base + (p + 1) * LANES)
    rows = zg_ref.shape[0]
    chunks = [slice(j * CHUNK, (j + 1) * CHUNK) for j in range(rows // CHUNK)]
    vs = [zg_ref[:, cols(GMLP_WIDTH, p)] for p in range(n_tiles)]
    m = sum(jnp.sum(t, axis=-1, keepdims=True) for t in vs) * (1.0 / GMLP_WIDTH)
    vcs = [t - m for t in vs]
    var = sum(jnp.sum(t * t, axis=-1, keepdims=True) for t in vcs) * (1.0 / GMLP_WIDTH)
    inv = lax.rsqrt(var + LN_EPS)
    yield
    first = _iota2((CHUNK, LANES), 1) < HEAD
    for p in range(n_tiles):
        vn = (vcs[p] * inv * lng_ref[:, cols(0, p)] + lnb_ref[:, cols(0, p)]).astype(BF16)
        u = zg_ref[:, cols(0, p)]
        for rs in chunks:
            mixed = _dot(wst_ref[p], _stack_heads(vn[rs], first))
            o_ref[rs, cols(RWKV_WIDTH, p)] = (
                u[rs] * (mixed + bs_ref[:, cols(0, p)])).astype(o_ref.dtype)
        yield


def _interleaved(*stage_generators):
    for _ in itertools.zip_longest(*stage_generators):
        yield


def _mixer_kernel(zr_ref, zg_ref, w0_ref, wdec_ref, a0_ref, waaa_ref, wgate_ref, kk_ref,
                  ka_ref, rk_ref, gng_ref, gnb_ref, lng_ref, lnb_ref, ws_ref, bs_ref,
                  o_ref, state_ref, wst_ref, opb_ref, opf_ref, dend_ref):
    s = pl.program_id(0)

    @pl.when(s == 0)
    def _():
        state_ref[...] = jnp.zeros_like(state_ref)
        opb_ref[...] = jnp.zeros_like(opb_ref)
        opf_ref[...] = jnp.zeros_like(opf_ref)
        dend_ref[...] = jnp.zeros_like(dend_ref)
        causal = _iota2((CHUNK, CHUNK), 0) >= _iota2((CHUNK, CHUNK), 1)
        for h in range(ws_ref.shape[0]):
            wst_ref[h // 2, :, (h % 2) * CHUNK:(h % 2 + 1) * CHUNK] = (
                jnp.where(causal, ws_ref[h], 0.0).astype(BF16))

    prepare = _rwkv_prepare_stages(zr_ref, w0_ref, wdec_ref, a0_ref, waaa_ref, wgate_ref, kk_ref,
                                   ka_ref, rk_ref, opb_ref, opf_ref, dend_ref, s % 2)
    gmlp = _gmlp_stages(zg_ref, lng_ref, lnb_ref, wst_ref, bs_ref, o_ref)
    filler = _interleaved(prepare, gmlp)
    _rwkv_chunks(opb_ref, opf_ref, dend_ref, (s + 1) % 2, gng_ref[...], gnb_ref[...],
                 state_ref, o_ref, filler)
    for _ in filler:
        pass


def _mixer(zr, zg, w0, wdec, a0, waaa, wgate, k_k, k_a, r_k, gn_g, gn_b, ln_g, ln_b, w_s, b_s):
    rows = MIX_CHUNKS * CHUNK
    n = zr.shape[0] // rows
    n_heads = w_s.shape[0]

    def full(arr):
        nd = arr.ndim
        return pl.BlockSpec(arr.shape, lambda s: (0,) * nd)

    params = (w0, wdec, a0, waaa, wgate, k_k, k_a, r_k, gn_g, gn_b, ln_g, ln_b, w_s, b_s)
    return pl.pallas_call(
        _mixer_kernel,
        out_shape=jax.ShapeDtypeStruct((n * rows, D_MODEL), BF16),
        grid=(n + 1,),
        in_specs=[pl.BlockSpec((rows, RWKV_COLS), lambda s: (jnp.minimum(s, n - 1), 0)),
                  pl.BlockSpec((rows, 2 * GMLP_WIDTH), lambda s: (jnp.maximum(s - 1, 0), 0))]
                 + [full(a) for a in params],
        out_specs=pl.BlockSpec((rows, D_MODEL), lambda s: (jnp.maximum(s - 1, 0), 0)),
        scratch_shapes=[
            pltpu.VMEM((N_PAIRS, LANES, LANES), F32),
            pltpu.VMEM((n_heads // 2, CHUNK, 2 * CHUNK), BF16),
            pltpu.VMEM((2, len(_OPS_B), rows, RWKV_WIDTH), BF16),
            pltpu.VMEM((2, len(_OPS_F), rows, RWKV_WIDTH), F32),
            pltpu.VMEM((2, rows // RWKV_CHUNK * SUBLANES, RWKV_WIDTH), F32),
        ],
        compiler_params=pltpu.CompilerParams(
            dimension_semantics=("arbitrary",), vmem_limit_bytes=VMEM_LIMIT),
        name="mixer",
    )(zr, zg, *params)


def kernel(x, p, norm_ffn1, ffn1_w1, ffn1_w3, ffn1_w2, norm_mix, w_in, shift_mu, rwkv_w0,
           rwkv_w_decay, rwkv_a0, rwkv_w_aaa, rwkv_w_gate, rwkv_k_k, rwkv_k_a, rwkv_r_k,
           rwkv_gn_g, rwkv_gn_b, sgu_ln_g, sgu_ln_b, sgu_w_s, sgu_b_s, w_out, norm_ffn2,
           ffn2_w1, ffn2_w3, ffn2_w2, norm_ple, w_ple_gate, w_ple, norm_final):
    assert x.shape[0] == 1 and p.shape[0] == 1, "one batch row and one layer, as the problem states"
    row = lambda a: a.reshape(1, -1)
    bf = lambda a: a.astype(BF16)
    h, zr, zg = _pre_mixer(
        x[0], [row(norm_ffn1[0]), bf(ffn1_w1[0]), bf(ffn1_w3[0]), bf(ffn1_w2[0]),
               row(norm_mix[0]), bf(w_in[0]), row(shift_mu[0])])
    lora = rwkv_w_decay.shape[1]
    pad = jnp.zeros((LANES - lora, RWKV_WIDTH), F32)
    wdec = jnp.concatenate([rwkv_w_decay[0], pad], axis=0)
    waaa = jnp.concatenate([pad, rwkv_w_aaa[0]], axis=0)
    bias = jnp.repeat(sgu_b_s[0].T, HEAD, axis=1)
    y = _mixer(zr, zg, row(rwkv_w0[0]), bf(wdec), row(rwkv_a0[0]), bf(waaa),
               bf(rwkv_w_gate[0]), row(rwkv_k_k[0]), row(rwkv_k_a[0]), row(rwkv_r_k[0]),
               row(rwkv_gn_g[0]), row(rwkv_gn_b[0]), row(sgu_ln_g[0]), row(sgu_ln_b[0]),
               sgu_w_s[0], bias)
    out = _post_mixer(
        h, y, p[0, 0],
        [bf(w_out[0]), row(norm_ffn2[0]), bf(ffn2_w1[0]), bf(ffn2_w3[0]), bf(ffn2_w2[0]),
         row(norm_ple[0]), bf(w_ple_gate[0]), bf(w_ple[0]), row(norm_final)])
    return out[None]
```

```python
import itertools

import jax
import jax.numpy as jnp
from jax import lax
from jax.experimental import pallas as pl
from jax.experimental.pallas import tpu as pltpu

F32 = jnp.float32
BF16 = jnp.bfloat16

D_MODEL = 1024
D_FF = 2816
RWKV_WIDTH = 512
RWKV_COLS = 1792
GMLP_WIDTH = 512
HEAD = 64
LANES = 128
SUBLANES = 8
N_PAIRS = RWKV_WIDTH // LANES
CHUNK = 128
RWKV_CHUNK = 64
MIX_CHUNKS = 4
RMS_EPS = 1e-6
LN_EPS = 1e-5
GN_EPS = 64e-5
VMEM_LIMIT = 56 * 1024 * 1024

_NN = (((1,), (0,)), ((), ()))
_NT = (((1,), (1,)), ((), ()))
_TN = (((0,), (0,)), ((), ()))


def _dot(a, b, dn=_NN):
    return lax.dot_general(a, b, dn, preferred_element_type=F32)


def _split2(x):
    hi = x.astype(BF16)
    lo = (x - hi.astype(F32)).astype(BF16)
    return hi, lo


def _hilo_cols(x):
    return jnp.concatenate(_split2(x), axis=1)


def _hilo_rows(x):
    return jnp.concatenate(_split2(x), axis=0)


def _rms(x, g):
    ms = jnp.mean(x * x, axis=-1, keepdims=True)
    return x * lax.rsqrt(ms + RMS_EPS) * g


FF_CHUNK = 256


def _swiglu_hidden(xn, w1_ref, w3_ref, hm_ref, filler):
    for j in range(hm_ref.shape[1] // FF_CHUNK):
        cols = slice(j * FF_CHUNK, (j + 1) * FF_CHUNK)
        a = _dot(xn, w1_ref[:, cols])
        b = _dot(xn, w3_ref[:, cols])
        hm_ref[:, cols] = (a * jax.nn.sigmoid(a) * b).astype(BF16)
        next(filler, None)


def _rms_stages(x_ref, g_ref, out_ref, slot, rows_per_stage=64):
    for start in range(0, x_ref.shape[0], rows_per_stage):
        rs = slice(start, start + rows_per_stage)
        out_ref[slot, rs, :] = _rms(x_ref[rs, :], g_ref[...]).astype(BF16)
        yield


def _pre_mixer_kernel(x_ref, xnext_ref, g1_ref, w1_ref, w3_ref, w2_ref, gm_ref, win_ref, mu_ref,
                      h_ref, zr_ref, zg_ref, hm_ref, xn1_ref, xn2_ref, carry_ref):
    i = pl.program_id(0)
    last = pl.num_programs(0) - 1

    def project(slot):
        z = _dot(xn2_ref[slot], win_ref[...])
        rows = z.shape[0]
        first_row = _iota2((rows, LANES), 0) == 0
        for col in range(0, RWKV_COLS, LANES):
            cols = slice(col, col + LANES)
            zc = z[:, cols]
            zp = jnp.where(first_row, carry_ref[:, cols], pltpu.roll(zc, 1, 0))
            carry_ref[:, cols] = zc[rows - 1:rows, :]
            zr_ref[:, cols] = zc + (zp - zc) * mu_ref[:, cols]
        zg = z[:, RWKV_COLS:]
        zg_ref[...] = 0.5 * zg * (1.0 + lax.erf(zg * 0.7071067811865476))

    @pl.when(i == 0)
    def _():
        xn1_ref[0] = _rms(x_ref[...], g1_ref[...]).astype(BF16)
        xn2_ref[1] = jnp.zeros(xn2_ref.shape[1:], BF16)
        carry_ref[...] = jnp.zeros_like(carry_ref)

    @pl.when(i < last)
    def _():
        project((i + 1) % 2)
        filler = _rms_stages(xnext_ref, g1_ref, xn1_ref, (i + 1) % 2)
        _swiglu_hidden(xn1_ref[i % 2], w1_ref, w3_ref, hm_ref, filler)
        for _ in filler:
            pass
        acc = _dot(hm_ref[...], w2_ref[...])
        h = x_ref[...] + 0.5 * acc
        h_ref[...] = h
        xn2_ref[i % 2] = _rms(h, gm_ref[...]).astype(BF16)

    @pl.when(i == last)
    def _():
        project((i + 1) % 2)


def _post_mixer_kernel(h_ref, y_ref, p_ref, wout_ref, g2_ref, w1_ref, w3_ref, w2_ref,
                       gp_ref, wg_ref, wp_ref, gf_ref, o_ref, hm_ref):
    h = h_ref[...] + _dot(y_ref[...], wout_ref[...])
    _swiglu_hidden(_rms(h, g2_ref[...]).astype(BF16), w1_ref, w3_ref, hm_ref, iter(()))
    h = h + 0.5 * _dot(hm_ref[...], w2_ref[...])
    gate = jax.nn.sigmoid(_dot(_rms(h, gp_ref[...]).astype(BF16), wg_ref[...]))
    e = _dot(p_ref[...].astype(BF16), wp_ref[...])
    o_ref[...] = _rms(h + gate * e, gf_ref[...])


def _rows(tm, width, tile_of_step=lambda i: i):
    return pl.BlockSpec((tm, width), lambda i: (tile_of_step(i), 0))


def _resident(arr):
    nd = arr.ndim
    return pl.BlockSpec(arr.shape, lambda i: (0,) * nd, pipeline_mode=pl.Buffered(1))


def _pre_mixer(x, params, *, tm=512):
    t, d = x.shape
    n = t // tm
    this = lambda i: jnp.minimum(i, n - 1)
    nxt = lambda i: jnp.minimum(i + 1, n - 1)
    prev = lambda i: jnp.maximum(i - 1, 0)
    return pl.pallas_call(
        _pre_mixer_kernel,
        out_shape=(jax.ShapeDtypeStruct((t, d), F32),
                   jax.ShapeDtypeStruct((t, RWKV_COLS), F32),
                   jax.ShapeDtypeStruct((t, 2 * GMLP_WIDTH), F32)),
        grid=(n + 1,),
        in_specs=[_rows(tm, d, this), _rows(tm, d, nxt)] + [_resident(a) for a in params],
        out_specs=(_rows(tm, d, this), _rows(tm, RWKV_COLS, prev),
                   _rows(tm, 2 * GMLP_WIDTH, prev)),
        scratch_shapes=[pltpu.VMEM((tm, D_FF), BF16),
                        pltpu.VMEM((2, tm, d), BF16),
                        pltpu.VMEM((2, tm, d), BF16),
                        pltpu.VMEM((1, RWKV_COLS), F32)],
        compiler_params=pltpu.CompilerParams(
            dimension_semantics=("arbitrary",), vmem_limit_bytes=VMEM_LIMIT),
        name="pre_mixer",
    )(x, x, *params)


def _post_mixer(h, y, p, params, *, tm=1024):
    t, d = h.shape
    return pl.pallas_call(
        _post_mixer_kernel,
        out_shape=jax.ShapeDtypeStruct((t, d), F32),
        grid=(t // tm,),
        in_specs=[_rows(tm, a.shape[1]) for a in (h, y, p)] + [_resident(a) for a in params],
        out_specs=_rows(tm, d),
        scratch_shapes=[pltpu.VMEM((tm, D_FF), BF16)],
        compiler_params=pltpu.CompilerParams(
            dimension_semantics=("arbitrary",), vmem_limit_bytes=VMEM_LIMIT),
        name="post_mixer",
    )(h, y, p, *params)


def _iota2(shape, dim):
    return lax.broadcasted_iota(jnp.int32, shape, dim)


def _stack_heads(x, first):
    zero = jnp.zeros_like(x)
    return jnp.concatenate([jnp.where(first, x, zero), jnp.where(first, zero, x)], axis=0)


def _unit_lower_inverses(mats, c, filler):
    row = _iota2((c, 2 * c), 0)
    lane = _iota2((c, 2 * c), 1)
    col = lane & (c - 1)
    left = lane < c
    eye = jnp.where(row == col, 1.0, 0.0)

    def same_block(b):
        s = b.bit_length() - 1
        return (row >> s) == (col >> s)

    def bd(x):
        zero = jnp.zeros_like(x)
        return jnp.concatenate([jnp.where(left, x, zero), jnp.where(left, zero, x)], axis=0)

    ts = [(eye + jnp.where(same_block(2), a, 0.0)).astype(BF16) for a in mats]
    b = 2
    while b < c:
        off = same_block(2 * b) & jnp.logical_not(same_block(b))
        tas = [_dot(t, bd(jnp.where(off, a, 0.0))).astype(BF16) for t, a in zip(ts, mats)]
        next(filler, None)
        ts = [(t + _dot(ta, bd(t))).astype(BF16) for t, ta in zip(ts, tas)]
        next(filler, None)
        b *= 2
    return ts


_OPS_B = ("rt", "at", "bt", "kt", "r0", "a0", "bh", "kh", "vb", "rkk")
_OPS_F = ("v", "g")


def _rwkv_prepare_stages(zr_ref, w0_ref, wdec_ref, a0_ref, waaa_ref,
                         wgate_ref, kk_ref, ka_ref, rk_ref, opb_ref, opf_ref, dend_ref, slot):
    c = RWKV_CHUNK
    rows = zr_ref.shape[0]
    chunks = [slice(j * c, (j + 1) * c) for j in range(rows // c)]
    shifted = lambda col: zr_ref[:, col:col + LANES]

    w = RWKV_WIDTH
    zwa = shifted(3 * w)
    zwa_tanh = jnp.tanh(zwa).astype(BF16)
    zwa_b = zwa.astype(BF16)
    zg_sig = jax.nn.sigmoid(shifted(3 * w + LANES)).astype(BF16)
    blockdiag = (_iota2((LANES, LANES), 0) >= HEAD) == (_iota2((LANES, LANES), 1) >= HEAD)
    ones_bd = blockdiag.astype(BF16)
    row2 = _iota2((c, 2 * c), 0)
    col2 = _iota2((c, 2 * c), 1) & (c - 1)
    tril_2 = (row2 >= col2).astype(BF16)
    yield
    for p in range(N_PAIRS):
        sl = slice(p * LANES, (p + 1) * LANES)
        r, k, v = shifted(p * LANES), shifted(w + p * LANES), shifted(2 * w + p * LANES)
        lw = w0_ref[:, sl] + _dot(zwa_tanh, wdec_ref[:, sl])
        logd = -0.6065306597126334 * jax.nn.sigmoid(lw)
        a = jax.nn.sigmoid(a0_ref[:, sl] + _dot(zwa_b, waaa_ref[:, sl]))
        g = _dot(zg_sig, wgate_ref[:, sl])
        kk = k * kk_ref[:, sl]
        n2 = _dot((kk * kk).astype(BF16), ones_bd)
        kkn = kk * lax.rsqrt(jnp.maximum(n2, 1e-24))
        k2 = k * (1.0 + (a - 1.0) * ka_ref[:, sl])
        bvec = kkn * a
        opb_ref[slot, _OPS_B.index("rkk"), :, sl] = (r * k2 * rk_ref[:, sl]).astype(BF16)
        opf_ref[slot, _OPS_F.index("v"), :, sl] = v
        opf_ref[slot, _OPS_F.index("g"), :, sl] = g
        yield
        for j, rs in enumerate(chunks):
            lcum = _dot(tril_2, _hilo_rows(logd[rs]))
            lmid = lcum[c // 2 - 1:c // 2, :]
            lend = lcum[c - 1:c, :]
            lexc = lcum - logd[rs]
            e_mid_from = jnp.exp(lmid - lcum)
            e_end = jnp.exp(lend - lcum)
            ops = dict(
                rt=r[rs] * jnp.exp(lcum - lmid), at=-kkn[rs] * jnp.exp(lexc - lmid),
                bt=bvec[rs] * e_mid_from, kt=k2[rs] * e_mid_from,
                r0=r[rs] * jnp.exp(lcum), a0=-kkn[rs] * jnp.exp(lexc),
                bh=bvec[rs] * e_end, kh=k2[rs] * e_end, vb=v[rs])
            for name, value in ops.items():
                opb_ref[slot, _OPS_B.index(name), rs, sl] = value.astype(BF16)
            dend_ref[slot, j * SUBLANES:(j + 1) * SUBLANES, sl] = jnp.broadcast_to(
                jnp.exp(lend), (SUBLANES, LANES))
        yield


def _rwkv_chunks(opb_ref, opf_ref, dend_ref, slot, gn_g, gn_b, state_ref, o_ref, filler):
    c = RWKV_CHUNK
    row2 = _iota2((c, 2 * c), 0)
    col2 = _iota2((c, 2 * c), 1) & (c - 1)
    incl2 = row2 >= col2
    strict2 = row2 > col2
    first = _iota2((c, LANES), 1) < HEAD
    blockdiag = (_iota2((LANES, LANES), 0) >= HEAD) == (_iota2((LANES, LANES), 1) >= HEAD)
    ones_bd = blockdiag.astype(BF16)
    avg_bd = (blockdiag.astype(F32) * (1.0 / HEAD)).astype(BF16)
    zero_bd = jnp.zeros((LANES, LANES), BF16)
    avg_2 = jnp.concatenate([avg_bd, avg_bd], axis=0)
    stat_bd = jnp.concatenate([jnp.concatenate([avg_bd, zero_bd], axis=1),
                               jnp.concatenate([zero_bd, ones_bd], axis=1)], axis=0)
    pairs = range(N_PAIRS)
    sls = [slice(p * LANES, (p + 1) * LANES) for p in pairs]
    chunks = [slice(j * c, (j + 1) * c) for j in range(opb_ref.shape[2] // c)]
    tiles = [(j, rs, p, sl) for j, rs in enumerate(chunks) for p, sl in enumerate(sls)]
    opb = lambda name, rs, sl: opb_ref[slot, _OPS_B.index(name), rs, sl]
    opf = lambda name, rs, sl: opf_ref[slot, _OPS_F.index(name), rs, sl]

    grams = [
        _dot(jnp.concatenate([opb("at", rs, sl), opb("rt", rs, sl)], axis=0),
             jnp.concatenate([_stack_heads(opb("bt", rs, sl), first),
                              _stack_heads(opb("kt", rs, sl), first)], axis=0), _NT)
        for _, rs, _, sl in tiles]
    a_abs = [jnp.where(strict2, gm[:c, :2 * c], 0.0).astype(BF16) for gm in grams]
    t_invs = _unit_lower_inverses(a_abs, c, filler)
    a_rbs = [jnp.where(incl2, gm[c:, :2 * c], 0.0).astype(BF16) for gm in grams]
    akvs = [
        _dot(jnp.concatenate([jnp.where(strict2, gm[:c, 2 * c:], 0.0),
                              jnp.where(incl2, gm[c:, 2 * c:], 0.0)], axis=0).astype(BF16),
             _stack_heads(opb("vb", rs, sl), first))
        for gm, (_, rs, _, sl) in zip(grams, tiles)]

    ta0s = [_dot(t, _stack_heads(opb("a0", rs, sl), first)).astype(BF16)
            for t, (_, rs, _, sl) in zip(t_invs, tiles)]
    tavs = [_dot(t, _stack_heads(akv[:c].astype(BF16), first)).astype(BF16)
            for t, akv in zip(t_invs, akvs)]
    next(filler, None)
    m_offs = [jnp.where(blockdiag, _dot(ta0, opb("bh", rs, sl), _TN), 0.0).astype(BF16)
              for ta0, (_, rs, _, sl) in zip(ta0s, tiles)]
    consts = [jnp.where(blockdiag,
                        _dot(jnp.concatenate([tav, opb("vb", rs, sl)], axis=0),
                             jnp.concatenate([opb("bh", rs, sl), opb("kh", rs, sl)], axis=0), _TN),
                        0.0)
              for tav, (_, rs, _, sl) in zip(tavs, tiles)]
    next(filler, None)
    ras = [(opb("r0", rs, sl).astype(F32) + _dot(a_rb, _stack_heads(ta0, first))).astype(BF16)
           for a_rb, ta0, (_, rs, _, sl) in zip(a_rbs, ta0s, tiles)]
    y_consts = [akv[c:] + _dot(a_rb, _stack_heads(tav, first))
                for akv, a_rb, tav in zip(akvs, a_rbs, tavs)]
    next(filler, None)

    states = [state_ref[p] for p in pairs]
    ys = []
    for j in range(len(chunks)):
        mine = slice(j * N_PAIRS, (j + 1) * N_PAIRS)
        s_bs = [s.astype(BF16) for s in states]
        ys += [_dot(ra, s_b, _NT) + yc for ra, yc, s_b in zip(ras[mine], y_consts[mine], s_bs)]
        states = [s * dend_ref[slot, j * SUBLANES:j * SUBLANES + 1, sl] + _dot(s_b, m) + cm
                  for s, s_b, m, cm, sl in zip(states, s_bs, m_offs[mine], consts[mine], sls)]
    for p, s in enumerate(states):
        state_ref[p] = s

    means = [_dot(_hilo_cols(y), avg_2) for y in ys]
    ycs = [y - m for y, m in zip(ys, means)]
    stats = [_dot(jnp.concatenate([(yc * yc).astype(BF16), opb("rkk", rs, sl)], axis=1), stat_bd)
             for yc, (_, rs, _, sl) in zip(ycs, tiles)]
    for yc, st, (_, rs, _, sl) in zip(ycs, stats, tiles):
        o_ref[rs, sl] = ((yc * lax.rsqrt(st[:, :LANES] + GN_EPS) * gn_g[:, sl] + gn_b[:, sl]
                          + st[:, LANES:] * opf("v", rs, sl)) * opf("g", rs, sl)
                         ).astype(o_ref.dtype)


def _gmlp_stages(zg_ref, lng_ref, lnb_ref, wst_ref, bs_ref, o_ref):
    n_tiles = GMLP_WIDTH // LANES
    cols = lambda base, p: slice(base + p * LANES, base + (p + 1) * LANES)
    rows = zg_ref.shape[0]
    chunks = [slice(j * CHUNK, (j + 1) * CHUNK) for j in range(rows // CHUNK)]
    vs = [zg_ref[:, cols(GMLP_WIDTH, p)] for p in range(n_tiles)]
    m = sum(jnp.sum(t, axis=-1, keepdims=True) for t in vs) * (1.0 / GMLP_WIDTH)
    vcs = [t - m for t in vs]
    var = sum(jnp.sum(t * t, axis=-1, keepdims=True) for t in vcs) * (1.0 / GMLP_WIDTH)
    inv = lax.rsqrt(var + LN_EPS)
    yield
    first = _iota2((CHUNK, LANES), 1) < HEAD
    for p in range(n_tiles):
        vn = (vcs[p] * inv * lng_ref[:, cols(0, p)] + lnb_ref[:, cols(0, p)]).astype(BF16)
        u = zg_ref[:, cols(0, p)]
        for rs in chunks:
            mixed = _dot(wst_ref[p], _stack_heads(vn[rs], first))
            o_ref[rs, cols(RWKV_WIDTH, p)] = (
                u[rs] * (mixed + bs_ref[:, cols(0, p)])).astype(o_ref.dtype)
        yield


def _interleaved(*stage_generators):
    for _ in itertools.zip_longest(*stage_generators):
        yield


def _mixer_kernel(zr_ref, zg_ref, w0_ref, wdec_ref, a0_ref, waaa_ref, wgate_ref, kk_ref,
                  ka_ref, rk_ref, gng_ref, gnb_ref, lng_ref, lnb_ref, ws_ref, bs_ref,
                  o_ref, state_ref, wst_ref, opb_ref, opf_ref, dend_ref):
    s = pl.program_id(0)

    @pl.when(s == 0)
    def _():
        state_ref[...] = jnp.zeros_like(state_ref)
        opb_ref[...] = jnp.zeros_like(opb_ref)
        opf_ref[...] = jnp.zeros_like(opf_ref)
        dend_ref[...] = jnp.zeros_like(dend_ref)
        causal = _iota2((CHUNK, CHUNK), 0) >= _iota2((CHUNK, CHUNK), 1)
        for h in range(ws_ref.shape[0]):
            wst_ref[h // 2, :, (h % 2) * CHUNK:(h % 2 + 1) * CHUNK] = (
                jnp.where(causal, ws_ref[h], 0.0).astype(BF16))

    prepare = _rwkv_prepare_stages(zr_ref, w0_ref, wdec_ref, a0_ref, waaa_ref, wgate_ref, kk_ref,
                                   ka_ref, rk_ref, opb_ref, opf_ref, dend_ref, s % 2)
    gmlp = _gmlp_stages(zg_ref, lng_ref, lnb_ref, wst_ref, bs_ref, o_ref)
    filler = _interleaved(prepare, gmlp)
    _rwkv_chunks(opb_ref, opf_ref, dend_ref, (s + 1) % 2, gng_ref[...], gnb_ref[...],
                 state_ref, o_ref, filler)
    for _ in filler:
        pass


def _mixer(zr, zg, w0, wdec, a0, waaa, wgate, k_k, k_a, r_k, gn_g, gn_b, ln_g, ln_b, w_s, b_s):
    rows = MIX_CHUNKS * CHUNK
    n = zr.shape[0] // rows
    n_heads = w_s.shape[0]

    def full(arr):
        nd = arr.ndim
        return pl.BlockSpec(arr.shape, lambda s: (0,) * nd)

    params = (w0, wdec, a0, waaa, wgate, k_k, k_a, r_k, gn_g, gn_b, ln_g, ln_b, w_s, b_s)
    return pl.pallas_call(
        _mixer_kernel,
        out_shape=jax.ShapeDtypeStruct((n * rows, D_MODEL), BF16),
        grid=(n + 1,),
        in_specs=[pl.BlockSpec((rows, RWKV_COLS), lambda s: (jnp.minimum(s, n - 1), 0)),
                  pl.BlockSpec((rows, 2 * GMLP_WIDTH), lambda s: (jnp.maximum(s - 1, 0), 0))]
                 + [full(a) for a in params],
        out_specs=pl.BlockSpec((rows, D_MODEL), lambda s: (jnp.maximum(s - 1, 0), 0)),
        scratch_shapes=[
            pltpu.VMEM((N_PAIRS, LANES, LANES), F32),
            pltpu.VMEM((n_heads // 2, CHUNK, 2 * CHUNK), BF16),
            pltpu.VMEM((2, len(_OPS_B), rows, RWKV_WIDTH), BF16),
            pltpu.VMEM((2, len(_OPS_F), rows, RWKV_WIDTH), F32),
            pltpu.VMEM((2, rows // RWKV_CHUNK * SUBLANES, RWKV_WIDTH), F32),
        ],
        compiler_params=pltpu.CompilerParams(
            dimension_semantics=("arbitrary",), vmem_limit_bytes=VMEM_LIMIT),
        name="mixer",
    )(zr, zg, *params)


def kernel(x, p, norm_ffn1, ffn1_w1, ffn1_w3, ffn1_w2, norm_mix, w_in, shift_mu, rwkv_w0,
           rwkv_w_decay, rwkv_a0, rwkv_w_aaa, rwkv_w_gate, rwkv_k_k, rwkv_k_a, rwkv_r_k,
           rwkv_gn_g, rwkv_gn_b, sgu_ln_g, sgu_ln_b, sgu_w_s, sgu_b_s, w_out, norm_ffn2,
           ffn2_w1, ffn2_w3, ffn2_w2, norm_ple, w_ple_gate, w_ple, norm_final):
    assert x.shape[0] == 1 and p.shape[0] == 1, "one batch row and one layer, as the problem states"
    row = lambda a: a.reshape(1, -1)
    bf = lambda a: a.astype(BF16)
    h, zr, zg = _pre_mixer(
        x[0], [row(norm_ffn1[0]), bf(ffn1_w1[0]), bf(ffn1_w3[0]), bf(ffn1_w2[0]),
               row(norm_mix[0]), bf(w_in[0]), row(shift_mu[0])])
    lora = rwkv_w_decay.shape[1]
    pad = jnp.zeros((LANES - lora, RWKV_WIDTH), F32)
    wdec = jnp.concatenate([rwkv_w_decay[0], pad], axis=0)
    waaa = jnp.concatenate([pad, rwkv_w_aaa[0]], axis=0)
    bias = jnp.repeat(sgu_b_s[0].T, HEAD, axis=1)
    y = _mixer(zr, zg, row(rwkv_w0[0]), bf(wdec), row(rwkv_a0[0]), bf(waaa),
               bf(rwkv_w_gate[0]), row(rwkv_k_k[0]), row(rwkv_k_a[0]), row(rwkv_r_k[0]),
               row(rwkv_gn_g[0]), row(rwkv_gn_b[0]), row(sgu_ln_g[0]), row(sgu_ln_b[0]),
               sgu_w_s[0], bias)
    out = _post_mixer(
        h, y, p[0, 0],
        [bf(w_out[0]), row(norm_ffn2[0]), bf(ffn2_w1[0]), bf(ffn2_w3[0]), bf(ffn2_w2[0]),
         row(norm_ple[0]), bf(w_ple_gate[0]), bf(w_ple[0]), row(norm_final)])
    return out[None]
```

```python
import itertools

import jax
import jax.numpy as jnp
from jax import lax
from jax.experimental import pallas as pl
from jax.experimental.pallas import tpu as pltpu

F32 = jnp.float32
BF16 = jnp.bfloat16

D_MODEL = 1024
D_FF = 2816
RWKV_WIDTH = 512
RWKV_COLS = 1792
GMLP_WIDTH = 512
HEAD = 64
LANES = 128
SUBLANES = 8
N_PAIRS = RWKV_WIDTH // LANES
CHUNK = 128
RWKV_CHUNK = 64
MIX_CHUNKS = 4
RMS_EPS = 1e-6
LN_EPS = 1e-5
GN_EPS = 64e-5
VMEM_LIMIT = 56 * 1024 * 1024

_NN = (((1,), (0,)), ((), ()))
_NT = (((1,), (1,)), ((), ()))
_TN = (((0,), (0,)), ((), ()))


def _dot(a, b, dn=_NN):
    return lax.dot_general(a, b, dn, preferred_element_type=F32)


def _split2(x):
    hi = x.astype(BF16)
    lo = (x - hi.astype(F32)).astype(BF16)
    return hi, lo


def _hilo_cols(x):
    return jnp.concatenate(_split2(x), axis=1)


def _hilo_rows(x):
    return jnp.concatenate(_split2(x), axis=0)


def _rms(x, g):
    ms = jnp.mean(x * x, axis=-1, keepdims=True)
    return x * lax.rsqrt(ms + RMS_EPS) * g


FF_CHUNK = 256


def _swiglu_hidden(xn, w1_ref, w3_ref, hm_ref, filler):
    for j in range(hm_ref.shape[1] // FF_CHUNK):
        cols = slice(j * FF_CHUNK, (j + 1) * FF_CHUNK)
        a = _dot(xn, w1_ref[:, cols])
        b = _dot(xn, w3_ref[:, cols])
        hm_ref[:, cols] = (a * jax.nn.sigmoid(a) * b).astype(BF16)
        next(filler, None)


def _rms_stages(x_ref, g_ref, out_ref, slot, rows_per_stage=64):
    for start in range(0, x_ref.shape[0], rows_per_stage):
        rs = slice(start, start + rows_per_stage)
        out_ref[slot, rs, :] = _rms(x_ref[rs, :], g_ref[...]).astype(BF16)
        yield


def _pre_mixer_kernel(x_ref, xnext_ref, g1_ref, w1_ref, w3_ref, w2_ref, gm_ref, win_ref, mu_ref,
                      h_ref, zr_ref, zg_ref, hm_ref, xn1_ref, xn2_ref, carry_ref):
    i = pl.program_id(0)
    last = pl.num_programs(0) - 1

    def project(slot):
        z = _dot(xn2_ref[slot], win_ref[...])
        rows = z.shape[0]
        first_row = _iota2((rows, LANES), 0) == 0
        for col in range(0, RWKV_COLS, LANES):
            cols = slice(col, col + LANES)
            zc = z[:, cols]
            zp = jnp.where(first_row, carry_ref[:, cols], pltpu.roll(zc, 1, 0))
            carry_ref[:, cols] = zc[rows - 1:rows, :]
            zr_ref[:, cols] = zc + (zp - zc) * mu_ref[:, cols]
        zg = z[:, RWKV_COLS:]
        zg_ref[...] = 0.5 * zg * (1.0 + lax.erf(zg * 0.7071067811865476))

    @pl.when(i == 0)
    def _():
        xn1_ref[0] = _rms(x_ref[...], g1_ref[...]).astype(BF16)
        xn2_ref[1] = jnp.zeros(xn2_ref.shape[1:], BF16)
        carry_ref[...] = jnp.zeros_like(carry_ref)

    @pl.when(i < last)
    def _():
        project((i + 1) % 2)
        filler = _rms_stages(xnext_ref, g1_ref, xn1_ref, (i + 1) % 2)
        _swiglu_hidden(xn1_ref[i % 2], w1_ref, w3_ref, hm_ref, filler)
        for _ in filler:
            pass
        acc = _dot(hm_ref[...], w2_ref[...])
        h = x_ref[...] + 0.5 * acc
        h_ref[...] = h
        xn2_ref[i % 2] = _rms(h, gm_ref[...]).astype(BF16)

    @pl.when(i == last)
    def _():
        project((i + 1) % 2)


def _post_mixer_kernel(h_ref, y_ref, p_ref, wout_ref, g2_ref, w1_ref, w3_ref, w2_ref,
                       gp_ref, wg_ref, wp_ref, gf_ref, o_ref, hm_ref):
    h = h_ref[...] + _dot(y_ref[...], wout_ref[...])
    _swiglu_hidden(_rms(h, g2_ref[...]).astype(BF16), w1_ref, w3_ref, hm_ref, iter(()))
    h = h + 0.5 * _dot(hm_ref[...], w2_ref[...])
    gate = jax.nn.sigmoid(_dot(_rms(h, gp_ref[...]).astype(BF16), wg_ref[...]))
    e = _dot(p_ref[...].astype(BF16), wp_ref[...])
    o_ref[...] = _rms(h + gate * e, gf_ref[...])


def _rows(tm, width, tile_of_step=lambda i: i):
    return pl.BlockSpec((tm, width), lambda i: (tile_of_step(i), 0))


def _resident(arr):
    nd = arr.ndim
    return pl.BlockSpec(arr.shape, lambda i: (0,) * nd, pipeline_mode=pl.Buffered(1))


def _pre_mixer(x, params, *, tm=512):
    t, d = x.shape
    n = t // tm
    this = lambda i: jnp.minimum(i, n - 1)
    nxt = lambda i: jnp.minimum(i + 1, n - 1)
    prev = lambda i: jnp.maximum(i - 1, 0)
    return pl.pallas_call(
        _pre_mixer_kernel,
        out_shape=(jax.ShapeDtypeStruct((t, d), F32),
                   jax.ShapeDtypeStruct((t, RWKV_COLS), F32),
                   jax.ShapeDtypeStruct((t, 2 * GMLP_WIDTH), F32)),
        grid=(n + 1,),
        in_specs=[_rows(tm, d, this), _rows(tm, d, nxt)] + [_resident(a) for a in params],
        out_specs=(_rows(tm, d, this), _rows(tm, RWKV_COLS, prev),
                   _rows(tm, 2 * GMLP_WIDTH, prev)),
        scratch_shapes=[pltpu.VMEM((tm, D_FF), BF16),
                        pltpu.VMEM((2, tm, d), BF16),
                        pltpu.VMEM((2, tm, d), BF16),
                        pltpu.VMEM((1, RWKV_COLS), F32)],
        compiler_params=pltpu.CompilerParams(
            dimension_semantics=("arbitrary",), vmem_limit_bytes=VMEM_LIMIT),
        name="pre_mixer",
    )(x, x, *params)


def _post_mixer(h, y, p, params, *, tm=1024):
    t, d = h.shape
    return pl.pallas_call(
        _post_mixer_kernel,
        out_shape=jax.ShapeDtypeStruct((t, d), F32),
        grid=(t // tm,),
        in_specs=[_rows(tm, a.shape[1]) for a in (h, y, p)] + [_resident(a) for a in params],
        out_specs=_rows(tm, d),
        scratch_shapes=[pltpu.VMEM((tm, D_FF), BF16)],
        compiler_params=pltpu.CompilerParams(
            dimension_semantics=("arbitrary",), vmem_limit_bytes=VMEM_LIMIT),
        name="post_mixer",
    )(h, y, p, *params)


def _iota2(shape, dim):
    return lax.broadcasted_iota(jnp.int32, shape, dim)


def _stack_heads(x, first):
    zero = jnp.zeros_like(x)
    return jnp.concatenate([jnp.where(first, x, zero), jnp.where(first, zero, x)], axis=0)


def _unit_lower_inverses(mats, c, filler):
    row = _iota2((c, 2 * c), 0)
    lane = _iota2((c, 2 * c), 1)
    col = lane & (c - 1)
    left = lane < c
    eye = jnp.where(row == col, 1.0, 0.0)

    def same_block(b):
        s = b.bit_length() - 1
        return (row >> s) == (col >> s)

    def bd(x):
        zero = jnp.zeros_like(x)
        return jnp.concatenate([jnp.where(left, x, zero), jnp.where(left, zero, x)], axis=0)

    ts = [(eye + jnp.where(same_block(2), a, 0.0)).astype(BF16) for a in mats]
    anchor = None
    b = 2
    while b < c:
        off = same_block(2 * b) & jnp.logical_not(same_block(b))
        tas = [_dot(t, bd(jnp.where(off, a, 0.0))) for t, a in zip(ts, mats)]
        tas[0] = _anchored(tas[0], anchor)
        tas = [ta.astype(BF16) for ta in tas]
        anchor = next(filler, None)
        new = [t + _dot(ta, bd(t)) for t, ta in zip(ts, tas)]
        new[0] = _anchored(new[0], anchor)
        ts = [t.astype(BF16) for t in new]
        anchor = next(filler, None)
        b *= 2
    return ts


_OPS_B = ("rt", "at", "bt", "kt", "r0", "a0", "bh", "kh", "vb", "rkk")
_OPS_F = ("v", "g")


def _rwkv_prepare_stages(zr_ref, w0_ref, wdec_ref, a0_ref, waaa_ref,
                         wgate_ref, kk_ref, ka_ref, rk_ref, opb_ref, opf_ref, dend_ref, slot):
    c = RWKV_CHUNK
    rows = zr_ref.shape[0]
    chunks = [slice(j * c, (j + 1) * c) for j in range(rows // c)]
    shifted = lambda col: zr_ref[:, col:col + LANES]

    w = RWKV_WIDTH
    zwa = shifted(3 * w)
    zwa_tanh = jnp.tanh(zwa).astype(BF16)
    zwa_b = zwa.astype(BF16)
    zg_sig = jax.nn.sigmoid(shifted(3 * w + LANES)).astype(BF16)
    blockdiag = (_iota2((LANES, LANES), 0) >= HEAD) == (_iota2((LANES, LANES), 1) >= HEAD)
    ones_bd = blockdiag.astype(BF16)
    row2 = _iota2((c, 2 * c), 0)
    col2 = _iota2((c, 2 * c), 1) & (c - 1)
    tril_2 = (row2 >= col2).astype(BF16)
    yield
    for p in range(N_PAIRS):
        sl = slice(p * LANES, (p + 1) * LANES)
        r, k, v = shifted(p * LANES), shifted(w + p * LANES), shifted(2 * w + p * LANES)
        lw = w0_ref[:, sl] + _dot(zwa_tanh, wdec_ref[:, sl])
        logd = -0.6065306597126334 * jax.nn.sigmoid(lw)
        a = jax.nn.sigmoid(a0_ref[:, sl] + _dot(zwa_b, waaa_ref[:, sl]))
        g = _dot(zg_sig, wgate_ref[:, sl])
        kk = k * kk_ref[:, sl]
        n2 = _dot((kk * kk).astype(BF16), ones_bd)
        kkn = kk * lax.rsqrt(jnp.maximum(n2, 1e-24))
        k2 = k * (1.0 + (a - 1.0) * ka_ref[:, sl])
        bvec = kkn * a
        opb_ref[slot, _OPS_B.index("rkk"), :, sl] = (r * k2 * rk_ref[:, sl]).astype(BF16)
        opf_ref[slot, _OPS_F.index("v"), :, sl] = v
        opf_ref[slot, _OPS_F.index("g"), :, sl] = g
        yield (r * k2)[:c]
        for j, rs in enumerate(chunks):
            lcum = _dot(tril_2, _hilo_rows(logd[rs]))
            lmid = lcum[c // 2 - 1:c // 2, :]
            lend = lcum[c - 1:c, :]
            lexc = lcum - logd[rs]
            e_mid_from = jnp.exp(lmid - lcum)
            e_end = jnp.exp(lend - lcum)
            ops = dict(
                rt=r[rs] * jnp.exp(lcum - lmid), at=-kkn[rs] * jnp.exp(lexc - lmid),
                bt=bvec[rs] * e_mid_from, kt=k2[rs] * e_mid_from,
                r0=r[rs] * jnp.exp(lcum), a0=-kkn[rs] * jnp.exp(lexc),
                bh=bvec[rs] * e_end, kh=k2[rs] * e_end, vb=v[rs])
            for name, value in ops.items():
                opb_ref[slot, _OPS_B.index(name), rs, sl] = value.astype(BF16)
            dend_ref[slot, j * SUBLANES:(j + 1) * SUBLANES, sl] = jnp.broadcast_to(
                jnp.exp(lend), (SUBLANES, LANES))
        yield ops["bh"] + ops["rt"]


def _rwkv_chunks(opb_ref, opf_ref, dend_ref, slot, gn_g, gn_b, state_ref, o_ref, filler):
    c = RWKV_CHUNK
    row2 = _iota2((c, 2 * c), 0)
    col2 = _iota2((c, 2 * c), 1) & (c - 1)
    incl2 = row2 >= col2
    strict2 = row2 > col2
    first = _iota2((c, LANES), 1) < HEAD
    blockdiag = (_iota2((LANES, LANES), 0) >= HEAD) == (_iota2((LANES, LANES), 1) >= HEAD)
    ones_bd = blockdiag.astype(BF16)
    avg_bd = (blockdiag.astype(F32) * (1.0 / HEAD)).astype(BF16)
    zero_bd = jnp.zeros((LANES, LANES), BF16)
    avg_2 = jnp.concatenate([avg_bd, avg_bd], axis=0)
    stat_bd = jnp.concatenate([jnp.concatenate([avg_bd, zero_bd], axis=1),
                               jnp.concatenate([zero_bd, ones_bd], axis=1)], axis=0)
    pairs = range(N_PAIRS)
    sls = [slice(p * LANES, (p + 1) * LANES) for p in pairs]
    chunks = [slice(j * c, (j + 1) * c) for j in range(opb_ref.shape[2] // c)]
    tiles = [(j, rs, p, sl) for j, rs in enumerate(chunks) for p, sl in enumerate(sls)]
    opb = lambda name, rs, sl: opb_ref[slot, _OPS_B.index(name), rs, sl]
    opf = lambda name, rs, sl: opf_ref[slot, _OPS_F.index(name), rs, sl]

    grams = [
        _dot(jnp.concatenate([opb("at", rs, sl), opb("rt", rs, sl)], axis=0),
             jnp.concatenate([_stack_heads(opb("bt", rs, sl), first),
                              _stack_heads(opb("kt", rs, sl), first)], axis=0), _NT)
        for _, rs, _, sl in tiles]
    a_abs = [jnp.where(strict2, gm[:c, :2 * c], 0.0).astype(BF16) for gm in grams]
    t_invs = _unit_lower_inverses(a_abs, c, filler)
    a_rbs = [jnp.where(incl2, gm[c:, :2 * c], 0.0).astype(BF16) for gm in grams]
    akvs = [
        _dot(jnp.concatenate([jnp.where(strict2, gm[:c, 2 * c:], 0.0),
                              jnp.where(incl2, gm[c:, 2 * c:], 0.0)], axis=0).astype(BF16),
             _stack_heads(opb("vb", rs, sl), first))
        for gm, (_, rs, _, sl) in zip(grams, tiles)]

    ta0s = [_dot(t, _stack_heads(opb("a0", rs, sl), first)).astype(BF16)
            for t, (_, rs, _, sl) in zip(t_invs, tiles)]
    tavs = [_dot(t, _stack_heads(akv[:c].astype(BF16), first)).astype(BF16)
            for t, akv in zip(t_invs, akvs)]
    next(filler, None)
    m_offs = [jnp.where(blockdiag, _dot(ta0, opb("bh", rs, sl), _TN), 0.0).astype(BF16)
              for ta0, (_, rs, _, sl) in zip(ta0s, tiles)]
    consts = [jnp.where(blockdiag,
                        _dot(jnp.concatenate([tav, opb("vb", rs, sl)], axis=0),
                             jnp.concatenate([opb("bh", rs, sl), opb("kh", rs, sl)], axis=0), _TN),
                        0.0)
              for tav, (_, rs, _, sl) in zip(tavs, tiles)]
    next(filler, None)
    ras = [(opb("r0", rs, sl).astype(F32) + _dot(a_rb, _stack_heads(ta0, first))).astype(BF16)
           for a_rb, ta0, (_, rs, _, sl) in zip(a_rbs, ta0s, tiles)]
    y_consts = [akv[c:] + _dot(a_rb, _stack_heads(tav, first))
                for akv, a_rb, tav in zip(akvs, a_rbs, tavs)]
    next(filler, None)

    states = [state_ref[p] for p in pairs]
    ys = []
    for j in range(len(chunks)):
        mine = slice(j * N_PAIRS, (j + 1) * N_PAIRS)
        s_bs = [s.astype(BF16) for s in states]
        ys += [_dot(ra, s_b, _NT) + yc for ra, yc, s_b in zip(ras[mine], y_consts[mine], s_bs)]
        states = [s * dend_ref[slot, j * SUBLANES:j * SUBLANES + 1, sl] + _dot(s_b, m) + cm
                  for s, s_b, m, cm, sl in zip(states, s_bs, m_offs[mine], consts[mine], sls)]
    for p, s in enumerate(states):
        state_ref[p] = s

    means = [_dot(_hilo_cols(y), avg_2) for y in ys]
    ycs = [y - m for y, m in zip(ys, means)]
    stats = [_dot(jnp.concatenate([(yc * yc).astype(BF16), opb("rkk", rs, sl)], axis=1), stat_bd)
             for yc, (_, rs, _, sl) in zip(ycs, tiles)]
    for yc, st, (_, rs, _, sl) in zip(ycs, stats, tiles):
        o_ref[rs, sl] = ((yc * lax.rsqrt(st[:, :LANES] + GN_EPS) * gn_g[:, sl] + gn_b[:, sl]
                          + st[:, LANES:] * opf("v", rs, sl)) * opf("g", rs, sl)
                         ).astype(o_ref.dtype)


def _gmlp_stages(zg_ref, lng_ref, lnb_ref, wst_ref, bs_ref, o_ref):
    n_tiles = GMLP_WIDTH // LANES
    cols = lambda base, p: slice(base + p * LANES, base + (p + 1) * LANES)
    rows = zg_ref.shape[0]
    chunks = [slice(j * CHUNK, (j + 1) * CHUNK) for j in range(rows // CHUNK)]
    vs = [zg_ref[:, cols(GMLP_WIDTH, p)] for p in range(n_tiles)]
    m = sum(jnp.sum(t, axis=-1, keepdims=True) for t in vs) * (1.0 / GMLP_WIDTH)
    vcs = [t - m for t in vs]
    var = sum(jnp.sum(t * t, axis=-1, keepdims=True) for t in vcs) * (1.0 / GMLP_WIDTH)
    inv = lax.rsqrt(var + LN_EPS)
    yield
    first = _iota2((CHUNK, LANES), 1) < HEAD
    for p in range(n_tiles):
        vn = (vcs[p] * inv * lng_ref[:, cols(0, p)] + lnb_ref[:, cols(0, p)]).astype(BF16)
        u = zg_ref[:, cols(0, p)]
        for rs in chunks:
            mixed = _dot(wst_ref[p], _stack_heads(vn[rs], first))
            o_ref[rs, cols(RWKV_WIDTH, p)] = (
                u[rs] * (mixed + bs_ref[:, cols(0, p)])).astype(o_ref.dtype)
        yield


def _interleaved(*stage_generators):
    for handed in itertools.zip_longest(*stage_generators):
        yield next((h for h in handed if h is not None), None)


def _anchored(x, rider_value):
    if rider_value is None:
        return x
    bits = lax.bitcast_convert_type(rider_value.astype(F32), jnp.int32)
    zero = lax.shift_right_logical(lax.shift_right_logical(bits, 16), 16)
    return x + zero.astype(F32)


def _mixer_kernel(zr_ref, zg_ref, w0_ref, wdec_ref, a0_ref, waaa_ref, wgate_ref, kk_ref,
                  ka_ref, rk_ref, gng_ref, gnb_ref, lng_ref, lnb_ref, ws_ref, bs_ref,
                  o_ref, state_ref, wst_ref, opb_ref, opf_ref, dend_ref):
    s = pl.program_id(0)

    @pl.when(s == 0)
    def _():
        state_ref[...] = jnp.zeros_like(state_ref)
        opb_ref[...] = jnp.zeros_like(opb_ref)
        opf_ref[...] = jnp.zeros_like(opf_ref)
        dend_ref[...] = jnp.zeros_like(dend_ref)
        causal = _iota2((CHUNK, CHUNK), 0) >= _iota2((CHUNK, CHUNK), 1)
        for h in range(ws_ref.shape[0]):
            wst_ref[h // 2, :, (h % 2) * CHUNK:(h % 2 + 1) * CHUNK] = (
                jnp.where(causal, ws_ref[h], 0.0).astype(BF16))

    prepare = _rwkv_prepare_stages(zr_ref, w0_ref, wdec_ref, a0_ref, waaa_ref, wgate_ref, kk_ref,
                                   ka_ref, rk_ref, opb_ref, opf_ref, dend_ref, s % 2)
    gmlp = _gmlp_stages(zg_ref, lng_ref, lnb_ref, wst_ref, bs_ref, o_ref)
    filler = _interleaved(prepare, gmlp)
    _rwkv_chunks(opb_ref, opf_ref, dend_ref, (s + 1) % 2, gng_ref[...], gnb_ref[...],
                 state_ref, o_ref, filler)
    for _ in filler:
        pass


def _mixer(zr, zg, w0, wdec, a0, waaa, wgate, k_k, k_a, r_k, gn_g, gn_b, ln_g, ln_b, w_s, b_s):
    rows = MIX_CHUNKS * CHUNK
    n = zr.shape[0] // rows
    n_heads = w_s.shape[0]

    def full(arr):
        nd = arr.ndim
        return pl.BlockSpec(arr.shape, lambda s: (0,) * nd)

    params = (w0, wdec, a0, waaa, wgate, k_k, k_a, r_k, gn_g, gn_b, ln_g, ln_b, w_s, b_s)
    return pl.pallas_call(
        _mixer_kernel,
        out_shape=jax.ShapeDtypeStruct((n * rows, D_MODEL), BF16),
        grid=(n + 1,),
        in_specs=[pl.BlockSpec((rows, RWKV_COLS), lambda s: (jnp.minimum(s, n - 1), 0)),
                  pl.BlockSpec((rows, 2 * GMLP_WIDTH), lambda s: (jnp.maximum(s - 1, 0), 0))]
                 + [full(a) for a in params],
        out_specs=pl.BlockSpec((rows, D_MODEL), lambda s: (jnp.maximum(s - 1, 0), 0)),
        scratch_shapes=[
            pltpu.VMEM((N_PAIRS, LANES, LANES), F32),
            pltpu.VMEM((n_heads // 2, CHUNK, 2 * CHUNK), BF16),
            pltpu.VMEM((2, len(_OPS_B), rows, RWKV_WIDTH), BF16),
            pltpu.VMEM((2, len(_OPS_F), rows, RWKV_WIDTH), F32),
            pltpu.VMEM((2, rows // RWKV_CHUNK * SUBLANES, RWKV_WIDTH), F32),
        ],
        compiler_params=pltpu.CompilerParams(
            dimension_semantics=("arbitrary",), vmem_limit_bytes=VMEM_LIMIT),
        name="mixer",
    )(zr, zg, *params)


def kernel(x, p, norm_ffn1, ffn1_w1, ffn1_w3, ffn1_w2, norm_mix, w_in, shift_mu, rwkv_w0,
           rwkv_w_decay, rwkv_a0, rwkv_w_aaa, rwkv_w_gate, rwkv_k_k, rwkv_k_a, rwkv_r_k,
           rwkv_gn_g, rwkv_gn_b, sgu_ln_g, sgu_ln_b, sgu_w_s, sgu_b_s, w_out, norm_ffn2,
           ffn2_w1, ffn2_w3, ffn2_w2, norm_ple, w_ple_gate, w_ple, norm_final):
    assert x.shape[0] == 1 and p.shape[0] == 1, "one batch row and one layer, as the problem states"
    row = lambda a: a.reshape(1, -1)
    bf = lambda a: a.astype(BF16)
    h, zr, zg = _pre_mixer(
        x[0], [row(norm_ffn1[0]), bf(ffn1_w1[0]), bf(ffn1_w3[0]), bf(ffn1_w2[0]),
               row(norm_mix[0]), bf(w_in[0]), row(shift_mu[0])])
    lora = rwkv_w_decay.shape[1]
    pad = jnp.zeros((LANES - lora, RWKV_WIDTH), F32)
    wdec = jnp.concatenate([rwkv_w_decay[0], pad], axis=0)
    waaa = jnp.concatenate([pad, rwkv_w_aaa[0]], axis=0)
    bias = jnp.repeat(sgu_b_s[0].T, HEAD, axis=1)
    y = _mixer(zr, zg, row(rwkv_w0[0]), bf(wdec), row(rwkv_a0[0]), bf(waaa),
               bf(rwkv_w_gate[0]), row(rwkv_k_k[0]), row(rwkv_k_a[0]), row(rwkv_r_k[0]),
               row(rwkv_gn_g[0]), row(rwkv_gn_b[0]), row(sgu_ln_g[0]), row(sgu_ln_b[0]),
               sgu_w_s[0], bias)
    out = _post_mixer(
        h, y, p[0, 0],
        [bf(w_out[0]), row(norm_ffn2[0]), bf(ffn2_w1[0]), bf(ffn2_w3[0]), bf(ffn2_w2[0]),
         row(norm_ple[0]), bf(w_ple_gate[0]), bf(w_ple[0]), row(norm_final)])
    return out[None]
```

```python
import itertools

import jax
import jax.numpy as jnp
from jax import lax
from jax.experimental import pallas as pl
from jax.experimental.pallas import tpu as pltpu

F32 = jnp.float32
BF16 = jnp.bfloat16

D_MODEL = 1024
D_FF = 2816
RWKV_WIDTH = 512
RWKV_COLS = 1792
GMLP_WIDTH = 512
HEAD = 64
LANES = 128
SUBLANES = 8
N_PAIRS = RWKV_WIDTH // LANES
CHUNK = 128
RWKV_CHUNK = 64
MIX_CHUNKS = 4
RMS_EPS = 1e-6
LN_EPS = 1e-5
GN_EPS = 64e-5
VMEM_LIMIT = 56 * 1024 * 1024

_NN = (((1,), (0,)), ((), ()))
_NT = (((1,), (1,)), ((), ()))
_TN = (((0,), (0,)), ((), ()))


def _dot(a, b, dn=_NN):
    return lax.dot_general(a, b, dn, preferred_element_type=F32)


def _split2(x):
    hi = x.astype(BF16)
    lo = (x - hi.astype(F32)).astype(BF16)
    return hi, lo


def _hilo_cols(x):
    return jnp.concatenate(_split2(x), axis=1)


def _hilo_rows(x):
    return jnp.concatenate(_split2(x), axis=0)


def _rms(x, g):
    ms = jnp.mean(x * x, axis=-1, keepdims=True)
    return x * lax.rsqrt(ms + RMS_EPS) * g


FF_CHUNK = 256


def _swiglu_hidden(xn, w1_ref, w3_ref, hm_ref, filler):
    for j in range(hm_ref.shape[1] // FF_CHUNK):
        cols = slice(j * FF_CHUNK, (j + 1) * FF_CHUNK)
        a = _dot(xn, w1_ref[:, cols])
        b = _dot(xn, w3_ref[:, cols])
        hm_ref[:, cols] = (a * jax.nn.sigmoid(a) * b).astype(BF16)
        next(filler, None)


def _rms_stages(x_ref, g_ref, out_ref, slot, rows_per_stage=64):
    for start in range(0, x_ref.shape[0], rows_per_stage):
        rs = slice(start, start + rows_per_stage)
        out_ref[slot, rs, :] = _rms(x_ref[rs, :], g_ref[...]).astype(BF16)
        yield


def _pre_mixer_kernel(x_ref, xnext_ref, g1_ref, w1_ref, w3_ref, w2_ref, gm_ref, win_ref, mu_ref,
                      h_ref, zr_ref, zg_ref, hm_ref, xn1_ref, xn2_ref, carry_ref):
    i = pl.program_id(0)
    last = pl.num_programs(0) - 1

    def project(slot):
        z = _dot(xn2_ref[slot], win_ref[...])
        rows = z.shape[0]
        first_row = _iota2((rows, LANES), 0) == 0
        for col in range(0, RWKV_COLS, LANES):
            cols = slice(col, col + LANES)
            zc = z[:, cols]
            zp = jnp.where(first_row, carry_ref[:, cols], pltpu.roll(zc, 1, 0))
            carry_ref[:, cols] = zc[rows - 1:rows, :]
            zr_ref[:, cols] = zc + (zp - zc) * mu_ref[:, cols]
        zg = z[:, RWKV_COLS:]
        zg_ref[...] = 0.5 * zg * (1.0 + lax.erf(zg * 0.7071067811865476))

    @pl.when(i == 0)
    def _():
        xn1_ref[0] = _rms(x_ref[...], g1_ref[...]).astype(BF16)
        xn2_ref[1] = jnp.zeros(xn2_ref.shape[1:], BF16)
        carry_ref[...] = jnp.zeros_like(carry_ref)

    @pl.when(i < last)
    def _():
        project((i + 1) % 2)
        filler = _rms_stages(xnext_ref, g1_ref, xn1_ref, (i + 1) % 2)
        _swiglu_hidden(xn1_ref[i % 2], w1_ref, w3_ref, hm_ref, filler)
        for _ in filler:
            pass
        acc = _dot(hm_ref[...], w2_ref[...])
        h = x_ref[...] + 0.5 * acc
        h_ref[...] = h
        xn2_ref[i % 2] = _rms(h, gm_ref[...]).astype(BF16)

    @pl.when(i == last)
    def _():
        project((i + 1) % 2)


def _post_mixer_kernel(h_ref, y_ref, p_ref, wout_ref, g2_ref, w1_ref, w3_ref, w2_ref,
                       gp_ref, wg_ref, wp_ref, gf_ref, o_ref, hm_ref):
    h = h_ref[...] + _dot(y_ref[...], wout_ref[...])
    _swiglu_hidden(_rms(h, g2_ref[...]).astype(BF16), w1_ref, w3_ref, hm_ref, iter(()))
    h = h + 0.5 * _dot(hm_ref[...], w2_ref[...])
    gate = jax.nn.sigmoid(_dot(_rms(h, gp_ref[...]).astype(BF16), wg_ref[...]))
    e = _dot(p_ref[...].astype(BF16), wp_ref[...])
    o_ref[...] = _rms(h + gate * e, gf_ref[...])


def _rows(tm, width, tile_of_step=lambda i: i):
    return pl.BlockSpec((tm, width), lambda i: (tile_of_step(i), 0))


def _resident(arr):
    nd = arr.ndim
    return pl.BlockSpec(arr.shape, lambda i: (0,) * nd, pipeline_mode=pl.Buffered(1))


def _pre_mixer(x, params, *, tm=512):
    t, d = x.shape
    n = t // tm
    this = lambda i: jnp.minimum(i, n - 1)
    nxt = lambda i: jnp.minimum(i + 1, n - 1)
    prev = lambda i: jnp.maximum(i - 1, 0)
    return pl.pallas_call(
        _pre_mixer_kernel,
        out_shape=(jax.ShapeDtypeStruct((t, d), F32),
                   jax.ShapeDtypeStruct((t, RWKV_COLS), F32),
                   jax.ShapeDtypeStruct((t, 2 * GMLP_WIDTH), F32)),
        grid=(n + 1,),
        in_specs=[_rows(tm, d, this), _rows(tm, d, nxt)] + [_resident(a) for a in params],
        out_specs=(_rows(tm, d, this), _rows(tm, RWKV_COLS, prev),
                   _rows(tm, 2 * GMLP_WIDTH, prev)),
        scratch_shapes=[pltpu.VMEM((tm, D_FF), BF16),
                        pltpu.VMEM((2, tm, d), BF16),
                        pltpu.VMEM((2, tm, d), BF16),
                        pltpu.VMEM((1, RWKV_COLS), F32)],
        compiler_params=pltpu.CompilerParams(
            dimension_semantics=("arbitrary",), vmem_limit_bytes=VMEM_LIMIT),
        name="pre_mixer",
    )(x, x, *params)


def _post_mixer(h, y, p, params, *, tm=1024):
    t, d = h.shape
    return pl.pallas_call(
        _post_mixer_kernel,
        out_shape=jax.ShapeDtypeStruct((t, d), F32),
        grid=(t // tm,),
        in_specs=[_rows(tm, a.shape[1]) for a in (h, y, p)] + [_resident(a) for a in params],
        out_specs=_rows(tm, d),
        scratch_shapes=[pltpu.VMEM((tm, D_FF), BF16)],
        compiler_params=pltpu.CompilerParams(
            dimension_semantics=("arbitrary",), vmem_limit_bytes=VMEM_LIMIT),
        name="post_mixer",
    )(h, y, p, *params)


def _iota2(shape, dim):
    return lax.broadcasted_iota(jnp.int32, shape, dim)


def _stack_heads(x, first):
    zero = jnp.zeros_like(x)
    return jnp.concatenate([jnp.where(first, x, zero), jnp.where(first, zero, x)], axis=0)


def _unit_lower_inverses(mats, c, filler):
    row = _iota2((c, 2 * c), 0)
    lane = _iota2((c, 2 * c), 1)
    col = lane & (c - 1)
    left = lane < c
    eye = jnp.where(row == col, 1.0, 0.0)

    def same_block(b):
        s = b.bit_length() - 1
        return (row >> s) == (col >> s)

    def bd(x):
        zero = jnp.zeros_like(x)
        return jnp.concatenate([jnp.where(left, x, zero), jnp.where(left, zero, x)], axis=0)

    ts = [(eye + jnp.where(same_block(2), a, 0.0)).astype(BF16) for a in mats]
    anchor = None
    anchored_tiles = (0, len(mats) // 2)
    b = 2
    while b < c:
        off = same_block(2 * b) & jnp.logical_not(same_block(b))
        tas = [_dot(t, bd(jnp.where(off, a, 0.0))) for t, a in zip(ts, mats)]
        for i in anchored_tiles:
            tas[i] = _anchored(tas[i], anchor)
        tas = [ta.astype(BF16) for ta in tas]
        anchor = next(filler, None)
        new = [t + _dot(ta, bd(t)) for t, ta in zip(ts, tas)]
        for i in anchored_tiles:
            new[i] = _anchored(new[i], anchor)
        ts = [t.astype(BF16) for t in new]
        anchor = next(filler, None)
        b *= 2
    return ts


_OPS_B = ("rt", "at", "bt", "kt", "r0", "a0", "bh", "kh", "vb", "rkk")
_OPS_F = ("v", "g")


def _rwkv_prepare_stages(zr_ref, w0_ref, wdec_ref, a0_ref, waaa_ref,
                         wgate_ref, kk_ref, ka_ref, rk_ref, opb_ref, opf_ref, dend_ref, slot):
    c = RWKV_CHUNK
    rows = zr_ref.shape[0]
    chunks = [slice(j * c, (j + 1) * c) for j in range(rows // c)]
    shifted = lambda col: zr_ref[:, col:col + LANES]

    w = RWKV_WIDTH
    zwa = shifted(3 * w)
    zwa_tanh = jnp.tanh(zwa).astype(BF16)
    zwa_b = zwa.astype(BF16)
    zg_sig = jax.nn.sigmoid(shifted(3 * w + LANES)).astype(BF16)
    blockdiag = (_iota2((LANES, LANES), 0) >= HEAD) == (_iota2((LANES, LANES), 1) >= HEAD)
    ones_bd = blockdiag.astype(BF16)
    row2 = _iota2((c, 2 * c), 0)
    col2 = _iota2((c, 2 * c), 1) & (c - 1)
    tril_2 = (row2 >= col2).astype(BF16)
    yield
    for p in range(N_PAIRS):
        sl = slice(p * LANES, (p + 1) * LANES)
        r, k, v = shifted(p * LANES), shifted(w + p * LANES), shifted(2 * w + p * LANES)
        lw = w0_ref[:, sl] + _dot(zwa_tanh, wdec_ref[:, sl])
        logd = -0.6065306597126334 * jax.nn.sigmoid(lw)
        a = jax.nn.sigmoid(a0_ref[:, sl] + _dot(zwa_b, waaa_ref[:, sl]))
        g = _dot(zg_sig, wgate_ref[:, sl])
        kk = k * kk_ref[:, sl]
        n2 = _dot((kk * kk).astype(BF16), ones_bd)
        kkn = kk * lax.rsqrt(jnp.maximum(n2, 1e-24))
        k2 = k * (1.0 + (a - 1.0) * ka_ref[:, sl])
        bvec = kkn * a
        opb_ref[slot, _OPS_B.index("rkk"), :, sl] = (r * k2 * rk_ref[:, sl]).astype(BF16)
        opf_ref[slot, _OPS_F.index("v"), :, sl] = v
        opf_ref[slot, _OPS_F.index("g"), :, sl] = g
        yield (r * k2)[:c]
        for j, rs in enumerate(chunks):
            lcum = _dot(tril_2, _hilo_rows(logd[rs]))
            lmid = lcum[c // 2 - 1:c // 2, :]
            lend = lcum[c - 1:c, :]
            lexc = lcum - logd[rs]
            e_mid_from = jnp.exp(lmid - lcum)
            e_end = jnp.exp(lend - lcum)
            ops = dict(
                rt=r[rs] * jnp.exp(lcum - lmid), at=-kkn[rs] * jnp.exp(lexc - lmid),
                bt=bvec[rs] * e_mid_from, kt=k2[rs] * e_mid_from,
                r0=r[rs] * jnp.exp(lcum), a0=-kkn[rs] * jnp.exp(lexc),
                bh=bvec[rs] * e_end, kh=k2[rs] * e_end, vb=v[rs])
            for name, value in ops.items():
                opb_ref[slot, _OPS_B.index(name), rs, sl] = value.astype(BF16)
            dend_ref[slot, j * SUBLANES:(j + 1) * SUBLANES, sl] = jnp.broadcast_to(
                jnp.exp(lend), (SUBLANES, LANES))
        yield ops["bh"] + ops["rt"]


def _rwkv_chunks(opb_ref, opf_ref, dend_ref, slot, gn_g, gn_b, state_ref, o_ref, filler):
    c = RWKV_CHUNK
    row2 = _iota2((c, 2 * c), 0)
    col2 = _iota2((c, 2 * c), 1) & (c - 1)
    incl2 = row2 >= col2
    strict2 = row2 > col2
    first = _iota2((c, LANES), 1) < HEAD
    blockdiag = (_iota2((LANES, LANES), 0) >= HEAD) == (_iota2((LANES, LANES), 1) >= HEAD)
    ones_bd = blockdiag.astype(BF16)
    avg_bd = (blockdiag.astype(F32) * (1.0 / HEAD)).astype(BF16)
    zero_bd = jnp.zeros((LANES, LANES), BF16)
    avg_2 = jnp.concatenate([avg_bd, avg_bd], axis=0)
    stat_bd = jnp.concatenate([jnp.concatenate([avg_bd, zero_bd], axis=1),
                               jnp.concatenate([zero_bd, ones_bd], axis=1)], axis=0)
    pairs = range(N_PAIRS)
    sls = [slice(p * LANES, (p + 1) * LANES) for p in pairs]
    chunks = [slice(j * c, (j + 1) * c) for j in range(opb_ref.shape[2] // c)]
    tiles = [(j, rs, p, sl) for j, rs in enumerate(chunks) for p, sl in enumerate(sls)]
    opb = lambda name, rs, sl: opb_ref[slot, _OPS_B.index(name), rs, sl]
    opf = lambda name, rs, sl: opf_ref[slot, _OPS_F.index(name), rs, sl]

    grams = [
        _dot(jnp.concatenate([opb("at", rs, sl), opb("rt", rs, sl)], axis=0),
             jnp.concatenate([_stack_heads(opb("bt", rs, sl), first),
                              _stack_heads(opb("kt", rs, sl), first)], axis=0), _NT)
        for _, rs, _, sl in tiles]
    a_abs = [jnp.where(strict2, gm[:c, :2 * c], 0.0).astype(BF16) for gm in grams]
    t_invs = _unit_lower_inverses(a_abs, c, filler)
    a_rbs = [jnp.where(incl2, gm[c:, :2 * c], 0.0).astype(BF16) for gm in grams]
    akvs = [
        _dot(jnp.concatenate([jnp.where(strict2, gm[:c, 2 * c:], 0.0),
                              jnp.where(incl2, gm[c:, 2 * c:], 0.0)], axis=0).astype(BF16),
             _stack_heads(opb("vb", rs, sl), first))
        for gm, (_, rs, _, sl) in zip(grams, tiles)]

    ta0s = [_dot(t, _stack_heads(opb("a0", rs, sl), first)).astype(BF16)
            for t, (_, rs, _, sl) in zip(t_invs, tiles)]
    tavs = [_dot(t, _stack_heads(akv[:c].astype(BF16), first)).astype(BF16)
            for t, akv in zip(t_invs, akvs)]
    anchor_a = next(filler, None)
    m_offs = [jnp.where(blockdiag, _dot(ta0, opb("bh", rs, sl), _TN), 0.0).astype(BF16)
              for ta0, (_, rs, _, sl) in zip(ta0s, tiles)]
    consts = [jnp.where(blockdiag,
                        _dot(jnp.concatenate([tav, opb("vb", rs, sl)], axis=0),
                             jnp.concatenate([opb("bh", rs, sl), opb("kh", rs, sl)], axis=0), _TN),
                        0.0)
              for tav, (_, rs, _, sl) in zip(tavs, tiles)]
    anchor_b = next(filler, None)
    ras = [(opb("r0", rs, sl).astype(F32) + _dot(a_rb, _stack_heads(ta0, first))).astype(BF16)
           for a_rb, ta0, (_, rs, _, sl) in zip(a_rbs, ta0s, tiles)]
    y_consts = [akv[c:] + _dot(a_rb, _stack_heads(tav, first))
                for akv, a_rb, tav in zip(akvs, a_rbs, tavs)]
    y_consts[0] = _anchored(_anchored(y_consts[0], anchor_a), anchor_b)
    next(filler, None)

    states = [state_ref[p] for p in pairs]
    ys = []
    for j in range(len(chunks)):
        mine = slice(j * N_PAIRS, (j + 1) * N_PAIRS)
        s_bs = [s.astype(BF16) for s in states]
        ys += [_dot(ra, s_b, _NT) + yc for ra, yc, s_b in zip(ras[mine], y_consts[mine], s_bs)]
        states = [s * dend_ref[slot, j * SUBLANES:j * SUBLANES + 1, sl] + _dot(s_b, m) + cm
                  for s, s_b, m, cm, sl in zip(states, s_bs, m_offs[mine], consts[mine], sls)]
    for p, s in enumerate(states):
        state_ref[p] = s

    means = [_dot(_hilo_cols(y), avg_2) for y in ys]
    ycs = [y - m for y, m in zip(ys, means)]
    stats = [_dot(jnp.concatenate([(yc * yc).astype(BF16), opb("rkk", rs, sl)], axis=1), stat_bd)
             for yc, (_, rs, _, sl) in zip(ycs, tiles)]
    for yc, st, (_, rs, _, sl) in zip(ycs, stats, tiles):
        o_ref[rs, sl] = ((yc * lax.rsqrt(st[:, :LANES] + GN_EPS) * gn_g[:, sl] + gn_b[:, sl]
                          + st[:, LANES:] * opf("v", rs, sl)) * opf("g", rs, sl)
                         ).astype(o_ref.dtype)


def _gmlp_stages(zg_ref, lng_ref, lnb_ref, wst_ref, bs_ref, o_ref):
    n_tiles = GMLP_WIDTH // LANES
    cols = lambda base, p: slice(base + p * LANES, base + (p + 1) * LANES)
    rows = zg_ref.shape[0]
    chunks = [slice(j * CHUNK, (j + 1) * CHUNK) for j in range(rows // CHUNK)]
    vs = [zg_ref[:, cols(GMLP_WIDTH, p)] for p in range(n_tiles)]
    m = sum(jnp.sum(t, axis=-1, keepdims=True) for t in vs) * (1.0 / GMLP_WIDTH)
    vcs = [t - m for t in vs]
    var = sum(jnp.sum(t * t, axis=-1, keepdims=True) for t in vcs) * (1.0 / GMLP_WIDTH)
    inv = lax.rsqrt(var + LN_EPS)
    yield
    first = _iota2((CHUNK, LANES), 1) < HEAD
    for p in range(n_tiles):
        vn = (vcs[p] * inv * lng_ref[:, cols(0, p)] + lnb_ref[:, cols(0, p)]).astype(BF16)
        u = zg_ref[:, cols(0, p)]
        for rs in chunks:
            mixed = _dot(wst_ref[p], _stack_heads(vn[rs], first))
            o_ref[rs, cols(RWKV_WIDTH, p)] = (
                u[rs] * (mixed + bs_ref[:, cols(0, p)])).astype(o_ref.dtype)
        yield mixed[:RWKV_CHUNK]


def _interleaved(*stage_generators):
    for handed in itertools.zip_longest(*stage_generators):
        values = [h for h in handed if h is not None]
        yield sum(values[1:], values[0]) if values else None


def _anchored(x, rider_value):
    if rider_value is None:
        return x
    bits = lax.bitcast_convert_type(rider_value.astype(F32), jnp.int32)
    zero = lax.shift_right_logical(lax.shift_right_logical(bits, 16), 16)
    return x + zero.astype(F32)


def _mixer_kernel(zr_ref, zg_ref, w0_ref, wdec_ref, a0_ref, waaa_ref, wgate_ref, kk_ref,
                  ka_ref, rk_ref, gng_ref, gnb_ref, lng_ref, lnb_ref, ws_ref, bs_ref,
                  o_ref, state_ref, wst_ref, opb_ref, opf_ref, dend_ref):
    s = pl.program_id(0)

    @pl.when(s == 0)
    def _():
        state_ref[...] = jnp.zeros_like(state_ref)
        opb_ref[...] = jnp.zeros_like(opb_ref)
        opf_ref[...] = jnp.zeros_like(opf_ref)
        dend_ref[...] = jnp.zeros_like(dend_ref)
        causal = _iota2((CHUNK, CHUNK), 0) >= _iota2((CHUNK, CHUNK), 1)
        for h in range(ws_ref.shape[0]):
            wst_ref[h // 2, :, (h % 2) * CHUNK:(h % 2 + 1) * CHUNK] = (
                jnp.where(causal, ws_ref[h], 0.0).astype(BF16))

    prepare = _rwkv_prepare_stages(zr_ref, w0_ref, wdec_ref, a0_ref, waaa_ref, wgate_ref, kk_ref,
                                   ka_ref, rk_ref, opb_ref, opf_ref, dend_ref, s % 2)
    gmlp = _gmlp_stages(zg_ref, lng_ref, lnb_ref, wst_ref, bs_ref, o_ref)
    filler = _interleaved(prepare, gmlp)
    _rwkv_chunks(opb_ref, opf_ref, dend_ref, (s + 1) % 2, gng_ref[...], gnb_ref[...],
                 state_ref, o_ref, filler)
    for _ in filler:
        pass


def _mixer(zr, zg, w0, wdec, a0, waaa, wgate, k_k, k_a, r_k, gn_g, gn_b, ln_g, ln_b, w_s, b_s):
    rows = MIX_CHUNKS * CHUNK
    n = zr.shape[0] // rows
    n_heads = w_s.shape[0]

    def full(arr):
        nd = arr.ndim
        return pl.BlockSpec(arr.shape, lambda s: (0,) * nd)

    params = (w0, wdec, a0, waaa, wgate, k_k, k_a, r_k, gn_g, gn_b, ln_g, ln_b, w_s, b_s)
    return pl.pallas_call(
        _mixer_kernel,
        out_shape=jax.ShapeDtypeStruct((n * rows, D_MODEL), BF16),
        grid=(n + 1,),
        in_specs=[pl.BlockSpec((rows, RWKV_COLS), lambda s: (jnp.minimum(s, n - 1), 0)),
                  pl.BlockSpec((rows, 2 * GMLP_WIDTH), lambda s: (jnp.maximum(s - 1, 0), 0))]
                 + [full(a) for a in params],
        out_specs=pl.BlockSpec((rows, D_MODEL), lambda s: (jnp.maximum(s - 1, 0), 0)),
        scratch_shapes=[
            pltpu.VMEM((N_PAIRS, LANES, LANES), F32),
            pltpu.VMEM((n_heads // 2, CHUNK, 2 * CHUNK), BF16),
            pltpu.VMEM((2, len(_OPS_B), rows, RWKV_WIDTH), BF16),
            pltpu.VMEM((2, len(_OPS_F), rows, RWKV_WIDTH), F32),
            pltpu.VMEM((2, rows // RWKV_CHUNK * SUBLANES, RWKV_WIDTH), F32),
        ],
        compiler_params=pltpu.CompilerParams(
            dimension_semantics=("arbitrary",), vmem_limit_bytes=VMEM_LIMIT),
        name="mixer",
    )(zr, zg, *params)


def kernel(x, p, norm_ffn1, ffn1_w1, ffn1_w3, ffn1_w2, norm_mix, w_in, shift_mu, rwkv_w0,
           rwkv_w_decay, rwkv_a0, rwkv_w_aaa, rwkv_w_gate, rwkv_k_k, rwkv_k_a, rwkv_r_k,
           rwkv_gn_g, rwkv_gn_b, sgu_ln_g, sgu_ln_b, sgu_w_s, sgu_b_s, w_out, norm_ffn2,
           ffn2_w1, ffn2_w3, ffn2_w2, norm_ple, w_ple_gate, w_ple, norm_final):
    assert x.shape[0] == 1 and p.shape[0] == 1, "one batch row and one layer, as the problem states"
    row = lambda a: a.reshape(1, -1)
    bf = lambda a: a.astype(BF16)
    h, zr, zg = _pre_mixer(
        x[0], [row(norm_ffn1[0]), bf(ffn1_w1[0]), bf(ffn1_w3[0]), bf(ffn1_w2[0]),
               row(norm_mix[0]), bf(w_in[0]), row(shift_mu[0])])
    lora = rwkv_w_decay.shape[1]
    pad = jnp.zeros((LANES - lora, RWKV_WIDTH), F32)
    wdec = jnp.concatenate([rwkv_w_decay[0], pad], axis=0)
    waaa = jnp.concatenate([pad, rwkv_w_aaa[0]], axis=0)
    bias = jnp.repeat(sgu_b_s[0].T, HEAD, axis=1)
    y = _mixer(zr, zg, row(rwkv_w0[0]), bf(wdec), row(rwkv_a0[0]), bf(waaa),
               bf(rwkv_w_gate[0]), row(rwkv_k_k[0]), row(rwkv_k_a[0]), row(rwkv_r_k[0]),
               row(rwkv_gn_g[0]), row(rwkv_gn_b[0]), row(sgu_ln_g[0]), row(sgu_ln_b[0]),
               sgu_w_s[0], bias)
    out = _post_mixer(
        h, y, p[0, 0],
        [bf(w_out[0]), row(norm_ffn2[0]), bf(ffn2_w1[0]), bf(ffn2_w3[0]), bf(ffn2_w2[0]),
         row(norm_ple[0]), bf(w_ple_gate[0]), bf(w_ple[0]), row(norm_final)])
    return out[None]
```

```python
import itertools

import jax
import jax.numpy as jnp
from jax import lax
from jax.experimental import pallas as pl
from jax.experimental.pallas import tpu as pltpu

F32 = jnp.float32
BF16 = jnp.bfloat16

D_MODEL = 1024
D_FF = 2816
RWKV_WIDTH = 512
RWKV_COLS = 1792
GMLP_WIDTH = 512
HEAD = 64
LANES = 128
SUBLANES = 8
N_PAIRS = RWKV_WIDTH // LANES
CHUNK = 128
RWKV_CHUNK = 64
MIX_CHUNKS = 4
RMS_EPS = 1e-6
LN_EPS = 1e-5
GN_EPS = 64e-5
VMEM_LIMIT = 56 * 1024 * 1024

_NN = (((1,), (0,)), ((), ()))
_NT = (((1,), (1,)), ((), ()))
_TN = (((0,), (0,)), ((), ()))


def _dot(a, b, dn=_NN):
    return lax.dot_general(a, b, dn, preferred_element_type=F32)


def _split2(x):
    hi = x.astype(BF16)
    lo = (x - hi.astype(F32)).astype(BF16)
    return hi, lo


def _hilo_cols(x):
    return jnp.concatenate(_split2(x), axis=1)


def _hilo_rows(x):
    return jnp.concatenate(_split2(x), axis=0)


def _rms(x, g):
    ms = jnp.mean(x * x, axis=-1, keepdims=True)
    return x * lax.rsqrt(ms + RMS_EPS) * g


FF_CHUNK = 256


def _swiglu_hidden(xn, w1_ref, w3_ref, hm_ref, filler):
    for j in range(hm_ref.shape[1] // FF_CHUNK):
        cols = slice(j * FF_CHUNK, (j + 1) * FF_CHUNK)
        a = _dot(xn, w1_ref[:, cols])
        b = _dot(xn, w3_ref[:, cols])
        hm_ref[:, cols] = (a * jax.nn.sigmoid(a) * b).astype(BF16)
        next(filler, None)


def _rms_stages(x_ref, g_ref, out_ref, slot, rows_per_stage=64):
    for start in range(0, x_ref.shape[0], rows_per_stage):
        rs = slice(start, start + rows_per_stage)
        out_ref[slot, rs, :] = _rms(x_ref[rs, :], g_ref[...]).astype(BF16)
        yield


def _pre_mixer_kernel(x_ref, xnext_ref, g1_ref, w1_ref, w3_ref, w2_ref, gm_ref, win_ref, mu_ref,
                      h_ref, zr_ref, zg_ref, hm_ref, xn1_ref, xn2_ref, carry_ref):
    i = pl.program_id(0)
    last = pl.num_programs(0) - 1

    def project(slot):
        z = _dot(xn2_ref[slot], win_ref[...])
        rows = z.shape[0]
        first_row = _iota2((rows, LANES), 0) == 0
        for col in range(0, RWKV_COLS, LANES):
            cols = slice(col, col + LANES)
            zc = z[:, cols]
            zp = jnp.where(first_row, carry_ref[:, cols], pltpu.roll(zc, 1, 0))
            carry_ref[:, cols] = zc[rows - 1:rows, :]
            zr_ref[:, cols] = zc + (zp - zc) * mu_ref[:, cols]
        zg = z[:, RWKV_COLS:]
        zg_ref[...] = 0.5 * zg * (1.0 + lax.erf(zg * 0.7071067811865476))

    @pl.when(i == 0)
    def _():
        xn1_ref[0] = _rms(x_ref[...], g1_ref[...]).astype(BF16)
        xn2_ref[1] = jnp.zeros(xn2_ref.shape[1:], BF16)
        carry_ref[...] = jnp.zeros_like(carry_ref)

    @pl.when(i < last)
    def _():
        project((i + 1) % 2)
        filler = _rms_stages(xnext_ref, g1_ref, xn1_ref, (i + 1) % 2)
        _swiglu_hidden(xn1_ref[i % 2], w1_ref, w3_ref, hm_ref, filler)
        for _ in filler:
            pass
        acc = _dot(hm_ref[...], w2_ref[...])
        h = x_ref[...] + 0.5 * acc
        h_ref[...] = h
        xn2_ref[i % 2] = _rms(h, gm_ref[...]).astype(BF16)

    @pl.when(i == last)
    def _():
        project((i + 1) % 2)


def _post_mixer_kernel(h_ref, y_ref, p_ref, wout_ref, g2_ref, w1_ref, w3_ref, w2_ref,
                       gp_ref, wg_ref, wp_ref, gf_ref, o_ref, hm_ref):
    h = h_ref[...] + _dot(y_ref[...], wout_ref[...])
    _swiglu_hidden(_rms(h, g2_ref[...]).astype(BF16), w1_ref, w3_ref, hm_ref, iter(()))
    h = h + 0.5 * _dot(hm_ref[...], w2_ref[...])
    gate = jax.nn.sigmoid(_dot(_rms(h, gp_ref[...]).astype(BF16), wg_ref[...]))
    e = _dot(p_ref[...].astype(BF16), wp_ref[...])
    o_ref[...] = _rms(h + gate * e, gf_ref[...])


def _rows(tm, width, tile_of_step=lambda i: i):
    return pl.BlockSpec((tm, width), lambda i: (tile_of_step(i), 0))


def _resident(arr):
    nd = arr.ndim
    return pl.BlockSpec(arr.shape, lambda i: (0,) * nd, pipeline_mode=pl.Buffered(1))


def _pre_mixer(x, params, *, tm=512):
    t, d = x.shape
    n = t // tm
    this = lambda i: jnp.minimum(i, n - 1)
    nxt = lambda i: jnp.minimum(i + 1, n - 1)
    prev = lambda i: jnp.maximum(i - 1, 0)
    return pl.pallas_call(
        _pre_mixer_kernel,
        out_shape=(jax.ShapeDtypeStruct((t, d), F32),
                   jax.ShapeDtypeStruct((t, RWKV_COLS), F32),
                   jax.ShapeDtypeStruct((t, 2 * GMLP_WIDTH), F32)),
        grid=(n + 1,),
        in_specs=[_rows(tm, d, this), _rows(tm, d, nxt)] + [_resident(a) for a in params],
        out_specs=(_rows(tm, d, this), _rows(tm, RWKV_COLS, prev),
                   _rows(tm, 2 * GMLP_WIDTH, prev)),
        scratch_shapes=[pltpu.VMEM((tm, D_FF), BF16),
                        pltpu.VMEM((2, tm, d), BF16),
                        pltpu.VMEM((2, tm, d), BF16),
                        pltpu.VMEM((1, RWKV_COLS), F32)],
        compiler_params=pltpu.CompilerParams(
            dimension_semantics=("arbitrary",), vmem_limit_bytes=VMEM_LIMIT),
        name="pre_mixer",
    )(x, x, *params)


def _post_mixer(h, y, p, params, *, tm=1024):
    t, d = h.shape
    return pl.pallas_call(
        _post_mixer_kernel,
        out_shape=jax.ShapeDtypeStruct((t, d), F32),
        grid=(t // tm,),
        in_specs=[_rows(tm, a.shape[1]) for a in (h, y, p)] + [_resident(a) for a in params],
        out_specs=_rows(tm, d),
        scratch_shapes=[pltpu.VMEM((tm, D_FF), BF16)],
        compiler_params=pltpu.CompilerParams(
            dimension_semantics=("arbitrary",), vmem_limit_bytes=VMEM_LIMIT),
        name="post_mixer",
    )(h, y, p, *params)


def _iota2(shape, dim):
    return lax.broadcasted_iota(jnp.int32, shape, dim)


def _stack_heads(x, first):
    zero = jnp.zeros_like(x)
    return jnp.concatenate([jnp.where(first, x, zero), jnp.where(first, zero, x)], axis=0)


def _unit_lower_inverses(mats, c, filler):
    row = _iota2((c, 2 * c), 0)
    lane = _iota2((c, 2 * c), 1)
    col = lane & (c - 1)
    left = lane < c
    eye = jnp.where(row == col, 1.0, 0.0)

    def same_block(b):
        s = b.bit_length() - 1
        return (row >> s) == (col >> s)

    def bd(x):
        zero = jnp.zeros_like(x)
        return jnp.concatenate([jnp.where(left, x, zero), jnp.where(left, zero, x)], axis=0)

    ts = [(eye + jnp.where(same_block(2), a, 0.0)).astype(BF16) for a in mats]
    anchor = None
    anchored_tiles = range(0, len(mats), N_PAIRS)
    b = 2
    while b < c:
        off = same_block(2 * b) & jnp.logical_not(same_block(b))
        tas = [_dot(t, bd(jnp.where(off, a, 0.0))) for t, a in zip(ts, mats)]
        for i in anchored_tiles:
            tas[i] = _anchored(tas[i], anchor)
        tas = [ta.astype(BF16) for ta in tas]
        anchor = next(filler, None)
        new = [t + _dot(ta, bd(t)) for t, ta in zip(ts, tas)]
        for i in anchored_tiles:
            new[i] = _anchored(new[i], anchor)
        ts = [t.astype(BF16) for t in new]
        anchor = next(filler, None)
        b *= 2
    return ts


_OPS_B = ("rt", "at", "bt", "kt", "r0", "a0", "bh", "kh", "vb", "rkk")
_OPS_F = ("v", "g")


def _rwkv_prepare_stages(zr_ref, w0_ref, wdec_ref, a0_ref, waaa_ref,
                         wgate_ref, kk_ref, ka_ref, rk_ref, opb_ref, opf_ref, dend_ref, slot):
    c = RWKV_CHUNK
    rows = zr_ref.shape[0]
    chunks = [slice(j * c, (j + 1) * c) for j in range(rows // c)]
    shifted = lambda col: zr_ref[:, col:col + LANES]

    w = RWKV_WIDTH
    zwa = shifted(3 * w)
    zwa_tanh = jnp.tanh(zwa).astype(BF16)
    zwa_b = zwa.astype(BF16)
    zg_sig = jax.nn.sigmoid(shifted(3 * w + LANES)).astype(BF16)
    blockdiag = (_iota2((LANES, LANES), 0) >= HEAD) == (_iota2((LANES, LANES), 1) >= HEAD)
    ones_bd = blockdiag.astype(BF16)
    row2 = _iota2((c, 2 * c), 0)
    col2 = _iota2((c, 2 * c), 1) & (c - 1)
    tril_2 = (row2 >= col2).astype(BF16)
    yield
    for p in range(N_PAIRS):
        sl = slice(p * LANES, (p + 1) * LANES)
        r, k, v = shifted(p * LANES), shifted(w + p * LANES), shifted(2 * w + p * LANES)
        lw = w0_ref[:, sl] + _dot(zwa_tanh, wdec_ref[:, sl])
        logd = -0.6065306597126334 * jax.nn.sigmoid(lw)
        a = jax.nn.sigmoid(a0_ref[:, sl] + _dot(zwa_b, waaa_ref[:, sl]))
        g = _dot(zg_sig, wgate_ref[:, sl])
        kk = k * kk_ref[:, sl]
        n2 = _dot((kk * kk).astype(BF16), ones_bd)
        kkn = kk * lax.rsqrt(jnp.maximum(n2, 1e-24))
        k2 = k * (1.0 + (a - 1.0) * ka_ref[:, sl])
        bvec = kkn * a
        opb_ref[slot, _OPS_B.index("rkk"), :, sl] = (r * k2 * rk_ref[:, sl]).astype(BF16)
        opf_ref[slot, _OPS_F.index("v"), :, sl] = v
        opf_ref[slot, _OPS_F.index("g"), :, sl] = g
        yield (r * k2)[:c] + g[:c]
        for j, rs in enumerate(chunks):
            lcum = _dot(tril_2, _hilo_rows(logd[rs]))
            lmid = lcum[c // 2 - 1:c // 2, :]
            lend = lcum[c - 1:c, :]
            lexc = lcum - logd[rs]
            e_mid_from = jnp.exp(lmid - lcum)
            e_end = jnp.exp(lend - lcum)
            ops = dict(
                rt=r[rs] * jnp.exp(lcum - lmid), at=-kkn[rs] * jnp.exp(lexc - lmid),
                bt=bvec[rs] * e_mid_from, kt=k2[rs] * e_mid_from,
                r0=r[rs] * jnp.exp(lcum), a0=-kkn[rs] * jnp.exp(lexc),
                bh=bvec[rs] * e_end, kh=k2[rs] * e_end, vb=v[rs])
            for name, value in ops.items():
                opb_ref[slot, _OPS_B.index(name), rs, sl] = value.astype(BF16)
            dend_ref[slot, j * SUBLANES:(j + 1) * SUBLANES, sl] = jnp.broadcast_to(
                jnp.exp(lend), (SUBLANES, LANES))
        yield (ops["bh"] + ops["rt"]) + (ops["kh"] + ops["a0"])


def _rwkv_chunks(opb_ref, opf_ref, dend_ref, slot, gn_g, gn_b, state_ref, o_ref, filler):
    c = RWKV_CHUNK
    row2 = _iota2((c, 2 * c), 0)
    col2 = _iota2((c, 2 * c), 1) & (c - 1)
    incl2 = row2 >= col2
    strict2 = row2 > col2
    first = _iota2((c, LANES), 1) < HEAD
    blockdiag = (_iota2((LANES, LANES), 0) >= HEAD) == (_iota2((LANES, LANES), 1) >= HEAD)
    ones_bd = blockdiag.astype(BF16)
    avg_bd = (blockdiag.astype(F32) * (1.0 / HEAD)).astype(BF16)
    zero_bd = jnp.zeros((LANES, LANES), BF16)
    avg_2 = jnp.concatenate([avg_bd, avg_bd], axis=0)
    stat_bd = jnp.concatenate([jnp.concatenate([avg_bd, zero_bd], axis=1),
                               jnp.concatenate([zero_bd, ones_bd], axis=1)], axis=0)
    pairs = range(N_PAIRS)
    sls = [slice(p * LANES, (p + 1) * LANES) for p in pairs]
    chunks = [slice(j * c, (j + 1) * c) for j in range(opb_ref.shape[2] // c)]
    tiles = [(j, rs, p, sl) for j, rs in enumerate(chunks) for p, sl in enumerate(sls)]
    opb = lambda name, rs, sl: opb_ref[slot, _OPS_B.index(name), rs, sl]
    opf = lambda name, rs, sl: opf_ref[slot, _OPS_F.index(name), rs, sl]

    grams = [
        _dot(jnp.concatenate([opb("at", rs, sl), opb("rt", rs, sl)], axis=0),
             jnp.concatenate([_stack_heads(opb("bt", rs, sl), first),
                              _stack_heads(opb("kt", rs, sl), first)], axis=0), _NT)
        for _, rs, _, sl in tiles]
    a_abs = [jnp.where(strict2, gm[:c, :2 * c], 0.0).astype(BF16) for gm in grams]
    t_invs = _unit_lower_inverses(a_abs, c, filler)
    a_rbs = [jnp.where(incl2, gm[c:, :2 * c], 0.0).astype(BF16) for gm in grams]
    akvs = [
        _dot(jnp.concatenate([jnp.where(strict2, gm[:c, 2 * c:], 0.0),
                              jnp.where(incl2, gm[c:, 2 * c:], 0.0)], axis=0).astype(BF16),
             _stack_heads(opb("vb", rs, sl), first))
        for gm, (_, rs, _, sl) in zip(grams, tiles)]

    ta0s = [_dot(t, _stack_heads(opb("a0", rs, sl), first)).astype(BF16)
            for t, (_, rs, _, sl) in zip(t_invs, tiles)]
    tavs = [_dot(t, _stack_heads(akv[:c].astype(BF16), first)).astype(BF16)
            for t, akv in zip(t_invs, akvs)]
    anchor_a = next(filler, None)
    m_offs = [jnp.where(blockdiag, _dot(ta0, opb("bh", rs, sl), _TN), 0.0).astype(BF16)
              for ta0, (_, rs, _, sl) in zip(ta0s, tiles)]
    consts = [jnp.where(blockdiag,
                        _dot(jnp.concatenate([tav, opb("vb", rs, sl)], axis=0),
                             jnp.concatenate([opb("bh", rs, sl), opb("kh", rs, sl)], axis=0), _TN),
                        0.0)
              for tav, (_, rs, _, sl) in zip(tavs, tiles)]
    anchor_b = next(filler, None)
    ras = [(opb("r0", rs, sl).astype(F32) + _dot(a_rb, _stack_heads(ta0, first))).astype(BF16)
           for a_rb, ta0, (_, rs, _, sl) in zip(a_rbs, ta0s, tiles)]
    y_consts = [akv[c:] + _dot(a_rb, _stack_heads(tav, first))
                for akv, a_rb, tav in zip(akvs, a_rbs, tavs)]
    y_consts[0] = _anchored(_anchored(y_consts[0], anchor_a), anchor_b)
    next(filler, None)

    states = [state_ref[p] for p in pairs]
    ys = []
    for j in range(len(chunks)):
        mine = slice(j * N_PAIRS, (j + 1) * N_PAIRS)
        s_bs = [s.astype(BF16) for s in states]
        ys += [_dot(ra, s_b, _NT) + yc for ra, yc, s_b in zip(ras[mine], y_consts[mine], s_bs)]
        states = [s * dend_ref[slot, j * SUBLANES:j * SUBLANES + 1, sl] + _dot(s_b, m) + cm
                  for s, s_b, m, cm, sl in zip(states, s_bs, m_offs[mine], consts[mine], sls)]
    for p, s in enumerate(states):
        state_ref[p] = s

    means = [_dot(_hilo_cols(y), avg_2) for y in ys]
    ycs = [y - m for y, m in zip(ys, means)]
    stats = [_dot(jnp.concatenate([(yc * yc).astype(BF16), opb("rkk", rs, sl)], axis=1), stat_bd)
             for yc, (_, rs, _, sl) in zip(ycs, tiles)]
    for yc, st, (_, rs, _, sl) in zip(ycs, stats, tiles):
        o_ref[rs, sl] = ((yc * lax.rsqrt(st[:, :LANES] + GN_EPS) * gn_g[:, sl] + gn_b[:, sl]
                          + st[:, LANES:] * opf("v", rs, sl)) * opf("g", rs, sl)
                         ).astype(o_ref.dtype)


def _gmlp_stages(zg_ref, lng_ref, lnb_ref, wst_ref, bs_ref, o_ref):
    n_tiles = GMLP_WIDTH // LANES
    cols = lambda base, p: slice(base + p * LANES, base + (p + 1) * LANES)
    rows = zg_ref.shape[0]
    chunks = [slice(j * CHUNK, (j + 1) * CHUNK) for j in range(rows // CHUNK)]
    vs = [zg_ref[:, cols(GMLP_WIDTH, p)] for p in range(n_tiles)]
    m = sum(jnp.sum(t, axis=-1, keepdims=True) for t in vs) * (1.0 / GMLP_WIDTH)
    vcs = [t - m for t in vs]
    var = sum(jnp.sum(t * t, axis=-1, keepdims=True) for t in vcs) * (1.0 / GMLP_WIDTH)
    inv = lax.rsqrt(var + LN_EPS)
    yield
    first = _iota2((CHUNK, LANES), 1) < HEAD
    for p in range(n_tiles):
        vn = (vcs[p] * inv * lng_ref[:, cols(0, p)] + lnb_ref[:, cols(0, p)]).astype(BF16)
        u = zg_ref[:, cols(0, p)]
        for rs in chunks:
            mixed = _dot(wst_ref[p], _stack_heads(vn[rs], first))
            o_ref[rs, cols(RWKV_WIDTH, p)] = (
                u[rs] * (mixed + bs_ref[:, cols(0, p)])).astype(o_ref.dtype)
        yield mixed[:RWKV_CHUNK]


def _interleaved(*stage_generators):
    for handed in itertools.zip_longest(*stage_generators):
        values = [h for h in handed if h is not None]
        yield sum(values[1:], values[0]) if values else None


def _anchored(x, rider_value):
    if rider_value is None:
        return x
    bits = lax.bitcast_convert_type(rider_value.astype(F32), jnp.int32)
    zero = lax.shift_right_logical(lax.shift_right_logical(bits, 16), 16)
    return x + zero.astype(F32)


def _mixer_kernel(zr_ref, zg_ref, w0_ref, wdec_ref, a0_ref, waaa_ref, wgate_ref, kk_ref,
                  ka_ref, rk_ref, gng_ref, gnb_ref, lng_ref, lnb_ref, ws_ref, bs_ref,
                  o_ref, state_ref, wst_ref, opb_ref, opf_ref, dend_ref):
    s = pl.program_id(0)

    @pl.when(s == 0)
    def _():
        state_ref[...] = jnp.zeros_like(state_ref)
        opb_ref[...] = jnp.zeros_like(opb_ref)
        opf_ref[...] = jnp.zeros_like(opf_ref)
        dend_ref[...] = jnp.zeros_like(dend_ref)
        causal = _iota2((CHUNK, CHUNK), 0) >= _iota2((CHUNK, CHUNK), 1)
        for h in range(ws_ref.shape[0]):
            wst_ref[h // 2, :, (h % 2) * CHUNK:(h % 2 + 1) * CHUNK] = (
                jnp.where(causal, ws_ref[h], 0.0).astype(BF16))

    prepare = _rwkv_prepare_stages(zr_ref, w0_ref, wdec_ref, a0_ref, waaa_ref, wgate_ref, kk_ref,
                                   ka_ref, rk_ref, opb_ref, opf_ref, dend_ref, s % 2)
    gmlp = _gmlp_stages(zg_ref, lng_ref, lnb_ref, wst_ref, bs_ref, o_ref)
    filler = _interleaved(prepare, gmlp)
    _rwkv_chunks(opb_ref, opf_ref, dend_ref, (s + 1) % 2, gng_ref[...], gnb_ref[...],
                 state_ref, o_ref, filler)
    for _ in filler:
        pass


def _mixer(zr, zg, w0, wdec, a0, waaa, wgate, k_k, k_a, r_k, gn_g, gn_b, ln_g, ln_b, w_s, b_s):
    rows = MIX_CHUNKS * CHUNK
    n = zr.shape[0] // rows
    n_heads = w_s.shape[0]

    def full(arr):
        nd = arr.ndim
        return pl.BlockSpec(arr.shape, lambda s: (0,) * nd)

    params = (w0, wdec, a0, waaa, wgate, k_k, k_a, r_k, gn_g, gn_b, ln_g, ln_b, w_s, b_s)
    return pl.pallas_call(
        _mixer_kernel,
        out_shape=jax.ShapeDtypeStruct((n * rows, D_MODEL), BF16),
        grid=(n + 1,),
        in_specs=[pl.BlockSpec((rows, RWKV_COLS), lambda s: (jnp.minimum(s, n - 1), 0)),
                  pl.BlockSpec((rows, 2 * GMLP_WIDTH), lambda s: (jnp.maximum(s - 1, 0), 0))]
                 + [full(a) for a in params],
        out_specs=pl.BlockSpec((rows, D_MODEL), lambda s: (jnp.maximum(s - 1, 0), 0)),
        scratch_shapes=[
            pltpu.VMEM((N_PAIRS, LANES, LANES), F32),
            pltpu.VMEM((n_heads // 2, CHUNK, 2 * CHUNK), BF16),
            pltpu.VMEM((2, len(_OPS_B), rows, RWKV_WIDTH), BF16),
            pltpu.VMEM((2, len(_OPS_F), rows, RWKV_WIDTH), F32),
            pltpu.VMEM((2, rows // RWKV_CHUNK * SUBLANES, RWKV_WIDTH), F32),
        ],
        compiler_params=pltpu.CompilerParams(
            dimension_semantics=("arbitrary",), vmem_limit_bytes=VMEM_LIMIT),
        name="mixer",
    )(zr, zg, *params)


def kernel(x, p, norm_ffn1, ffn1_w1, ffn1_w3, ffn1_w2, norm_mix, w_in, shift_mu, rwkv_w0,
           rwkv_w_decay, rwkv_a0, rwkv_w_aaa, rwkv_w_gate, rwkv_k_k, rwkv_k_a, rwkv_r_k,
           rwkv_gn_g, rwkv_gn_b, sgu_ln_g, sgu_ln_b, sgu_w_s, sgu_b_s, w_out, norm_ffn2,
           ffn2_w1, ffn2_w3, ffn2_w2, norm_ple, w_ple_gate, w_ple, norm_final):
    assert x.shape[0] == 1 and p.shape[0] == 1, "one batch row and one layer, as the problem states"
    row = lambda a: a.reshape(1, -1)
    bf = lambda a: a.astype(BF16)
    h, zr, zg = _pre_mixer(
        x[0], [row(norm_ffn1[0]), bf(ffn1_w1[0]), bf(ffn1_w3[0]), bf(ffn1_w2[0]),
               row(norm_mix[0]), bf(w_in[0]), row(shift_mu[0])])
    lora = rwkv_w_decay.shape[1]
    pad = jnp.zeros((LANES - lora, RWKV_WIDTH), F32)
    wdec = jnp.concatenate([rwkv_w_decay[0], pad], axis=0)
    waaa = jnp.concatenate([pad, rwkv_w_aaa[0]], axis=0)
    bias = jnp.repeat(sgu_b_s[0].T, HEAD, axis=1)
    y = _mixer(zr, zg, row(rwkv_w0[0]), bf(wdec), row(rwkv_a0[0]), bf(waaa),
               bf(rwkv_w_gate[0]), row(rwkv_k_k[0]), row(rwkv_k_a[0]), row(rwkv_r_k[0]),
               row(rwkv_gn_g[0]), row(rwkv_gn_b[0]), row(sgu_ln_g[0]), row(sgu_ln_b[0]),
               sgu_w_s[0], bias)
    out = _post_mixer(
        h, y, p[0, 0],
        [bf(w_out[0]), row(norm_ffn2[0]), bf(ffn2_w1[0]), bf(ffn2_w3[0]), bf(ffn2_w2[0]),
         row(norm_ple[0]), bf(w_ple_gate[0]), bf(w_ple[0]), row(norm_final)])
    return out[None]
```

```python
import itertools

import jax
import jax.numpy as jnp
from jax import lax
from jax.experimental import pallas as pl
from jax.experimental.pallas import tpu as pltpu

F32 = jnp.float32
BF16 = jnp.bfloat16

D_MODEL = 1024
D_FF = 2816
RWKV_WIDTH = 512
RWKV_COLS = 1792
GMLP_WIDTH = 512
HEAD = 64
LANES = 128
SUBLANES = 8
N_PAIRS = RWKV_WIDTH // LANES
CHUNK = 128
RWKV_CHUNK = 64
MIX_CHUNKS = 4
RMS_EPS = 1e-6
LN_EPS = 1e-5
GN_EPS = 64e-5
VMEM_LIMIT = 56 * 1024 * 1024

_NN = (((1,), (0,)), ((), ()))
_NT = (((1,), (1,)), ((), ()))
_TN = (((0,), (0,)), ((), ()))


def _dot(a, b, dn=_NN):
    return lax.dot_general(a, b, dn, preferred_element_type=F32)


def _split2(x):
    hi = x.astype(BF16)
    lo = (x - hi.astype(F32)).astype(BF16)
    return hi, lo


def _hilo_cols(x):
    return jnp.concatenate(_split2(x), axis=1)


def _hilo_rows(x):
    return jnp.concatenate(_split2(x), axis=0)


def _rms(x, g):
    ms = jnp.mean(x * x, axis=-1, keepdims=True)
    return x * lax.rsqrt(ms + RMS_EPS) * g


FF_CHUNK = 256


def _swiglu_hidden(xn, w1_ref, w3_ref, hm_ref, filler):
    for j in range(hm_ref.shape[1] // FF_CHUNK):
        cols = slice(j * FF_CHUNK, (j + 1) * FF_CHUNK)
        a = _dot(xn, w1_ref[:, cols])
        b = _dot(xn, w3_ref[:, cols])
        hm_ref[:, cols] = (a * jax.nn.sigmoid(a) * b).astype(BF16)
        next(filler, None)


def _rms_stages(x_ref, g_ref, out_ref, slot, rows_per_stage=64):
    for start in range(0, x_ref.shape[0], rows_per_stage):
        rs = slice(start, start + rows_per_stage)
        out_ref[slot, rs, :] = _rms(x_ref[rs, :], g_ref[...]).astype(BF16)
        yield


def _pre_mixer_kernel(x_ref, xnext_ref, g1_ref, w1_ref, w3_ref, w2_ref, gm_ref, win_ref, mu_ref,
                      h_ref, zr_ref, zg_ref, hm_ref, xn1_ref, xn2_ref, carry_ref):
    i = pl.program_id(0)
    last = pl.num_programs(0) - 1

    def project(slot):
        z = _dot(xn2_ref[slot], win_ref[...])
        rows = z.shape[0]
        first_row = _iota2((rows, LANES), 0) == 0
        for col in range(0, RWKV_COLS, LANES):
            cols = slice(col, col + LANES)
            zc = z[:, cols]
            zp = jnp.where(first_row, carry_ref[:, cols], pltpu.roll(zc, 1, 0))
            carry_ref[:, cols] = zc[rows - 1:rows, :]
            zr_ref[:, cols] = zc + (zp - zc) * mu_ref[:, cols]
        zg = z[:, RWKV_COLS:]
        zg_ref[...] = 0.5 * zg * (1.0 + lax.erf(zg * 0.7071067811865476))

    @pl.when(i == 0)
    def _():
        xn1_ref[0] = _rms(x_ref[...], g1_ref[...]).astype(BF16)
        xn2_ref[1] = jnp.zeros(xn2_ref.shape[1:], BF16)
        carry_ref[...] = jnp.zeros_like(carry_ref)

    @pl.when(i < last)
    def _():
        project((i + 1) % 2)
        filler = _rms_stages(xnext_ref, g1_ref, xn1_ref, (i + 1) % 2)
        _swiglu_hidden(xn1_ref[i % 2], w1_ref, w3_ref, hm_ref, filler)
        for _ in filler:
            pass
        acc = _dot(hm_ref[...], w2_ref[...])
        h = x_ref[...] + 0.5 * acc
        h_ref[...] = h
        xn2_ref[i % 2] = _rms(h, gm_ref[...]).astype(BF16)

    @pl.when(i == last)
    def _():
        project((i + 1) % 2)


def _post_mixer_kernel(h_ref, y_ref, p_ref, wout_ref, g2_ref, w1_ref, w3_ref, w2_ref,
                       gp_ref, wg_ref, wp_ref, gf_ref, o_ref, hm_ref):
    h = h_ref[...] + _dot(y_ref[...], wout_ref[...])
    _swiglu_hidden(_rms(h, g2_ref[...]).astype(BF16), w1_ref, w3_ref, hm_ref, iter(()))
    h = h + 0.5 * _dot(hm_ref[...], w2_ref[...])
    gate = jax.nn.sigmoid(_dot(_rms(h, gp_ref[...]).astype(BF16), wg_ref[...]))
    e = _dot(p_ref[...].astype(BF16), wp_ref[...])
    o_ref[...] = _rms(h + gate * e, gf_ref[...])


def _rows(tm, width, tile_of_step=lambda i: i):
    return pl.BlockSpec((tm, width), lambda i: (tile_of_step(i), 0))


def _resident(arr):
    nd = arr.ndim
    return pl.BlockSpec(arr.shape, lambda i: (0,) * nd, pipeline_mode=pl.Buffered(1))


def _pre_mixer(x, params, *, tm=512):
    t, d = x.shape
    n = t // tm
    this = lambda i: jnp.minimum(i, n - 1)
    nxt = lambda i: jnp.minimum(i + 1, n - 1)
    prev = lambda i: jnp.maximum(i - 1, 0)
    return pl.pallas_call(
        _pre_mixer_kernel,
        out_shape=(jax.ShapeDtypeStruct((t, d), F32),
                   jax.ShapeDtypeStruct((t, RWKV_COLS), F32),
                   jax.ShapeDtypeStruct((t, 2 * GMLP_WIDTH), F32)),
        grid=(n + 1,),
        in_specs=[_rows(tm, d, this), _rows(tm, d, nxt)] + [_resident(a) for a in params],
        out_specs=(_rows(tm, d, this), _rows(tm, RWKV_COLS, prev),
                   _rows(tm, 2 * GMLP_WIDTH, prev)),
        scratch_shapes=[pltpu.VMEM((tm, D_FF), BF16),
                        pltpu.VMEM((2, tm, d), BF16),
                        pltpu.VMEM((2, tm, d), BF16),
                        pltpu.VMEM((1, RWKV_COLS), F32)],
        compiler_params=pltpu.CompilerParams(
            dimension_semantics=("arbitrary",), vmem_limit_bytes=VMEM_LIMIT),
        name="pre_mixer",
    )(x, x, *params)


def _post_mixer(h, y, p, params, *, tm=1024):
    t, d = h.shape
    return pl.pallas_call(
        _post_mixer_kernel,
        out_shape=jax.ShapeDtypeStruct((t, d), F32),
        grid=(t // tm,),
        in_specs=[_rows(tm, a.shape[1]) for a in (h, y, p)] + [_resident(a) for a in params],
        out_specs=_rows(tm, d),
        scratch_shapes=[pltpu.VMEM((tm, D_FF), BF16)],
        compiler_params=pltpu.CompilerParams(
            dimension_semantics=("arbitrary",), vmem_limit_bytes=VMEM_LIMIT),
        name="post_mixer",
    )(h, y, p, *params)


def _iota2(shape, dim):
    return lax.broadcasted_iota(jnp.int32, shape, dim)


def _stack_heads(x, first):
    zero = jnp.zeros_like(x)
    return jnp.concatenate([jnp.where(first, x, zero), jnp.where(first, zero, x)], axis=0)


def _unit_lower_inverses(mats, c, filler):
    row = _iota2((c, 2 * c), 0)
    lane = _iota2((c, 2 * c), 1)
    col = lane & (c - 1)
    left = lane < c
    eye = jnp.where(row == col, 1.0, 0.0)

    def same_block(b):
        s = b.bit_length() - 1
        return (row >> s) == (col >> s)

    def bd(x):
        zero = jnp.zeros_like(x)
        return jnp.concatenate([jnp.where(left, x, zero), jnp.where(left, zero, x)], axis=0)

    ts = [(eye + jnp.where(same_block(2), a, 0.0)).astype(BF16) for a in mats]
    anchor = None
    anchored_tiles = range(0, len(mats), 2 * N_PAIRS)
    b = 2
    while b < c:
        off = same_block(2 * b) & jnp.logical_not(same_block(b))
        tas = [_dot(t, bd(jnp.where(off, a, 0.0))) for t, a in zip(ts, mats)]
        for i in anchored_tiles:
            tas[i] = _anchored(tas[i], anchor)
        tas = [ta.astype(BF16) for ta in tas]
        anchor = next(filler, None)
        new = [t + _dot(ta, bd(t)) for t, ta in zip(ts, tas)]
        for i in anchored_tiles:
            new[i] = _anchored(new[i], anchor)
        ts = [t.astype(BF16) for t in new]
        anchor = next(filler, None)
        b *= 2
    return ts


_OPS_B = ("rt", "at", "bt", "kt", "r0", "a0", "bh", "kh", "vb", "rkk")
_OPS_F = ("v", "g")


def _rwkv_prepare_stages(zr_ref, w0_ref, wdec_ref, a0_ref, waaa_ref,
                         wgate_ref, kk_ref, ka_ref, rk_ref, opb_ref, opf_ref, dend_ref, slot):
    c = RWKV_CHUNK
    rows = zr_ref.shape[0]
    chunks = [slice(j * c, (j + 1) * c) for j in range(rows // c)]
    shifted = lambda col: zr_ref[:, col:col + LANES]

    w = RWKV_WIDTH
    zwa = shifted(3 * w)
    zwa_tanh = jnp.tanh(zwa).astype(BF16)
    zwa_b = zwa.astype(BF16)
    zg_sig = jax.nn.sigmoid(shifted(3 * w + LANES)).astype(BF16)
    blockdiag = (_iota2((LANES, LANES), 0) >= HEAD) == (_iota2((LANES, LANES), 1) >= HEAD)
    ones_bd = blockdiag.astype(BF16)
    row2 = _iota2((c, 2 * c), 0)
    col2 = _iota2((c, 2 * c), 1) & (c - 1)
    tril_2 = (row2 >= col2).astype(BF16)
    yield
    for p in range(N_PAIRS):
        sl = slice(p * LANES, (p + 1) * LANES)
        r, k, v = shifted(p * LANES), shifted(w + p * LANES), shifted(2 * w + p * LANES)
        lw = w0_ref[:, sl] + _dot(zwa_tanh, wdec_ref[:, sl])
        logd = -0.6065306597126334 * jax.nn.sigmoid(lw)
        a = jax.nn.sigmoid(a0_ref[:, sl] + _dot(zwa_b, waaa_ref[:, sl]))
        g = _dot(zg_sig, wgate_ref[:, sl])
        kk = k * kk_ref[:, sl]
        n2 = _dot((kk * kk).astype(BF16), ones_bd)
        kkn = kk * lax.rsqrt(jnp.maximum(n2, 1e-24))
        k2 = k * (1.0 + (a - 1.0) * ka_ref[:, sl])
        bvec = kkn * a
        opb_ref[slot, _OPS_B.index("rkk"), :, sl] = (r * k2 * rk_ref[:, sl]).astype(BF16)
        opf_ref[slot, _OPS_F.index("v"), :, sl] = v
        opf_ref[slot, _OPS_F.index("g"), :, sl] = g
        yield (r * k2)[:c]
        for j, rs in enumerate(chunks):
            lcum = _dot(tril_2, _hilo_rows(logd[rs]))
            lmid = lcum[c // 2 - 1:c // 2, :]
            lend = lcum[c - 1:c, :]
            lexc = lcum - logd[rs]
            e_mid_from = jnp.exp(lmid - lcum)
            e_end = jnp.exp(lend - lcum)
            ops = dict(
                rt=r[rs] * jnp.exp(lcum - lmid), at=-kkn[rs] * jnp.exp(lexc - lmid),
                bt=bvec[rs] * e_mid_from, kt=k2[rs] * e_mid_from,
                r0=r[rs] * jnp.exp(lcum), a0=-kkn[rs] * jnp.exp(lexc),
                bh=bvec[rs] * e_end, kh=k2[rs] * e_end, vb=v[rs])
            for name, value in ops.items():
                opb_ref[slot, _OPS_B.index(name), rs, sl] = value.astype(BF16)
            dend_ref[slot, j * SUBLANES:(j + 1) * SUBLANES, sl] = jnp.broadcast_to(
                jnp.exp(lend), (SUBLANES, LANES))
        yield ops["bh"] + ops["rt"]


def _rwkv_chunks(opb_ref, opf_ref, dend_ref, slot, gn_g, gn_b, state_ref, o_ref, filler):
    c = RWKV_CHUNK
    row2 = _iota2((c, 2 * c), 0)
    col2 = _iota2((c, 2 * c), 1) & (c - 1)
    incl2 = row2 >= col2
    strict2 = row2 > col2
    first = _iota2((c, LANES), 1) < HEAD
    blockdiag = (_iota2((LANES, LANES), 0) >= HEAD) == (_iota2((LANES, LANES), 1) >= HEAD)
    ones_bd = blockdiag.astype(BF16)
    avg_bd = (blockdiag.astype(F32) * (1.0 / HEAD)).astype(BF16)
    zero_bd = jnp.zeros((LANES, LANES), BF16)
    avg_2 = jnp.concatenate([avg_bd, avg_bd], axis=0)
    stat_bd = jnp.concatenate([jnp.concatenate([avg_bd, zero_bd], axis=1),
                               jnp.concatenate([zero_bd, ones_bd], axis=1)], axis=0)
    pairs = range(N_PAIRS)
    sls = [slice(p * LANES, (p + 1) * LANES) for p in pairs]
    chunks = [slice(j * c, (j + 1) * c) for j in range(opb_ref.shape[2] // c)]
    tiles = [(j, rs, p, sl) for j, rs in enumerate(chunks) for p, sl in enumerate(sls)]
    opb = lambda name, rs, sl: opb_ref[slot, _OPS_B.index(name), rs, sl]
    opf = lambda name, rs, sl: opf_ref[slot, _OPS_F.index(name), rs, sl]

    grams = [
        _dot(jnp.concatenate([opb("at", rs, sl), opb("rt", rs, sl)], axis=0),
             jnp.concatenate([_stack_heads(opb("bt", rs, sl), first),
                              _stack_heads(opb("kt", rs, sl), first)], axis=0), _NT)
        for _, rs, _, sl in tiles]
    a_abs = [jnp.where(strict2, gm[:c, :2 * c], 0.0).astype(BF16) for gm in grams]
    t_invs = _unit_lower_inverses(a_abs, c, filler)
    a_rbs = [jnp.where(incl2, gm[c:, :2 * c], 0.0).astype(BF16) for gm in grams]
    akvs = [
        _dot(jnp.concatenate([jnp.where(strict2, gm[:c, 2 * c:], 0.0),
                              jnp.where(incl2, gm[c:, 2 * c:], 0.0)], axis=0).astype(BF16),
             _stack_heads(opb("vb", rs, sl), first))
        for gm, (_, rs, _, sl) in zip(grams, tiles)]

    ta0s = [_dot(t, _stack_heads(opb("a0", rs, sl), first)).astype(BF16)
            for t, (_, rs, _, sl) in zip(t_invs, tiles)]
    tavs = [_dot(t, _stack_heads(akv[:c].astype(BF16), first)).astype(BF16)
            for t, akv in zip(t_invs, akvs)]
    anchor_a = next(filler, None)
    m_offs = [jnp.where(blockdiag, _dot(ta0, opb("bh", rs, sl), _TN), 0.0).astype(BF16)
              for ta0, (_, rs, _, sl) in zip(ta0s, tiles)]
    consts = [jnp.where(blockdiag,
                        _dot(jnp.concatenate([tav, opb("vb", rs, sl)], axis=0),
                             jnp.concatenate([opb("bh", rs, sl), opb("kh", rs, sl)], axis=0), _TN),
                        0.0)
              for tav, (_, rs, _, sl) in zip(tavs, tiles)]
    anchor_b = next(filler, None)
    ras = [(opb("r0", rs, sl).astype(F32) + _dot(a_rb, _stack_heads(ta0, first))).astype(BF16)
           for a_rb, ta0, (_, rs, _, sl) in zip(a_rbs, ta0s, tiles)]
    y_consts = [akv[c:] + _dot(a_rb, _stack_heads(tav, first))
                for akv, a_rb, tav in zip(akvs, a_rbs, tavs)]
    y_consts[0] = _anchored(_anchored(y_consts[0], anchor_a), anchor_b)
    next(filler, None)

    states = [state_ref[p] for p in pairs]
    ys = []
    for j in range(len(chunks)):
        mine = slice(j * N_PAIRS, (j + 1) * N_PAIRS)
        s_bs = [s.astype(BF16) for s in states]
        ys += [_dot(ra, s_b, _NT) + yc for ra, yc, s_b in zip(ras[mine], y_consts[mine], s_bs)]
        states = [s * dend_ref[slot, j * SUBLANES:j * SUBLANES + 1, sl] + _dot(s_b, m) + cm
                  for s, s_b, m, cm, sl in zip(states, s_bs, m_offs[mine], consts[mine], sls)]
    for p, s in enumerate(states):
        state_ref[p] = s

    means = [_dot(_hilo_cols(y), avg_2) for y in ys]
    ycs = [y - m for y, m in zip(ys, means)]
    stats = [_dot(jnp.concatenate([(yc * yc).astype(BF16), opb("rkk", rs, sl)], axis=1), stat_bd)
             for yc, (_, rs, _, sl) in zip(ycs, tiles)]
    for yc, st, (_, rs, _, sl) in zip(ycs, stats, tiles):
        o_ref[rs, sl] = ((yc * lax.rsqrt(st[:, :LANES] + GN_EPS) * gn_g[:, sl] + gn_b[:, sl]
                          + st[:, LANES:] * opf("v", rs, sl)) * opf("g", rs, sl)
                         ).astype(o_ref.dtype)


def _gmlp_stages(zg_ref, lng_ref, lnb_ref, wst_ref, bs_ref, o_ref):
    n_tiles = GMLP_WIDTH // LANES
    cols = lambda base, p: slice(base + p * LANES, base + (p + 1) * LANES)
    rows = zg_ref.shape[0]
    chunks = [slice(j * CHUNK, (j + 1) * CHUNK) for j in range(rows // CHUNK)]
    vs = [zg_ref[:, cols(GMLP_WIDTH, p)] for p in range(n_tiles)]
    m = sum(jnp.sum(t, axis=-1, keepdims=True) for t in vs) * (1.0 / GMLP_WIDTH)
    vcs = [t - m for t in vs]
    var = sum(jnp.sum(t * t, axis=-1, keepdims=True) for t in vcs) * (1.0 / GMLP_WIDTH)
    inv = lax.rsqrt(var + LN_EPS)
    yield
    first = _iota2((CHUNK, LANES), 1) < HEAD
    for p in range(n_tiles):
        vn = (vcs[p] * inv * lng_ref[:, cols(0, p)] + lnb_ref[:, cols(0, p)]).astype(BF16)
        u = zg_ref[:, cols(0, p)]
        for rs in chunks:
            mixed = _dot(wst_ref[p], _stack_heads(vn[rs], first))
            o_ref[rs, cols(RWKV_WIDTH, p)] = (
                u[rs] * (mixed + bs_ref[:, cols(0, p)])).astype(o_ref.dtype)
        yield mixed[:RWKV_CHUNK]


def _interleaved(*stage_generators):
    for handed in itertools.zip_longest(*stage_generators):
        values = [h for h in handed if h is not None]
        yield sum(values[1:], values[0]) if values else None


def _anchored(x, rider_value):
    if rider_value is None:
        return x
    bits = lax.bitcast_convert_type(rider_value.astype(F32), jnp.int32)
    zero = lax.shift_right_logical(lax.shift_right_logical(bits, 16), 16)
    return x + zero.astype(F32)


def _mixer_kernel(zr_ref, zg_ref, w0_ref, wdec_ref, a0_ref, waaa_ref, wgate_ref, kk_ref,
                  ka_ref, rk_ref, gng_ref, gnb_ref, lng_ref, lnb_ref, ws_ref, bs_ref,
                  o_ref, state_ref, wst_ref, opb_ref, opf_ref, dend_ref):
    s = pl.program_id(0)

    @pl.when(s == 0)
    def _():
        state_ref[...] = jnp.zeros_like(state_ref)
        opb_ref[...] = jnp.zeros_like(opb_ref)
        opf_ref[...] = jnp.zeros_like(opf_ref)
        dend_ref[...] = jnp.zeros_like(dend_ref)
        causal = _iota2((CHUNK, CHUNK), 0) >= _iota2((CHUNK, CHUNK), 1)
        for h in range(ws_ref.shape[0]):
            wst_ref[h // 2, :, (h % 2) * CHUNK:(h % 2 + 1) * CHUNK] = (
                jnp.where(causal, ws_ref[h], 0.0).astype(BF16))

    prepare = _rwkv_prepare_stages(zr_ref, w0_ref, wdec_ref, a0_ref, waaa_ref, wgate_ref, kk_ref,
                                   ka_ref, rk_ref, opb_ref, opf_ref, dend_ref, s % 2)
    gmlp = _gmlp_stages(zg_ref, lng_ref, lnb_ref, wst_ref, bs_ref, o_ref)
    filler = _interleaved(prepare, gmlp)
    _rwkv_chunks(opb_ref, opf_ref, dend_ref, (s + 1) % 2, gng_ref[...], gnb_ref[...],
                 state_ref, o_ref, filler)
    for _ in filler:
        pass


def _mixer(zr, zg, w0, wdec, a0, waaa, wgate, k_k, k_a, r_k, gn_g, gn_b, ln_g, ln_b, w_s, b_s):
    rows = MIX_CHUNKS * CHUNK
    n = zr.shape[0] // rows
    n_heads = w_s.shape[0]

    def full(arr):
        nd = arr.ndim
        return pl.BlockSpec(arr.shape, lambda s: (0,) * nd)

    params = (w0, wdec, a0, waaa, wgate, k_k, k_a, r_k, gn_g, gn_b, ln_g, ln_b, w_s, b_s)
    return pl.pallas_call(
        _mixer_kernel,
        out_shape=jax.ShapeDtypeStruct((n * rows, D_MODEL), BF16),
        grid=(n + 1,),
        in_specs=[pl.BlockSpec((rows, RWKV_COLS), lambda s: (jnp.minimum(s, n - 1), 0)),
                  pl.BlockSpec((rows, 2 * GMLP_WIDTH), lambda s: (jnp.maximum(s - 1, 0), 0))]
                 + [full(a) for a in params],
        out_specs=pl.BlockSpec((rows, D_MODEL), lambda s: (jnp.maximum(s - 1, 0), 0)),
        scratch_shapes=[
            pltpu.VMEM((N_PAIRS, LANES, LANES), F32),
            pltpu.VMEM((n_heads // 2, CHUNK, 2 * CHUNK), BF16),
            pltpu.VMEM((2, len(_OPS_B), rows, RWKV_WIDTH), BF16),
            pltpu.VMEM((2, len(_OPS_F), rows, RWKV_WIDTH), F32),
            pltpu.VMEM((2, rows // RWKV_CHUNK * SUBLANES, RWKV_WIDTH), F32),
        ],
        compiler_params=pltpu.CompilerParams(
            dimension_semantics=("arbitrary",), vmem_limit_bytes=VMEM_LIMIT),
        name="mixer",
    )(zr, zg, *params)


def kernel(x, p, norm_ffn1, ffn1_w1, ffn1_w3, ffn1_w2, norm_mix, w_in, shift_mu, rwkv_w0,
           rwkv_w_decay, rwkv_a0, rwkv_w_aaa, rwkv_w_gate, rwkv_k_k, rwkv_k_a, rwkv_r_k,
           rwkv_gn_g, rwkv_gn_b, sgu_ln_g, sgu_ln_b, sgu_w_s, sgu_b_s, w_out, norm_ffn2,
           ffn2_w1, ffn2_w3, ffn2_w2, norm_ple, w_ple_gate, w_ple, norm_final):
    assert x.shape[0] == 1 and p.shape[0] == 1, "one batch row and one layer, as the problem states"
    row = lambda a: a.reshape(1, -1)
    bf = lambda a: a.astype(BF16)
    h, zr, zg = _pre_mixer(
        x[0], [row(norm_ffn1[0]), bf(ffn1_w1[0]), bf(ffn1_w3[0]), bf(ffn1_w2[0]),
               row(norm_mix[0]), bf(w_in[0]), row(shift_mu[0])])
    lora = rwkv_w_decay.shape[1]
    pad = jnp.zeros((LANES - lora, RWKV_WIDTH), F32)
    wdec = jnp.concatenate([rwkv_w_decay[0], pad], axis=0)
    waaa = jnp.concatenate([pad, rwkv_w_aaa[0]], axis=0)
    bias = jnp.repeat(sgu_b_s[0].T, HEAD, axis=1)
    y = _mixer(zr, zg, row(rwkv_w0[0]), bf(wdec), row(rwkv_a0[0]), bf(waaa),
               bf(rwkv_w_gate[0]), row(rwkv_k_k[0]), row(rwkv_k_a[0]), row(rwkv_r_k[0]),
               row(rwkv_gn_g[0]), row(rwkv_gn_b[0]), row(sgu_ln_g[0]), row(sgu_ln_b[0]),
               sgu_w_s[0], bias)
    out = _post_mixer(
        h, y, p[0, 0],
        [bf(w_out[0]), row(norm_ffn2[0]), bf(ffn2_w1[0]), bf(ffn2_w3[0]), bf(ffn2_w2[0]),
         row(norm_ple[0]), bf(w_ple_gate[0]), bf(w_ple[0]), row(norm_final)])
    return out[None]
```
